```python
import jax, jax.numpy as jnp
from jax import lax
import numpy as np

D_MODEL = 2048
BATCH = 4
SEQ = 2048
DEPTH = 1
DEC_BATCH = 128
DEC_SEQ = 8
PAST_LEN = 8192
PAGE_SIZE = 128

HEAD_DIM = 64
N_Q_HEADS = 16
N_KV_HEADS = 4
Q_PER_KV = N_Q_HEADS // N_KV_HEADS
WINDOW = 128
ATTN_Q_WIDTH = N_Q_HEADS * HEAD_DIM
ATTN_KV_WIDTH = N_KV_HEADS * HEAD_DIM
CHUNK = 128
GMLP_GROUPS = 8
GMLP_GROUP_DIM = 128
GMLP_WIDTH = GMLP_GROUPS * GMLP_GROUP_DIM
D_FF = -(-8 * D_MODEL // (3 * 256)) * 256
IN_WIDTH = ATTN_Q_WIDTH + 2 * ATTN_KV_WIDTH + 2 * GMLP_WIDTH + 2 * D_MODEL
EPS = 1e-6
NEG = -1e30

kernel_name = "gated_parallel_swa_sink_chunk_gmlp_decoder_step"


def rms_norm(x, g):
    xf = x.astype(jnp.float32)
    y = xf * lax.rsqrt(jnp.mean(xf * xf, axis=-1, keepdims=True) + EPS)
    return (y * g.astype(jnp.float32)).astype(x.dtype)


def layer_norm_nobias(x, g):
    xf = x.astype(jnp.float32)
    xc = xf - jnp.mean(xf, axis=-1, keepdims=True)
    y = xc * lax.rsqrt(jnp.mean(xc * xc, axis=-1, keepdims=True) + EPS)
    return (y * g.astype(jnp.float32)).astype(x.dtype)


def split_in(z):
    o0 = ATTN_Q_WIDTH
    o1 = o0 + ATTN_KV_WIDTH
    o2 = o1 + ATTN_KV_WIDTH
    o3 = o2 + GMLP_WIDTH
    o4 = o3 + GMLP_WIDTH
    o5 = o4 + D_MODEL
    return (z[..., :o0], z[..., o0:o1], z[..., o1:o2], z[..., o2:o3],
            z[..., o3:o4], z[..., o4:o5], z[..., o5:])


def sink_softmax(scores, mask, sink):
    s = jnp.where(mask, scores, NEG)
    m = jnp.maximum(jnp.max(s, axis=-1, keepdims=True), sink)
    p = jnp.exp(s - m)
    return p / (jnp.sum(p, axis=-1, keepdims=True) + jnp.exp(sink - m))


def chunk_spatial_gate(u, v, w_s, b_s):
    B, L = u.shape[0], u.shape[1]
    vc = v.reshape(B, L // CHUNK, CHUNK, GMLP_GROUPS, GMLP_GROUP_DIM)
    w = w_s * jnp.tril(jnp.ones((CHUNK, CHUNK), w_s.dtype))[None]
    mixed = jnp.einsum('gts,bcsgd->bctgd', w, vc) + b_s.T[None, None, :, :, None]
    return u * mixed.reshape(B, L, GMLP_GROUPS, GMLP_GROUP_DIM)


def branch_inputs(x, norm1_g, w_in, gmlp_norm_g):
    h = rms_norm(x, norm1_g)
    q, k, v, u, gv, ga, gb = split_in(h @ w_in)
    B, L = x.shape[0], x.shape[1]
    q = q.reshape(B, L, N_KV_HEADS, Q_PER_KV, HEAD_DIM)
    k = k.reshape(B, L, N_KV_HEADS, HEAD_DIM)
    v = v.reshape(B, L, N_KV_HEADS, HEAD_DIM)
    u = jax.nn.gelu(u).reshape(B, L, GMLP_GROUPS, GMLP_GROUP_DIM)
    gv = layer_norm_nobias(jax.nn.gelu(gv), gmlp_norm_g).reshape(B, L, GMLP_GROUPS, GMLP_GROUP_DIM)
    return q, k, v, u, gv, ga, gb


def merge_and_ffn(x, attn, gmlp, ga, gb, w_pa, w_pb, w_o, norm2_g, w_ff_gate, w_ff_up, w_ff_down):
    mix = jax.nn.sigmoid(ga) * (gmlp @ w_pa) + jax.nn.sigmoid(gb) * (attn @ w_pb)
    x = x + mix @ w_o
    hh = rms_norm(x, norm2_g)
    return x + (jax.nn.silu(hh @ w_ff_gate) * (hh @ w_ff_up)) @ w_ff_down


def prompt_attention(q, k, v, sinks):
    B, S = q.shape[0], q.shape[1]
    nb = S // WINDOW
    qb = q.reshape(B, nb, WINDOW, N_KV_HEADS, Q_PER_KV, HEAD_DIM).astype(jnp.float32)
    kb = k.reshape(B, nb, WINDOW, N_KV_HEADS, HEAD_DIM)
    vb = v.reshape(B, nb, WINDOW, N_KV_HEADS, HEAD_DIM)
    kk = jnp.concatenate([jnp.concatenate([jnp.zeros_like(kb[:, :1]), kb[:, :-1]], 1), kb], 2)
    vv = jnp.concatenate([jnp.concatenate([jnp.zeros_like(vb[:, :1]), vb[:, :-1]], 1), vb], 2)
    scores = jnp.einsum('bnqhrd,bnkhd->bnhrqk', qb, kk.astype(jnp.float32)) * (HEAD_DIM ** -0.5)
    i = jnp.arange(WINDOW)[:, None]
    j = jnp.arange(2 * WINDOW)[None, :]
    diff = i + WINDOW - j
    c = jnp.arange(nb)[:, None, None]
    mask = (diff >= 0)[None] & (diff <= WINDOW)[None] & (c * WINDOW + j - WINDOW >= 0)
    sink = sinks.astype(jnp.float32).reshape(1, 1, N_KV_HEADS, Q_PER_KV, 1, 1)
    probs = sink_softmax(scores, mask[None, :, None, None], sink)
    out = jnp.einsum('bnhrqk,bnkhd->bnqhrd', probs.astype(v.dtype), vv)
    return out.reshape(B, S, ATTN_Q_WIDTH)


def sample_attention(q, k, v, cache_k, cache_v, sinks):
    B, T = q.shape[0], q.shape[1]
    wc = cache_k.shape[1]
    kk = jnp.concatenate([cache_k, k], 1)
    vv = jnp.concatenate([cache_v, v], 1)
    scores = jnp.einsum('bqhrd,bkhd->bhrqk', q.astype(jnp.float32), kk.astype(jnp.float32)) * (HEAD_DIM ** -0.5)
    i = jnp.arange(T)[:, None]
    j = jnp.arange(wc + T)[None, :]
    diff = wc + i - j
    mask = (diff >= 0) & (diff <= WINDOW)
    sink = sinks.astype(jnp.float32).reshape(1, N_KV_HEADS, Q_PER_KV, 1, 1)
    probs = sink_softmax(scores, mask, sink)
    out = jnp.einsum('bhrqk,bkhd->bqhrd', probs.astype(v.dtype), vv)
    return out.reshape(B, T, ATTN_Q_WIDTH), kk[:, -wc:], vv[:, -wc:]


def setup_inputs(seed: int = 0) -> dict:
    key = jax.random.key(seed)
    ks = jax.random.split(key, 20)
    f32 = jnp.float32
    n = lambda k, shape, s: jax.random.normal(k, shape, f32) * s
    cache_w = min(WINDOW, PAST_LEN)
    return {
        "x_prompt": n(ks[0], (BATCH, SEQ, D_MODEL), 1.0),
        "x_sample": n(ks[1], (DEC_BATCH, DEC_SEQ, D_MODEL), 1.0),
        "cache_k": n(ks[2], (DEPTH, DEC_BATCH, cache_w, N_KV_HEADS, HEAD_DIM), 1.0),
        "cache_v": n(ks[3], (DEPTH, DEC_BATCH, cache_w, N_KV_HEADS, HEAD_DIM), 1.0),
        "norm1_g": 1.0 + n(ks[4], (DEPTH, D_MODEL), 0.02),
        "w_in": n(ks[5], (DEPTH, D_MODEL, IN_WIDTH), D_MODEL ** -0.5),
        "gmlp_norm_g": 1.0 + n(ks[6], (DEPTH, GMLP_WIDTH), 0.02),
        "w_s": n(ks[7], (DEPTH, GMLP_GROUPS, CHUNK, CHUNK), CHUNK ** -0.5),
        "b_s": n(ks[8], (DEPTH, GMLP_GROUPS, CHUNK), 0.02),
        "sinks": n(ks[9], (DEPTH, N_Q_HEADS), 0.5),
        "w_pa": n(ks[10], (DEPTH, GMLP_WIDTH, D_MODEL), GMLP_WIDTH ** -0.5),
        "w_pb": n(ks[11], (DEPTH, ATTN_Q_WIDTH, D_MODEL), ATTN_Q_WIDTH ** -0.5),
        "w_o": n(ks[12], (DEPTH, D_MODEL, D_MODEL), D_MODEL ** -0.5),
        "norm2_g": 1.0 + n(ks[13], (DEPTH, D_MODEL), 0.02),
        "w_ff_gate": n(ks[14], (DEPTH, D_MODEL, D_FF), D_MODEL ** -0.5),
        "w_ff_up": n(ks[15], (DEPTH, D_MODEL, D_FF), D_MODEL ** -0.5),
        "w_ff_down": n(ks[16], (DEPTH, D_FF, D_MODEL), D_FF ** -0.5),
        "final_g": 1.0 + n(ks[17], (D_MODEL,), 0.02),
    }


def reference(x_prompt, x_sample, cache_k, cache_v, norm1_g, w_in, gmlp_norm_g, w_s, b_s, sinks,
              w_pa, w_pb, w_o, norm2_g, w_ff_gate, w_ff_up, w_ff_down, final_g):
    xp, xs = x_prompt, x_sample
    pk_list, pv_list, sk_list, sv_list, sg_list = [], [], [], [], []
    for l in range(DEPTH):
        B, S = xp.shape[0], xp.shape[1]
        q, k, v, u, gv, ga, gb = branch_inputs(xp, norm1_g[l], w_in[l], gmlp_norm_g[l])
        attn = prompt_attention(q, k, v, sinks[l])
        gm = chunk_spatial_gate(u, gv, w_s[l], b_s[l]).reshape(B, S, GMLP_WIDTH)
        xp = merge_and_ffn(xp, attn, gm, ga, gb, w_pa[l], w_pb[l], w_o[l], norm2_g[l],
                           w_ff_gate[l], w_ff_up[l], w_ff_down[l])
        pk_list.append(k[:, -WINDOW:])
        pv_list.append(v[:, -WINDOW:])

        Bd, T = xs.shape[0], xs.shape[1]
        q, k, v, u, gv, ga, gb = branch_inputs(xs, norm1_g[l], w_in[l], gmlp_norm_g[l])
        attn, k_win, v_win = sample_attention(q, k, v, cache_k[l], cache_v[l], sinks[l])
        pad = (-T) % CHUNK
        padw = ((0, 0), (0, pad), (0, 0), (0, 0))
        gm = chunk_spatial_gate(jnp.pad(u, padw), jnp.pad(gv, padw), w_s[l], b_s[l])[:, :T]
        xs = merge_and_ffn(xs, attn, gm.reshape(Bd, T, GMLP_WIDTH), ga, gb, w_pa[l], w_pb[l], w_o[l],
                           norm2_g[l], w_ff_gate[l], w_ff_up[l], w_ff_down[l])
        sk_list.append(k_win)
        sv_list.append(v_win)
        sg_list.append(gv)

    y_prompt = rms_norm(xp, final_g)
    y_sample = rms_norm(xs, final_g)
    prompt_k_win = jnp.stack(pk_list, 0)
    prompt_v_win = jnp.stack(pv_list, 0)
    sample_k_win = jnp.stack(sk_list, 0)
    sample_v_win = jnp.stack(sv_list, 0)
    sample_gmlp_v = jnp.stack(sg_list, 0)
    return (y_prompt, y_sample, prompt_k_win, prompt_v_win, sample_k_win, sample_v_win, sample_gmlp_v)
```

```python
import functools

import jax
import jax.numpy as jnp
from jax import lax
from jax.experimental import pallas as pl
from jax.experimental.pallas import tpu as pltpu

F32 = jnp.float32
BF16 = jnp.bfloat16

D_MODEL = 2048
BATCH = 4
SEQ = 2048
DEC_BATCH = 128
DEC_SEQ = 8
HEAD_DIM = 64
N_Q_HEADS = 16
N_KV_HEADS = 4
Q_PER_KV = N_Q_HEADS // N_KV_HEADS
WINDOW = 128
CHUNK = 128
GROUPS = 8
GROUP_DIM = 128
Q_W = N_Q_HEADS * HEAD_DIM
KV_W = N_KV_HEADS * HEAD_DIM
GMLP_W = GROUPS * GROUP_DIM
D_FF = 5632
IN_W = Q_W + 2 * KV_W + 2 * GMLP_W + 2 * D_MODEL
EPS = 1e-6
NEG = -1e30

N_PROMPT = BATCH * SEQ
N_SAMPLE = DEC_BATCH * DEC_SEQ
N_TOK = N_PROMPT + N_SAMPLE

Z_W = 2 * D_MODEL + Q_W + 2 * GMLP_W
Z_GA, Z_GB, Z_Q, Z_U, Z_GV = 0, 2048, 4096, 5120, 6144

MIB = 1024 * 1024


def _rms(x, g):
    return x * lax.rsqrt(jnp.mean(x * x, axis=-1, keepdims=True) + EPS) * g


def _dot(a, b):
    return jnp.dot(a, b, preferred_element_type=F32)


def _dot_t(a, b):
    return lax.dot_general(a, b, (((1,), (1,)), ((), ())), preferred_element_type=F32)


IP_TM = 512
IP_TN = 512
IP_NPT = N_PROMPT // IP_TM
IP_NT = N_TOK // IP_TM
IP_NJ = IN_W // IP_TN


def _inproj_zcol(j):
    return jnp.where(j < 2, 8 + j, jnp.where(j == 2, 9, jnp.where(j < 7, j + 7, j - 7)))


def _inproj_kernel(xp_ref, xs_ref, g_ref, w_ref, z_ref, kv_ref, h_ref):
    i = pl.program_id(0)
    j = pl.program_id(1)

    @pl.when((j == 0) & (i < IP_NPT))
    def _():
        h_ref[...] = _rms(xp_ref[...], g_ref[...]).astype(BF16)

    @pl.when((j == 0) & (i >= IP_NPT))
    def _():
        h_ref[...] = _rms(xs_ref[...], g_ref[...]).astype(BF16)

    acc = _dot(h_ref[...], w_ref[...])

    @pl.when(j < 2)
    def _():
        z_ref[...] = acc.astype(BF16)

    @pl.when(j == 2)
    def _():
        kv_ref[...] = acc

    @pl.when((j >= 3) & (j < 7))
    def _():
        z_ref[...] = jax.nn.gelu(acc).astype(BF16)

    @pl.when(j >= 7)
    def _():
        z_ref[...] = jax.nn.sigmoid(acc).astype(BF16)


def _inproj(xp, xs, g, w):
    return pl.pallas_call(
        _inproj_kernel,
        grid=(IP_NT, IP_NJ),
        in_specs=[
            pl.BlockSpec((IP_TM, D_MODEL), lambda i, j: (jnp.minimum(i, IP_NPT - 1), 0)),
            pl.BlockSpec((IP_TM, D_MODEL), lambda i, j: (jnp.maximum(i - IP_NPT, 0), 0)),
            pl.BlockSpec((1, D_MODEL), lambda i, j: (0, 0)),
            pl.BlockSpec((D_MODEL, IP_TN), lambda i, j: (0, j)),
        ],
        out_specs=[
            pl.BlockSpec((IP_TM, IP_TN), lambda i, j: (i, _inproj_zcol(j))),
            pl.BlockSpec((IP_TM, 2 * KV_W), lambda i, j: (i, 0)),
        ],
        out_shape=[
            jax.ShapeDtypeStruct((N_TOK, Z_W), BF16),
            jax.ShapeDtypeStruct((N_TOK, 2 * KV_W), F32),
        ],
        scratch_shapes=[pltpu.VMEM((IP_TM, D_MODEL), BF16)],
        compiler_params=pltpu.CompilerParams(
            dimension_semantics=("arbitrary", "arbitrary"), vmem_limit_bytes=40 * MIB),
        name="inproj",
    )(xp, xs, g, w)


NBLK = SEQ // WINDOW


def _softmax_sink_pv(s, mask, sink, v):
    s = jnp.where(mask, s, NEG)
    m = jnp.maximum(jnp.max(s, axis=-1, keepdims=True), sink)
    p = jnp.exp(s - m)
    denom = jnp.sum(p, axis=-1, keepdims=True) + jnp.exp(sink - m)
    return _dot(p.astype(BF16), v) / denom


def _attn_prompt_kernel(sink_ref, q_ref, kc_ref, kp_ref, o_ref):
    t = pl.program_id(0)
    not_first = (t % NBLK) != 0
    q = q_ref[...]
    kc = kc_ref[...]
    kp = kp_ref[...]
    k_all = jnp.concatenate([kp[:, :KV_W], kc[:, :KV_W]], axis=0).astype(BF16)
    v_all = jnp.concatenate([kp[:, KV_W:], kc[:, KV_W:]], axis=0).astype(BF16)
    row = lax.broadcasted_iota(jnp.int32, (WINDOW, 2 * WINDOW), 0)
    col = lax.broadcasted_iota(jnp.int32, (WINDOW, 2 * WINDOW), 1)
    mask = (col >= row) & (col <= row + WINDOW) & ((col >= WINDOW) | not_first)
    for h in range(N_KV_HEADS):
        kh = k_all[:, h * HEAD_DIM:(h + 1) * HEAD_DIM]
        vh = v_all[:, h * HEAD_DIM:(h + 1) * HEAD_DIM]
        for r in range(Q_PER_KV):
            hr = h * Q_PER_KV + r
            qh = q[:, hr * HEAD_DIM:(hr + 1) * HEAD_DIM]
            s = _dot_t(qh, kh) * (HEAD_DIM ** -0.5)
            o = _softmax_sink_pv(s, mask, sink_ref[hr], vh)
            o_ref[:, hr * HEAD_DIM:(hr + 1) * HEAD_DIM] = o.astype(BF16)


def _attn_prompt(sinks, z, kv):
    nblk = N_PROMPT // WINDOW
    return pl.pallas_call(
        _attn_prompt_kernel,
        grid=(nblk,),
        in_specs=[
            pl.BlockSpec(memory_space=pltpu.SMEM),
            pl.BlockSpec((WINDOW, Q_W), lambda t: (t, Z_Q // Q_W)),
            pl.BlockSpec((WINDOW, 2 * KV_W), lambda t: (t, 0)),
            pl.BlockSpec((WINDOW, 2 * KV_W), lambda t: (jnp.maximum(t - 1, 0), 0)),
        ],
        out_specs=pl.BlockSpec((WINDOW, Q_W), lambda t: (t, 0)),
        out_shape=jax.ShapeDtypeStruct((N_PROMPT, Q_W), BF16),
        compiler_params=pltpu.CompilerParams(
            dimension_semantics=("arbitrary",), vmem_limit_bytes=24 * MIB),
        name="attn_prompt",
    )(sinks, z, kv, kv)


SA_BT = 16
SA_ROWS = N_Q_HEADS * DEC_SEQ
SA_STEPS = DEC_BATCH // SA_BT


def _attn_sample_kernel(sink_ref, qs_ref, ck_ref, cv_ref, kvn_ref, o_ref, kw_ref, vw_ref):
    e_row = lax.broadcasted_iota(jnp.int32, (HEAD_DIM, KV_W), 0)
    e_col = lax.broadcasted_iota(jnp.int32, (HEAD_DIM, KV_W), 1)
    expand = ((e_col & (HEAD_DIM - 1)) == e_row).astype(BF16)
    r_i = lax.broadcasted_iota(jnp.int32, (SA_ROWS, KV_W), 0)
    c_i = lax.broadcasted_iota(jnp.int32, (SA_ROWS, KV_W), 1)
    head_mask = (r_i >> 5) == (c_i >> 6)
    rr = lax.broadcasted_iota(jnp.int32, (SA_ROWS, WINDOW), 0)
    cc = lax.broadcasted_iota(jnp.int32, (SA_ROWS, WINDOW), 1)
    tq = rr & (DEC_SEQ - 1)
    mask_c = cc >= tq
    sink = sink_ref[...]
    kvn = kvn_ref[...]
    k_new = kvn[:, :KV_W].astype(BF16)
    v_new = kvn[:, KV_W:].astype(BF16)
    scale = HEAD_DIM ** -0.5

    for b in range(SA_BT):
        rows = slice(b * SA_ROWS, (b + 1) * SA_ROWS)
        q_rep = _dot(qs_ref[rows, :], expand)
        q_exp = jnp.where(head_mask, q_rep, 0.0).astype(BF16)
        s_c = _dot_t(q_exp, ck_ref[b].astype(BF16)) * scale
        s_n = _dot_t(q_exp, k_new) * scale
        mask_n = ((cc >> 3) == b) & ((cc & (DEC_SEQ - 1)) <= tq)
        s_c = jnp.where(mask_c, s_c, NEG)
        s_n = jnp.where(mask_n, s_n, NEG)
        m = jnp.maximum(jnp.maximum(jnp.max(s_c, axis=-1, keepdims=True),
                                    jnp.max(s_n, axis=-1, keepdims=True)), sink)
        p_c = jnp.exp(s_c - m)
        p_n = jnp.exp(s_n - m)
        denom = (jnp.sum(p_c, axis=-1, keepdims=True) + jnp.sum(p_n, axis=-1, keepdims=True)
                 + jnp.exp(sink - m))
        o = (_dot(p_c.astype(BF16), cv_ref[b].astype(BF16)) + _dot(p_n.astype(BF16), v_new)) / denom
        o = jnp.where(head_mask, o, 0.0).astype(BF16)
        o_ref[rows, :] = _dot_t(o, expand).astype(BF16)
        kw_ref[b, 0:WINDOW - DEC_SEQ, :] = ck_ref[b, DEC_SEQ:WINDOW, :]
        kw_ref[b, WINDOW - DEC_SEQ:WINDOW, :] = kvn[b * DEC_SEQ:(b + 1) * DEC_SEQ, :KV_W]
        vw_ref[b, 0:WINDOW - DEC_SEQ, :] = cv_ref[b, DEC_SEQ:WINDOW, :]
        vw_ref[b, WINDOW - DEC_SEQ:WINDOW, :] = kvn[b * DEC_SEQ:(b + 1) * DEC_SEQ, KV_W:]


def _attn_sample(sink_col, qs_rows, ck, cv, kv):
    step_rows = SA_BT * SA_ROWS
    kv_blk0 = N_PROMPT // (SA_BT * DEC_SEQ)
    return pl.pallas_call(
        _attn_sample_kernel,
        grid=(SA_STEPS,),
        in_specs=[
            pl.BlockSpec((SA_ROWS, 1), lambda s: (0, 0)),
            pl.BlockSpec((step_rows, HEAD_DIM), lambda s: (s, 0)),
            pl.BlockSpec((SA_BT, WINDOW, KV_W), lambda s: (s, 0, 0)),
            pl.BlockSpec((SA_BT, WINDOW, KV_W), lambda s: (s, 0, 0)),
            pl.BlockSpec((SA_BT * DEC_SEQ, 2 * KV_W), lambda s: (kv_blk0 + s, 0)),
        ],
        out_specs=[
            pl.BlockSpec((step_rows, HEAD_DIM), lambda s: (s, 0)),
            pl.BlockSpec((SA_BT, WINDOW, KV_W), lambda s: (s, 0, 0)),
            pl.BlockSpec((SA_BT, WINDOW, KV_W), lambda s: (s, 0, 0)),
        ],
        out_shape=[
            jax.ShapeDtypeStruct((DEC_BATCH * SA_ROWS, HEAD_DIM), BF16),
            jax.ShapeDtypeStruct((DEC_BATCH, WINDOW, KV_W), F32),
            jax.ShapeDtypeStruct((DEC_BATCH, WINDOW, KV_W), F32),
        ],
        compiler_params=pltpu.CompilerParams(
            dimension_semantics=("arbitrary",), vmem_limit_bytes=32 * MIB),
        name="attn_sample",
    )(sink_col, qs_rows, ck, cv, kv)


def _gmlp_kernel(u_ref, gv_ref, lng_ref, w_ref, bias_ref, a_ref, *gvn_refs, nchunk, sample):
    ri = lax.broadcasted_iota(jnp.int32, (CHUNK, CHUNK), 0)
    ci = lax.broadcasted_iota(jnp.int32, (CHUNK, CHUNK), 1)
    if sample:
        wmask = ((ri >> 3) == (ci >> 3)) & ((ri & (DEC_SEQ - 1)) >= (ci & (DEC_SEQ - 1)))
    else:
        wmask = ri >= ci
    w = [jnp.where(wmask, w_ref[g], 0.0).astype(BF16) for g in range(GROUPS)]
    lng = lng_ref[...]
    for c in range(nchunk):
        rows = slice(c * CHUNK, (c + 1) * CHUNK)
        gv = gv_ref[rows, :].astype(F32)
        xc = gv - jnp.mean(gv, axis=-1, keepdims=True)
        y = xc * lax.rsqrt(jnp.mean(xc * xc, axis=-1, keepdims=True) + EPS) * lng
        if sample:
            gvn_refs[0][rows, :] = y
        yb = y.astype(BF16)
        for g in range(GROUPS):
            cols = slice(g * GROUP_DIM, (g + 1) * GROUP_DIM)
            mixed = _dot(w[g], yb[:, cols]) + bias_ref[:, g:g + 1]
            a_ref[rows, cols] = (u_ref[rows, cols].astype(F32) * mixed).astype(BF16)


def _gmlp(z, lng, w, bias, *, sample):
    rows = CHUNK if sample else 8 * CHUNK
    nsteps = (N_SAMPLE if sample else N_PROMPT) // rows
    blk0 = (N_PROMPT // rows) if sample else 0
    n_out = N_SAMPLE if sample else N_PROMPT
    out_specs = [pl.BlockSpec((rows, GMLP_W), lambda s: (s, 0))]
    out_shape = [jax.ShapeDtypeStruct((n_out, GMLP_W), BF16)]
    if sample:
        out_specs.append(pl.BlockSpec((rows, GMLP_W), lambda s: (s, 0)))
        out_shape.append(jax.ShapeDtypeStruct((n_out, GMLP_W), F32))
    return pl.pallas_call(
        functools.partial(_gmlp_kernel, nchunk=rows // CHUNK, sample=sample),
        grid=(nsteps,),
        in_specs=[
            pl.BlockSpec((rows, GMLP_W), lambda s: (blk0 + s, Z_U // GMLP_W)),
            pl.BlockSpec((rows, GMLP_W), lambda s: (blk0 + s, Z_GV // GMLP_W)),
            pl.BlockSpec((1, GMLP_W), lambda s: (0, 0)),
            pl.BlockSpec((GROUPS, CHUNK, CHUNK), lambda s: (0, 0, 0)),
            pl.BlockSpec((CHUNK, GROUPS), lambda s: (0, 0)),
        ],
        out_specs=out_specs,
        out_shape=out_shape,
        compiler_params=pltpu.CompilerParams(
            dimension_semantics=("arbitrary",), vmem_limit_bytes=32 * MIB),
        name="gmlp_sample" if sample else "gmlp_prompt",
    )(z, z, lng, w, bias)


MG_TM = 256
MG_NPT = N_PROMPT // MG_TM
MG_NT = N_TOK // MG_TM


def _merge_kernel(xp_ref, xs_ref, ap_ref, as_ref, op_ref, os_ref, ga_ref, gb_ref,
                  wpa_ref, wpb_ref, wo_ref, x1_ref):
    i = pl.program_id(0)

    def body(x_ref, a_ref, o_ref):
        pa = _dot(a_ref[...], wpa_ref[...])
        pb = _dot(o_ref[...], wpb_ref[...])
        mix = ga_ref[...].astype(F32) * pa + gb_ref[...].astype(F32) * pb
        x1_ref[...] = x_ref[...] + _dot(mix.astype(BF16), wo_ref[...])

    @pl.when(i < MG_NPT)
    def _():
        body(xp_ref, ap_ref, op_ref)

    @pl.when(i >= MG_NPT)
    def _():
        body(xs_ref, as_ref, os_ref)


def _merge(xp, xs, a_p, a_s, o_p, o_s, z, wpa, wpb, wo):
    p_idx = lambda i: (jnp.minimum(i, MG_NPT - 1), 0)
    s_idx = lambda i: (jnp.maximum(i - MG_NPT, 0), 0)
    const = lambda i: (0, 0)
    resident = dict(pipeline_mode=pl.Buffered(1))
    return pl.pallas_call(
        _merge_kernel,
        grid=(MG_NT,),
        in_specs=[
            pl.BlockSpec((MG_TM, D_MODEL), p_idx),
            pl.BlockSpec((MG_TM, D_MODEL), s_idx),
            pl.BlockSpec((MG_TM, GMLP_W), p_idx),
            pl.BlockSpec((MG_TM, GMLP_W), s_idx),
            pl.BlockSpec((MG_TM, Q_W), p_idx),
            pl.BlockSpec((MG_TM, Q_W), s_idx),
            pl.BlockSpec((MG_TM, D_MODEL), lambda i: (i, Z_GA // D_MODEL)),
            pl.BlockSpec((MG_TM, D_MODEL), lambda i: (i, Z_GB // D_MODEL)),
            pl.BlockSpec((GMLP_W, D_MODEL), const, **resident),
            pl.BlockSpec((Q_W, D_MODEL), const, **resident),
            pl.BlockSpec((D_MODEL, D_MODEL), const, **resident),
        ],
        out_specs=pl.BlockSpec((MG_TM, D_MODEL), lambda i: (i, 0)),
        out_shape=jax.ShapeDtypeStruct((N_TOK, D_MODEL), F32),
        compiler_params=pltpu.CompilerParams(
            dimension_semantics=("arbitrary",), vmem_limit_bytes=48 * MIB),
        name="merge",
    )(xp, xs, a_p, a_s, o_p, o_s, z, z, wpa, wpb, wo)


FF_TM = 512
FF_TF = 512
FF_NPT = N_PROMPT // FF_TM
FF_NT = N_TOK // FF_TM
FF_NJ = D_FF // FF_TF


def _ffn_kernel(x1_ref, n2_ref, wg_ref, wu_ref, wd_ref, fg_ref, yp_ref, ys_ref, hh_ref, acc_ref):
    i = pl.program_id(0)
    j = pl.program_id(1)

    @pl.when(j == 0)
    def _():
        hh_ref[...] = _rms(x1_ref[...], n2_ref[...]).astype(BF16)
        acc_ref[...] = jnp.zeros_like(acc_ref)

    hh = hh_ref[...]
    act = jax.nn.silu(_dot(hh, wg_ref[...])) * _dot(hh, wu_ref[...])
    acc_ref[...] += _dot(act.astype(BF16), wd_ref[...])

    @pl.when(j == FF_NJ - 1)
    def _():
        y = _rms(x1_ref[...] + acc_ref[...], fg_ref[...])

        @pl.when(i < FF_NPT)
        def _():
            yp_ref[...] = y

        @pl.when(i >= FF_NPT)
        def _():
            ys_ref[...] = y


def _ffn(x1, n2, wg, wu, wd, fg):
    return pl.pallas_call(
        _ffn_kernel,
        grid=(FF_NT, FF_NJ),
        in_specs=[
            pl.BlockSpec((FF_TM, D_MODEL), lambda i, j: (i, 0)),
            pl.BlockSpec((1, D_MODEL), lambda i, j: (0, 0)),
            pl.BlockSpec((D_MODEL, FF_TF), lambda i, j: (0, j)),
            pl.BlockSpec((D_MODEL, FF_TF), lambda i, j: (0, j)),
            pl.BlockSpec((FF_TF, D_MODEL), lambda i, j: (j, 0)),
            pl.BlockSpec((1, D_MODEL), lambda i, j: (0, 0)),
        ],
        out_specs=[
            pl.BlockSpec((FF_TM, D_MODEL), lambda i, j: (jnp.minimum(i, FF_NPT - 1), 0)),
            pl.BlockSpec((FF_TM, D_MODEL), lambda i, j: (jnp.maximum(i - FF_NPT, 0), 0)),
        ],
        out_shape=[
            jax.ShapeDtypeStruct((N_PROMPT, D_MODEL), F32),
            jax.ShapeDtypeStruct((N_SAMPLE, D_MODEL), F32),
        ],
        scratch_shapes=[pltpu.VMEM((FF_TM, D_MODEL), BF16), pltpu.VMEM((FF_TM, D_MODEL), F32)],
        compiler_params=pltpu.CompilerParams(
            dimension_semantics=("arbitrary", "arbitrary"), vmem_limit_bytes=56 * MIB),
        name="ffn",
    )(x1, n2, wg, wu, wd, fg)


def kernel(x_prompt, x_sample, cache_k, cache_v, norm1_g, w_in, gmlp_norm_g, w_s, b_s, sinks,
           w_pa, w_pb, w_o, norm2_g, w_ff_gate, w_ff_up, w_ff_down, final_g):
    xp = x_prompt.reshape(N_PROMPT, D_MODEL)
    xs = x_sample.reshape(N_SAMPLE, D_MODEL)

    z, kv = _inproj(xp, xs, norm1_g.reshape(1, D_MODEL), w_in.reshape(D_MODEL, IN_W).astype(BF16))

    sinks1 = sinks.reshape(N_Q_HEADS)
    o_p = _attn_prompt(sinks1, z, kv)
    qs_rows = (z[N_PROMPT:, Z_Q:Z_Q + Q_W].reshape(DEC_BATCH, DEC_SEQ, N_Q_HEADS, HEAD_DIM)
               .transpose(0, 2, 1, 3).reshape(DEC_BATCH * SA_ROWS, HEAD_DIM))
    sink_col = jnp.repeat(sinks1, DEC_SEQ).reshape(SA_ROWS, 1)
    o_rows, k_win, v_win = _attn_sample(
        sink_col, qs_rows, cache_k.reshape(DEC_BATCH, WINDOW, KV_W),
        cache_v.reshape(DEC_BATCH, WINDOW, KV_W), kv)
    o_s = (o_rows.reshape(DEC_BATCH, N_Q_HEADS, DEC_SEQ, HEAD_DIM)
           .transpose(0, 2, 1, 3).reshape(N_SAMPLE, Q_W))

    lng = gmlp_norm_g.reshape(1, GMLP_W)
    ws = w_s.reshape(GROUPS, CHUNK, CHUNK)
    bias = b_s.reshape(GROUPS, CHUNK).T
    nseq = CHUNK // DEC_SEQ
    (a_p,) = _gmlp(z, lng, ws, bias, sample=False)
    a_s, gvn_s = _gmlp(z, lng, jnp.tile(ws[:, :DEC_SEQ, :DEC_SEQ], (1, nseq, nseq)),
                       jnp.tile(bias[:DEC_SEQ], (nseq, 1)), sample=True)

    x1 = _merge(xp, xs, a_p, a_s, o_p, o_s, z,
                w_pa.reshape(GMLP_W, D_MODEL).astype(BF16),
                w_pb.reshape(Q_W, D_MODEL).astype(BF16),
                w_o.reshape(D_MODEL, D_MODEL).astype(BF16))

    y_p, y_s = _ffn(x1, norm2_g.reshape(1, D_MODEL),
                    w_ff_gate.reshape(D_MODEL, D_FF).astype(BF16),
                    w_ff_up.reshape(D_MODEL, D_FF).astype(BF16),
                    w_ff_down.reshape(D_FF, D_MODEL).astype(BF16),
                    final_g.reshape(1, D_MODEL))

    kv_p = kv[:N_PROMPT].reshape(BATCH, SEQ, 2 * KV_W)[:, SEQ - WINDOW:, :]
    pk = kv_p[..., :KV_W].reshape(1, BATCH, WINDOW, N_KV_HEADS, HEAD_DIM)
    pv = kv_p[..., KV_W:].reshape(1, BATCH, WINDOW, N_KV_HEADS, HEAD_DIM)
    return (y_p.reshape(BATCH, SEQ, D_MODEL),
            y_s.reshape(DEC_BATCH, DEC_SEQ, D_MODEL),
            pk, pv,
            k_win.reshape(1, DEC_BATCH, WINDOW, N_KV_HEADS, HEAD_DIM),
            v_win.reshape(1, DEC_BATCH, WINDOW, N_KV_HEADS, HEAD_DIM),
            gvn_s.reshape(1, DEC_BATCH, DEC_SEQ, GROUPS, GROUP_DIM))
```

```python
import functools

import jax
import jax.numpy as jnp
from jax import lax
from jax.experimental import pallas as pl
from jax.experimental.pallas import tpu as pltpu

F32 = jnp.float32
BF16 = jnp.bfloat16

D_MODEL = 2048
BATCH = 4
SEQ = 2048
DEC_BATCH = 128
DEC_SEQ = 8
HEAD_DIM = 64
N_Q_HEADS = 16
N_KV_HEADS = 4
Q_PER_KV = N_Q_HEADS // N_KV_HEADS
WINDOW = 128
CHUNK = 128
GROUPS = 8
GROUP_DIM = 128
Q_W = N_Q_HEADS * HEAD_DIM
KV_W = N_KV_HEADS * HEAD_DIM
GMLP_W = GROUPS * GROUP_DIM
D_FF = 5632
IN_W = Q_W + 2 * KV_W + 2 * GMLP_W + 2 * D_MODEL
EPS = 1e-6
NEG = -1e30

N_PROMPT = BATCH * SEQ
N_SAMPLE = DEC_BATCH * DEC_SEQ
N_TOK = N_PROMPT + N_SAMPLE

Z_W = 2 * D_MODEL + Q_W + 2 * GMLP_W
Z_GA, Z_GB, Z_Q, Z_U, Z_GV = 0, 2048, 4096, 5120, 6144

MIB = 1024 * 1024


def _rms(x, g):
    return x * lax.rsqrt(jnp.mean(x * x, axis=-1, keepdims=True) + EPS) * g


def _dot(a, b):
    return jnp.dot(a, b, preferred_element_type=F32)


def _dot_t(a, b):
    return lax.dot_general(a, b, (((1,), (1,)), ((), ())), preferred_element_type=F32)


IP_TM = 1024
IP_TN = 512
IP_CH = 128
IP_NPT = N_PROMPT // IP_TM
IP_NT = N_TOK // IP_TM
IP_NJ = IN_W // IP_TN


def _inproj_zcol(j):
    return jnp.where(j < 2, 8 + j, jnp.where(j == 2, 9, jnp.where(j < 7, j + 7, j - 7)))


def _inproj_kernel(xp_ref, xs_ref, g_ref, w_ref, z_ref, kv_ref, h_ref):
    i = pl.program_id(0)
    j = pl.program_id(1)

    @pl.when((j == 0) & (i < IP_NPT))
    def _():
        h_ref[...] = _rms(xp_ref[...], g_ref[...]).astype(BF16)

    @pl.when((j == 0) & (i >= IP_NPT))
    def _():
        h_ref[...] = _rms(xs_ref[...], g_ref[...]).astype(BF16)

    def proj(out_ref, act):
        w = w_ref[...].astype(BF16)
        for c in range(IP_TM // IP_CH):
            rows = slice(c * IP_CH, (c + 1) * IP_CH)
            out_ref[rows, :] = act(_dot(h_ref[rows, :], w)).astype(out_ref.dtype)

    @pl.when(j < 2)
    def _():
        proj(z_ref, lambda v: v)

    @pl.when(j == 2)
    def _():
        proj(kv_ref, lambda v: v)

    @pl.when((j >= 3) & (j < 7))
    def _():
        proj(z_ref, jax.nn.gelu)

    @pl.when(j >= 7)
    def _():
        proj(z_ref, jax.nn.sigmoid)


def _inproj(xp, xs, g, w):
    return pl.pallas_call(
        _inproj_kernel,
        grid=(IP_NT, IP_NJ),
        in_specs=[
            pl.BlockSpec((IP_TM, D_MODEL), lambda i, j: (jnp.minimum(i, IP_NPT - 1), 0)),
            pl.BlockSpec((IP_TM, D_MODEL), lambda i, j: (jnp.maximum(i - IP_NPT, 0), 0),
                         pipeline_mode=pl.Buffered(1)),
            pl.BlockSpec((1, D_MODEL), lambda i, j: (0, 0)),
            pl.BlockSpec((D_MODEL, IP_TN), lambda i, j: (0, j)),
        ],
        out_specs=[
            pl.BlockSpec((IP_TM, IP_TN), lambda i, j: (i, _inproj_zcol(j))),
            pl.BlockSpec((IP_TM, 2 * KV_W), lambda i, j: (i, 0)),
        ],
        out_shape=[
            jax.ShapeDtypeStruct((N_TOK, Z_W), BF16),
            jax.ShapeDtypeStruct((N_TOK, 2 * KV_W), F32),
        ],
        scratch_shapes=[pltpu.VMEM((IP_TM, D_MODEL), BF16)],
        compiler_params=pltpu.CompilerParams(
            dimension_semantics=("arbitrary", "arbitrary"), vmem_limit_bytes=52 * MIB),
        name="inproj",
    )(xp, xs, g, w)


NBLK = SEQ // WINDOW


PA_QB = 4
PA_ROWS = PA_QB * WINDOW
PA_STEPS_PER_SEQ = NBLK // PA_QB
GRP_W = Q_PER_KV * HEAD_DIM


def _lane_group_rotations(x):
    x1 = pltpu.roll(x, HEAD_DIM, 1)
    return [x.astype(BF16), x1.astype(BF16),
            pltpu.roll(x, 2 * HEAD_DIM, 1).astype(BF16), pltpu.roll(x1, 2 * HEAD_DIM, 1).astype(BF16)]


def _attn_prompt_kernel(sink_ref, q_ref, kc_ref, kp_ref, o_ref, pk_ref, pv_ref):
    t = pl.program_id(0)
    kv = jnp.concatenate([kp_ref[...], kc_ref[...]], axis=0)
    k_rot = _lane_group_rotations(kv[:, :KV_W])
    v_rot = _lane_group_rotations(kv[:, KV_W:])
    lane_grp = lax.broadcasted_iota(jnp.int32, (WINDOW, GRP_W), 1) >> 6
    in_grp = [lane_grp == g for g in range(Q_PER_KV)]
    nrow = N_KV_HEADS * WINDOW
    key = lax.broadcasted_iota(jnp.int32, (2 * WINDOW, nrow), 0)
    qry = lax.broadcasted_iota(jnp.int32, (2 * WINDOW, nrow), 1) & (WINDOW - 1)
    band = (key >= qry) & (key <= qry + WINDOW)
    bias_inner = jnp.where(band, 0.0, NEG)
    bias_first = jnp.where(band & (key >= WINDOW), 0.0, NEG)

    for qb in range(PA_QB):
        is_first = ((t * PA_QB + qb) % NBLK) == 0
        bias = jnp.where(is_first, bias_first, bias_inner)
        q = q_ref[qb * WINDOW:(qb + 1) * WINDOW, :] * (HEAD_DIM ** -0.5)
        keys = slice(qb * WINDOW, (qb + 2) * WINDOW)
        outs = [[None] * Q_PER_KV for _ in range(N_KV_HEADS)]
        for rot in range(Q_PER_KV):
            lhs = jnp.concatenate(
                [jnp.where(in_grp[(h + rot) % Q_PER_KV], q[:, h * GRP_W:(h + 1) * GRP_W], 0)
                 for h in range(N_KV_HEADS)], axis=0)
            sink = jnp.concatenate(
                [jnp.full((1, WINDOW), sink_ref[h * Q_PER_KV + (h + rot) % Q_PER_KV], F32)
                 for h in range(N_KV_HEADS)], axis=1)
            s = _dot_t(k_rot[rot][keys], lhs) + bias
            m = jnp.maximum(jnp.max(s, axis=0, keepdims=True), sink)
            p = jnp.exp(s - m)
            denom = jnp.sum(p, axis=0, keepdims=True) + jnp.exp(sink - m)
            probs = (p / denom).astype(BF16)
            o = lax.dot_general(probs, v_rot[rot][keys], (((0,), (0,)), ((), ())),
                                preferred_element_type=F32)
            for h in range(N_KV_HEADS):
                outs[h][(h + rot) % Q_PER_KV] = o[h * WINDOW:(h + 1) * WINDOW]
        for h in range(N_KV_HEADS):
            oh = outs[h][Q_PER_KV - 1]
            for g in range(Q_PER_KV - 2, -1, -1):
                oh = jnp.where(in_grp[g], outs[h][g], oh)
            o_ref[qb * WINDOW:(qb + 1) * WINDOW, h * GRP_W:(h + 1) * GRP_W] = oh.astype(BF16)

    @pl.when(t % PA_STEPS_PER_SEQ == PA_STEPS_PER_SEQ - 1)
    def _():
        pk_ref[0] = kc_ref[PA_ROWS - WINDOW:PA_ROWS, :KV_W]
        pv_ref[0] = kc_ref[PA_ROWS - WINDOW:PA_ROWS, KV_W:]


def _attn_prompt(sinks, z, kv):
    return pl.pallas_call(
        _attn_prompt_kernel,
        grid=(N_PROMPT // PA_ROWS,),
        in_specs=[
            pl.BlockSpec(memory_space=pltpu.SMEM),
            pl.BlockSpec((PA_ROWS, Q_W), lambda t: (t, Z_Q // Q_W)),
            pl.BlockSpec((PA_ROWS, 2 * KV_W), lambda t: (t, 0)),
            pl.BlockSpec((WINDOW, 2 * KV_W), lambda t: (jnp.maximum(t * PA_QB - 1, 0), 0)),
        ],
        out_specs=[
            pl.BlockSpec((PA_ROWS, Q_W), lambda t: (t, 0)),
            pl.BlockSpec((1, WINDOW, KV_W), lambda t: (t // PA_STEPS_PER_SEQ, 0, 0)),
            pl.BlockSpec((1, WINDOW, KV_W), lambda t: (t // PA_STEPS_PER_SEQ, 0, 0)),
        ],
        out_shape=[
            jax.ShapeDtypeStruct((N_PROMPT, Q_W), BF16),
            jax.ShapeDtypeStruct((BATCH, WINDOW, KV_W), F32),
            jax.ShapeDtypeStruct((BATCH, WINDOW, KV_W), F32),
        ],
        compiler_params=pltpu.CompilerParams(
            dimension_semantics=("arbitrary",), vmem_limit_bytes=32 * MIB),
        name="attn_prompt",
    )(sinks, z, kv, kv)


SA_BT = 16
SA_ROWS = N_Q_HEADS * DEC_SEQ
SA_STEPS = DEC_BATCH // SA_BT


def _attn_sample_kernel(sink_ref, qs_ref, ck_ref, cv_ref, kvn_ref, o_ref, kw_ref, vw_ref):
    e_row = lax.broadcasted_iota(jnp.int32, (HEAD_DIM, KV_W), 0)
    e_col = lax.broadcasted_iota(jnp.int32, (HEAD_DIM, KV_W), 1)
    expand = ((e_col & (HEAD_DIM - 1)) == e_row).astype(BF16)
    r_i = lax.broadcasted_iota(jnp.int32, (SA_ROWS, KV_W), 0)
    c_i = lax.broadcasted_iota(jnp.int32, (SA_ROWS, KV_W), 1)
    head_mask = (r_i >> 5) == (c_i >> 6)
    rr = lax.broadcasted_iota(jnp.int32, (SA_ROWS, WINDOW), 0)
    cc = lax.broadcasted_iota(jnp.int32, (SA_ROWS, WINDOW), 1)
    tq = rr & (DEC_SEQ - 1)
    mask_c = cc >= tq
    sink = sink_ref[...]
    kvn = kvn_ref[...]
    k_new = kvn[:, :KV_W].astype(BF16)
    v_new = kvn[:, KV_W:].astype(BF16)
    scale = HEAD_DIM ** -0.5

    for b in range(SA_BT):
        rows = slice(b * SA_ROWS, (b + 1) * SA_ROWS)
        q_rep = _dot(qs_ref[rows, :], expand)
        q_exp = jnp.where(head_mask, q_rep, 0.0).astype(BF16)
        s_c = _dot_t(q_exp, ck_ref[b].astype(BF16)) * scale
        s_n = _dot_t(q_exp, k_new) * scale
        mask_n = ((cc >> 3) == b) & ((cc & (DEC_SEQ - 1)) <= tq)
        s_c = jnp.where(mask_c, s_c, NEG)
        s_n = jnp.where(mask_n, s_n, NEG)
        m = jnp.maximum(jnp.maximum(jnp.max(s_c, axis=-1, keepdims=True),
                                    jnp.max(s_n, axis=-1, keepdims=True)), sink)
        p_c = jnp.exp(s_c - m)
        p_n = jnp.exp(s_n - m)
        denom = (jnp.sum(p_c, axis=-1, keepdims=True) + jnp.sum(p_n, axis=-1, keepdims=True)
                 + jnp.exp(sink - m))
        o = (_dot(p_c.astype(BF16), cv_ref[b].astype(BF16)) + _dot(p_n.astype(BF16), v_new)) / denom
        o = jnp.where(head_mask, o, 0.0).astype(BF16)
        o_ref[rows, :] = _dot_t(o, expand).astype(BF16)
        kw_ref[b, 0:WINDOW - DEC_SEQ, :] = ck_ref[b, DEC_SEQ:WINDOW, :]
        kw_ref[b, WINDOW - DEC_SEQ:WINDOW, :] = kvn[b * DEC_SEQ:(b + 1) * DEC_SEQ, :KV_W]
        vw_ref[b, 0:WINDOW - DEC_SEQ, :] = cv_ref[b, DEC_SEQ:WINDOW, :]
        vw_ref[b, WINDOW - DEC_SEQ:WINDOW, :] = kvn[b * DEC_SEQ:(b + 1) * DEC_SEQ, KV_W:]


def _attn_sample(sink_col, qs_rows, ck, cv, kv):
    step_rows = SA_BT * SA_ROWS
    kv_blk0 = N_PROMPT // (SA_BT * DEC_SEQ)
    return pl.pallas_call(
        _attn_sample_kernel,
        grid=(SA_STEPS,),
        in_specs=[
            pl.BlockSpec((SA_ROWS, 1), lambda s: (0, 0)),
            pl.BlockSpec((step_rows, HEAD_DIM), lambda s: (s, 0)),
            pl.BlockSpec((SA_BT, WINDOW, KV_W), lambda s: (s, 0, 0)),
            pl.BlockSpec((SA_BT, WINDOW, KV_W), lambda s: (s, 0, 0)),
            pl.BlockSpec((SA_BT * DEC_SEQ, 2 * KV_W), lambda s: (kv_blk0 + s, 0)),
        ],
        out_specs=[
            pl.BlockSpec((step_rows, HEAD_DIM), lambda s: (s, 0)),
            pl.BlockSpec((SA_BT, WINDOW, KV_W), lambda s: (s, 0, 0)),
            pl.BlockSpec((SA_BT, WINDOW, KV_W), lambda s: (s, 0, 0)),
        ],
        out_shape=[
            jax.ShapeDtypeStruct((DEC_BATCH * SA_ROWS, HEAD_DIM), BF16),
            jax.ShapeDtypeStruct((DEC_BATCH, WINDOW, KV_W), F32),
            jax.ShapeDtypeStruct((DEC_BATCH, WINDOW, KV_W), F32),
        ],
        compiler_params=pltpu.CompilerParams(
            dimension_semantics=("arbitrary",), vmem_limit_bytes=32 * MIB),
        name="attn_sample",
    )(sink_col, qs_rows, ck, cv, kv)


def _gmlp_kernel(u_ref, gv_ref, lng_ref, w_ref, bias_ref, a_ref, *gvn_refs, nchunk, sample):
    ri = lax.broadcasted_iota(jnp.int32, (CHUNK, CHUNK), 0)
    ci = lax.broadcasted_iota(jnp.int32, (CHUNK, CHUNK), 1)
    nseq = CHUNK // DEC_SEQ
    if sample:
        wmask = ((ri >> 3) == (ci >> 3)) & ((ri & (DEC_SEQ - 1)) >= (ci & (DEC_SEQ - 1)))
        corner_lanes = lax.broadcasted_iota(jnp.int32, (DEC_SEQ, CHUNK), 1) < DEC_SEQ

        def group_w(g):
            w8 = jnp.where(corner_lanes, w_ref[g, 0:DEC_SEQ, :], 0.0)
            shift = DEC_SEQ
            while shift < CHUNK:
                w8 = w8 + pltpu.roll(w8, shift, 1)
                shift *= 2
            return jnp.concatenate([w8] * nseq, axis=0)

        bias = jnp.concatenate([bias_ref[0:DEC_SEQ, :]] * nseq, axis=0)
    else:
        wmask = ri >= ci
        group_w = lambda g: w_ref[g]
        bias = bias_ref[...]
    w = [jnp.where(wmask, group_w(g), 0.0).astype(BF16) for g in range(GROUPS)]
    lng = lng_ref[...]
    for c in range(nchunk):
        rows = slice(c * CHUNK, (c + 1) * CHUNK)
        gv = gv_ref[rows, :].astype(F32)
        xc = gv - jnp.mean(gv, axis=-1, keepdims=True)
        y = xc * lax.rsqrt(jnp.mean(xc * xc, axis=-1, keepdims=True) + EPS) * lng
        if sample:
            gvn_refs[0][rows, :] = y
        yb = y.astype(BF16)
        for g in range(GROUPS):
            cols = slice(g * GROUP_DIM, (g + 1) * GROUP_DIM)
            mixed = _dot(w[g], yb[:, cols]) + bias[:, g:g + 1]
            a_ref[rows, cols] = (u_ref[rows, cols].astype(F32) * mixed).astype(BF16)


def _gmlp(z, lng, w, bias, *, sample):
    rows = CHUNK if sample else 8 * CHUNK
    nsteps = (N_SAMPLE if sample else N_PROMPT) // rows
    blk0 = (N_PROMPT // rows) if sample else 0
    n_out = N_SAMPLE if sample else N_PROMPT
    out_specs = [pl.BlockSpec((rows, GMLP_W), lambda s: (s, 0))]
    out_shape = [jax.ShapeDtypeStruct((n_out, GMLP_W), BF16)]
    if sample:
        out_specs.append(pl.BlockSpec((rows, GMLP_W), lambda s: (s, 0)))
        out_shape.append(jax.ShapeDtypeStruct((n_out, GMLP_W), F32))
    return pl.pallas_call(
        functools.partial(_gmlp_kernel, nchunk=rows // CHUNK, sample=sample),
        grid=(nsteps,),
        in_specs=[
            pl.BlockSpec((rows, GMLP_W), lambda s: (blk0 + s, Z_U // GMLP_W)),
            pl.BlockSpec((rows, GMLP_W), lambda s: (blk0 + s, Z_GV // GMLP_W)),
            pl.BlockSpec((1, GMLP_W), lambda s: (0, 0)),
            pl.BlockSpec((GROUPS, CHUNK, CHUNK), lambda s: (0, 0, 0)),
            pl.BlockSpec((CHUNK, GROUPS), lambda s: (0, 0)),
        ],
        out_specs=out_specs,
        out_shape=out_shape,
        compiler_params=pltpu.CompilerParams(
            dimension_semantics=("arbitrary",), vmem_limit_bytes=32 * MIB),
        name="gmlp_sample" if sample else "gmlp_prompt",
    )(z, z, lng, w, bias)


MG_TM = 256
MG_NPT = N_PROMPT // MG_TM
MG_NT = N_TOK // MG_TM


def _merge_kernel(xp_ref, xs_ref, ap_ref, as_ref, op_ref, os_ref, ga_ref, gb_ref,
                  wpa_ref, wpb_ref, wo_ref, x1_ref):
    i = pl.program_id(0)

    def body(x_ref, a_ref, o_ref):
        pa = _dot(a_ref[...], wpa_ref[...])
        pb = _dot(o_ref[...], wpb_ref[...])
        mix = ga_ref[...].astype(F32) * pa + gb_ref[...].astype(F32) * pb
        x1_ref[...] = x_ref[...] + _dot(mix.astype(BF16), wo_ref[...])

    @pl.when(i < MG_NPT)
    def _():
        body(xp_ref, ap_ref, op_ref)

    @pl.when(i >= MG_NPT)
    def _():
        body(xs_ref, as_ref, os_ref)


def _merge(xp, xs, a_p, a_s, o_p, o_s, z, wpa, wpb, wo):
    p_idx = lambda i: (jnp.minimum(i, MG_NPT - 1), 0)
    s_idx = lambda i: (jnp.maximum(i - MG_NPT, 0), 0)
    const = lambda i: (0, 0)
    resident = dict(pipeline_mode=pl.Buffered(1))
    return pl.pallas_call(
        _merge_kernel,
        grid=(MG_NT,),
        in_specs=[
            pl.BlockSpec((MG_TM, D_MODEL), p_idx),
            pl.BlockSpec((MG_TM, D_MODEL), s_idx),
            pl.BlockSpec((MG_TM, GMLP_W), p_idx),
            pl.BlockSpec((MG_TM, GMLP_W), s_idx),
            pl.BlockSpec((MG_TM, Q_W), p_idx),
            pl.BlockSpec((MG_TM, Q_W), s_idx),
            pl.BlockSpec((MG_TM, D_MODEL), lambda i: (i, Z_GA // D_MODEL)),
            pl.BlockSpec((MG_TM, D_MODEL), lambda i: (i, Z_GB // D_MODEL)),
            pl.BlockSpec((GMLP_W, D_MODEL), const, **resident),
            pl.BlockSpec((Q_W, D_MODEL), const, **resident),
            pl.BlockSpec((D_MODEL, D_MODEL), const, **resident),
        ],
        out_specs=pl.BlockSpec((MG_TM, D_MODEL), lambda i: (i, 0)),
        out_shape=jax.ShapeDtypeStruct((N_TOK, D_MODEL), F32),
        compiler_params=pltpu.CompilerParams(
            dimension_semantics=("arbitrary",), vmem_limit_bytes=48 * MIB),
        name="merge",
    )(xp, xs, a_p, a_s, o_p, o_s, z, z, wpa, wpb, wo)


FF_TM = 512
FF_TF = 512
FF_NPT = N_PROMPT // FF_TM
FF_NT = N_TOK // FF_TM
FF_NJ = D_FF // FF_TF


def _ffn_kernel(x1_ref, n2_ref, wg_ref, wu_ref, wd_ref, fg_ref, yp_ref, ys_ref, hh_ref, acc_ref):
    i = pl.program_id(0)
    j = pl.program_id(1)

    @pl.when(j == 0)
    def _():
        hh_ref[...] = _rms(x1_ref[...], n2_ref[...]).astype(BF16)
        acc_ref[...] = jnp.zeros_like(acc_ref)

    hh = hh_ref[...]
    act = jax.nn.silu(_dot(hh, wg_ref[...])) * _dot(hh, wu_ref[...])
    acc_ref[...] += _dot(act.astype(BF16), wd_ref[...])

    @pl.when(j == FF_NJ - 1)
    def _():
        y = _rms(x1_ref[...] + acc_ref[...], fg_ref[...])

        @pl.when(i < FF_NPT)
        def _():
            yp_ref[...] = y

        @pl.when(i >= FF_NPT)
        def _():
            ys_ref[...] = y


def _ffn(x1, n2, wg, wu, wd, fg):
    return pl.pallas_call(
        _ffn_kernel,
        grid=(FF_NT, FF_NJ),
        in_specs=[
            pl.BlockSpec((FF_TM, D_MODEL), lambda i, j: (i, 0)),
            pl.BlockSpec((1, D_MODEL), lambda i, j: (0, 0)),
            pl.BlockSpec((D_MODEL, FF_TF), lambda i, j: (0, j)),
            pl.BlockSpec((D_MODEL, FF_TF), lambda i, j: (0, j)),
            pl.BlockSpec((FF_TF, D_MODEL), lambda i, j: (j, 0)),
            pl.BlockSpec((1, D_MODEL), lambda i, j: (0, 0)),
        ],
        out_specs=[
            pl.BlockSpec((FF_TM, D_MODEL), lambda i, j: (jnp.minimum(i, FF_NPT - 1), 0)),
            pl.BlockSpec((FF_TM, D_MODEL), lambda i, j: (jnp.maximum(i - FF_NPT, 0), 0)),
        ],
        out_shape=[
            jax.ShapeDtypeStruct((N_PROMPT, D_MODEL), F32),
            jax.ShapeDtypeStruct((N_SAMPLE, D_MODEL), F32),
        ],
        scratch_shapes=[pltpu.VMEM((FF_TM, D_MODEL), BF16), pltpu.VMEM((FF_TM, D_MODEL), F32)],
        compiler_params=pltpu.CompilerParams(
            dimension_semantics=("arbitrary", "arbitrary"), vmem_limit_bytes=56 * MIB),
        name="ffn",
    )(x1, n2, wg, wu, wd, fg)


def kernel(x_prompt, x_sample, cache_k, cache_v, norm1_g, w_in, gmlp_norm_g, w_s, b_s, sinks,
           w_pa, w_pb, w_o, norm2_g, w_ff_gate, w_ff_up, w_ff_down, final_g):
    xp = x_prompt.reshape(N_PROMPT, D_MODEL)
    xs = x_sample.reshape(N_SAMPLE, D_MODEL)

    z, kv = _inproj(xp, xs, norm1_g.reshape(1, D_MODEL), w_in.reshape(D_MODEL, IN_W))

    sinks1 = sinks.reshape(N_Q_HEADS)
    o_p, pk, pv = _attn_prompt(sinks1, z, kv)
    qs_rows = (z[N_PROMPT:, Z_Q:Z_Q + Q_W].reshape(DEC_BATCH, DEC_SEQ, N_Q_HEADS, HEAD_DIM)
               .transpose(0, 2, 1, 3).reshape(DEC_BATCH * SA_ROWS, HEAD_DIM))
    sink_col = jnp.repeat(sinks1, DEC_SEQ).reshape(SA_ROWS, 1)
    o_rows, k_win, v_win = _attn_sample(
        sink_col, qs_rows, cache_k.reshape(DEC_BATCH, WINDOW, KV_W),
        cache_v.reshape(DEC_BATCH, WINDOW, KV_W), kv)
    o_s = (o_rows.reshape(DEC_BATCH, N_Q_HEADS, DEC_SEQ, HEAD_DIM)
           .transpose(0, 2, 1, 3).reshape(N_SAMPLE, Q_W))

    lng = gmlp_norm_g.reshape(1, GMLP_W)
    ws = w_s.reshape(GROUPS, CHUNK, CHUNK)
    bias = b_s.reshape(GROUPS, CHUNK).T
    (a_p,) = _gmlp(z, lng, ws, bias, sample=False)
    a_s, gvn_s = _gmlp(z, lng, ws, bias, sample=True)

    x1 = _merge(xp, xs, a_p, a_s, o_p, o_s, z,
                w_pa.reshape(GMLP_W, D_MODEL).astype(BF16),
                w_pb.reshape(Q_W, D_MODEL).astype(BF16),
                w_o.reshape(D_MODEL, D_MODEL).astype(BF16))

    y_p, y_s = _ffn(x1, norm2_g.reshape(1, D_MODEL),
                    w_ff_gate.reshape(D_MODEL, D_FF).astype(BF16),
                    w_ff_up.reshape(D_MODEL, D_FF).astype(BF16),
                    w_ff_down.reshape(D_FF, D_MODEL).astype(BF16),
                    final_g.reshape(1, D_MODEL))

    return (y_p.reshape(BATCH, SEQ, D_MODEL),
            y_s.reshape(DEC_BATCH, DEC_SEQ, D_MODEL),
            pk.reshape(1, BATCH, WINDOW, N_KV_HEADS, HEAD_DIM),
            pv.reshape(1, BATCH, WINDOW, N_KV_HEADS, HEAD_DIM),
            k_win.reshape(1, DEC_BATCH, WINDOW, N_KV_HEADS, HEAD_DIM),
            v_win.reshape(1, DEC_BATCH, WINDOW, N_KV_HEADS, HEAD_DIM),
            gvn_s.reshape(1, DEC_BATCH, DEC_SEQ, GROUPS, GROUP_DIM))
```

```python
import functools

import jax
import jax.numpy as jnp
from jax import lax
from jax.experimental import pallas as pl
from jax.experimental.pallas import tpu as pltpu

F32 = jnp.float32
BF16 = jnp.bfloat16

D_MODEL = 2048
BATCH = 4
SEQ = 2048
DEC_BATCH = 128
DEC_SEQ = 8
HEAD_DIM = 64
N_Q_HEADS = 16
N_KV_HEADS = 4
Q_PER_KV = N_Q_HEADS // N_KV_HEADS
WINDOW = 128
CHUNK = 128
GROUPS = 8
GROUP_DIM = 128
Q_W = N_Q_HEADS * HEAD_DIM
KV_W = N_KV_HEADS * HEAD_DIM
GMLP_W = GROUPS * GROUP_DIM
D_FF = 5632
IN_W = Q_W + 2 * KV_W + 2 * GMLP_W + 2 * D_MODEL
EPS = 1e-6
NEG = -1e30

N_PROMPT = BATCH * SEQ
N_SAMPLE = DEC_BATCH * DEC_SEQ
N_TOK = N_PROMPT + N_SAMPLE

Z_W = 2 * D_MODEL + Q_W + 2 * GMLP_W
Z_GA, Z_GB, Z_Q, Z_U, Z_GV = 0, 2048, 4096, 5120, 6144

MIB = 1024 * 1024


def _rms(x, g):
    return x * lax.rsqrt(jnp.mean(x * x, axis=-1, keepdims=True) + EPS) * g


def _dot(a, b):
    return jnp.dot(a, b, preferred_element_type=F32)


def _col_tiles(w, tn):
    k, n = w.shape
    return w.reshape(k, n // tn, tn).transpose(1, 0, 2).astype(BF16)


def _dot_t(a, b):
    return lax.dot_general(a, b, (((1,), (1,)), ((), ())), preferred_element_type=F32)


IP_TM = 1024
IP_TN = 512
IP_CH = 128
IP_NPT = N_PROMPT // IP_TM
IP_NT = N_TOK // IP_TM
IP_NJ = IN_W // IP_TN


def _inproj_zcol(j):
    return jnp.where(j < 2, 8 + j, jnp.where(j == 2, 9, jnp.where(j < 7, j + 7, j - 7)))


def _inproj_kernel(xp_ref, xs_ref, g_ref, w_ref, z_ref, kv_ref, h_ref):
    i = pl.program_id(0)
    j = pl.program_id(1)

    @pl.when((j == 0) & (i < IP_NPT))
    def _():
        h_ref[...] = _rms(xp_ref[...], g_ref[...]).astype(BF16)

    @pl.when((j == 0) & (i >= IP_NPT))
    def _():
        h_ref[...] = _rms(xs_ref[...], g_ref[...]).astype(BF16)

    def proj(out_ref, act):
        w = w_ref[...]
        for c in range(IP_TM // IP_CH):
            rows = slice(c * IP_CH, (c + 1) * IP_CH)
            out_ref[rows, :] = act(_dot(h_ref[rows, :], w)).astype(out_ref.dtype)

    @pl.when(j < 2)
    def _():
        proj(z_ref, lambda v: v)

    @pl.when(j == 2)
    def _():
        proj(kv_ref, lambda v: v)

    @pl.when((j >= 3) & (j < 7))
    def _():
        proj(z_ref, jax.nn.gelu)

    @pl.when(j >= 7)
    def _():
        proj(z_ref, jax.nn.sigmoid)


def _inproj(xp, xs, g, w):
    return pl.pallas_call(
        _inproj_kernel,
        grid=(IP_NT, IP_NJ),
        in_specs=[
            pl.BlockSpec((IP_TM, D_MODEL), lambda i, j: (jnp.minimum(i, IP_NPT - 1), 0)),
            pl.BlockSpec((IP_TM, D_MODEL), lambda i, j: (jnp.maximum(i - IP_NPT, 0), 0),
                         pipeline_mode=pl.Buffered(1)),
            pl.BlockSpec((1, D_MODEL), lambda i, j: (0, 0)),
            pl.BlockSpec((None, D_MODEL, IP_TN), lambda i, j: (j, 0, 0)),
        ],
        out_specs=[
            pl.BlockSpec((IP_TM, IP_TN), lambda i, j: (i, _inproj_zcol(j))),
            pl.BlockSpec((IP_TM, 2 * KV_W), lambda i, j: (i, 0)),
        ],
        out_shape=[
            jax.ShapeDtypeStruct((N_TOK, Z_W), BF16),
            jax.ShapeDtypeStruct((N_TOK, 2 * KV_W), F32),
        ],
        scratch_shapes=[pltpu.VMEM((IP_TM, D_MODEL), BF16)],
        compiler_params=pltpu.CompilerParams(
            dimension_semantics=("arbitrary", "arbitrary"), vmem_limit_bytes=52 * MIB),
        name="inproj",
    )(xp, xs, g, w)


NBLK = SEQ // WINDOW


PA_QB = 4
PA_ROWS = PA_QB * WINDOW
PA_STEPS_PER_SEQ = NBLK // PA_QB
GRP_W = Q_PER_KV * HEAD_DIM


def _lane_group_rotations(x):
    x1 = pltpu.roll(x, HEAD_DIM, 1)
    return [x.astype(BF16), x1.astype(BF16),
            pltpu.roll(x, 2 * HEAD_DIM, 1).astype(BF16), pltpu.roll(x1, 2 * HEAD_DIM, 1).astype(BF16)]


def _attn_prompt_kernel(sink_ref, q_ref, kc_ref, kp_ref, o_ref, pk_ref, pv_ref):
    t = pl.program_id(0)
    kv = jnp.concatenate([kp_ref[...], kc_ref[...]], axis=0)
    k_rot = _lane_group_rotations(kv[:, :KV_W])
    v_rot = _lane_group_rotations(kv[:, KV_W:])
    lane_grp = lax.broadcasted_iota(jnp.int32, (WINDOW, GRP_W), 1) >> 6
    in_grp = [lane_grp == g for g in range(Q_PER_KV)]
    nrow = N_KV_HEADS * WINDOW
    key = lax.broadcasted_iota(jnp.int32, (2 * WINDOW, nrow), 0)
    qry = lax.broadcasted_iota(jnp.int32, (2 * WINDOW, nrow), 1) & (WINDOW - 1)
    band = (key >= qry) & (key <= qry + WINDOW)
    bias_inner = jnp.where(band, 0.0, NEG)
    bias_first = jnp.where(band & (key >= WINDOW), 0.0, NEG)

    for qb in range(PA_QB):
        is_first = ((t * PA_QB + qb) % NBLK) == 0
        bias = jnp.where(is_first, bias_first, bias_inner)
        q = q_ref[qb * WINDOW:(qb + 1) * WINDOW, :] * (HEAD_DIM ** -0.5)
        keys = slice(qb * WINDOW, (qb + 2) * WINDOW)
        outs = [[None] * Q_PER_KV for _ in range(N_KV_HEADS)]
        for rot in range(Q_PER_KV):
            lhs = jnp.concatenate(
                [jnp.where(in_grp[(h + rot) % Q_PER_KV], q[:, h * GRP_W:(h + 1) * GRP_W], 0)
                 for h in range(N_KV_HEADS)], axis=0)
            sink = jnp.concatenate(
                [jnp.full((1, WINDOW), sink_ref[h * Q_PER_KV + (h + rot) % Q_PER_KV], F32)
                 for h in range(N_KV_HEADS)], axis=1)
            s = _dot_t(k_rot[rot][keys], lhs) + bias
            m = jnp.maximum(jnp.max(s, axis=0, keepdims=True), sink)
            p = jnp.exp(s - m)
            denom = jnp.sum(p, axis=0, keepdims=True) + jnp.exp(sink - m)
            probs = (p / denom).astype(BF16)
            o = lax.dot_general(probs, v_rot[rot][keys], (((0,), (0,)), ((), ())),
                                preferred_element_type=F32)
            for h in range(N_KV_HEADS):
                outs[h][(h + rot) % Q_PER_KV] = o[h * WINDOW:(h + 1) * WINDOW]
        for h in range(N_KV_HEADS):
            oh = outs[h][Q_PER_KV - 1]
            for g in range(Q_PER_KV - 2, -1, -1):
                oh = jnp.where(in_grp[g], outs[h][g], oh)
            o_ref[qb * WINDOW:(qb + 1) * WINDOW, h * GRP_W:(h + 1) * GRP_W] = oh.astype(BF16)

    @pl.when(t % PA_STEPS_PER_SEQ == PA_STEPS_PER_SEQ - 1)
    def _():
        pk_ref[0] = kc_ref[PA_ROWS - WINDOW:PA_ROWS, :KV_W]
        pv_ref[0] = kc_ref[PA_ROWS - WINDOW:PA_ROWS, KV_W:]


def _attn_prompt(sinks, z, kv):
    return pl.pallas_call(
        _attn_prompt_kernel,
        grid=(N_PROMPT // PA_ROWS,),
        in_specs=[
            pl.BlockSpec(memory_space=pltpu.SMEM),
            pl.BlockSpec((PA_ROWS, Q_W), lambda t: (t, Z_Q // Q_W)),
            pl.BlockSpec((PA_ROWS, 2 * KV_W), lambda t: (t, 0)),
            pl.BlockSpec((WINDOW, 2 * KV_W), lambda t: (jnp.maximum(t * PA_QB - 1, 0), 0)),
        ],
        out_specs=[
            pl.BlockSpec((PA_ROWS, Q_W), lambda t: (t, 0)),
            pl.BlockSpec((1, WINDOW, KV_W), lambda t: (t // PA_STEPS_PER_SEQ, 0, 0)),
            pl.BlockSpec((1, WINDOW, KV_W), lambda t: (t // PA_STEPS_PER_SEQ, 0, 0)),
        ],
        out_shape=[
            jax.ShapeDtypeStruct((N_PROMPT, Q_W), BF16),
            jax.ShapeDtypeStruct((BATCH, WINDOW, KV_W), F32),
            jax.ShapeDtypeStruct((BATCH, WINDOW, KV_W), F32),
        ],
        compiler_params=pltpu.CompilerParams(
            dimension_semantics=("arbitrary",), vmem_limit_bytes=32 * MIB),
        name="attn_prompt",
    )(sinks, z, kv, kv)


SA_BT = 16
SA_ROWS = N_Q_HEADS * DEC_SEQ
SA_STEPS = DEC_BATCH // SA_BT


def _attn_sample_kernel(sink_ref, q_ref, ck_ref, cv_ref, kvn_ref, o_ref, kw_ref, vw_ref):
    lane_grp = lax.broadcasted_iota(jnp.int32, (SA_BT * DEC_SEQ, GRP_W), 1) >> 6
    in_grp = [lane_grp == g for g in range(Q_PER_KV)]
    qf = q_ref[...].astype(F32) * (HEAD_DIM ** -0.5)
    q_exp = []
    for h in range(N_KV_HEADS):
        q_h = qf[:, h * GRP_W:(h + 1) * GRP_W]
        for r in range(Q_PER_KV):
            shift = ((h - r) % Q_PER_KV) * HEAD_DIM
            moved = q_h if shift == 0 else pltpu.roll(q_h, shift, 1)
            q_exp.append(jnp.where(in_grp[h], moved, 0.0))
    lhs = jnp.concatenate([q_exp[hr][b * DEC_SEQ:(b + 1) * DEC_SEQ]
                           for b in range(SA_BT) for hr in range(N_Q_HEADS)], axis=0).astype(BF16)

    kvn = kvn_ref[...]
    k_new = kvn[:, :KV_W].astype(BF16)
    v_new = kvn[:, KV_W:].astype(BF16)
    seq_rows = lambda b: slice(b * SA_ROWS, (b + 1) * SA_ROWS)
    s_c = jnp.concatenate([_dot_t(lhs[seq_rows(b)], ck_ref[b].astype(BF16)) for b in range(SA_BT)],
                          axis=0)
    s_n = _dot_t(lhs, k_new)

    nrow = SA_BT * SA_ROWS
    row = lax.broadcasted_iota(jnp.int32, (nrow, WINDOW), 0)
    col = lax.broadcasted_iota(jnp.int32, (nrow, WINDOW), 1)
    tq = row & (DEC_SEQ - 1)
    s_c = jnp.where(col >= tq, s_c, NEG)
    s_n = jnp.where(((col >> 3) == (row >> 7)) & ((col & (DEC_SEQ - 1)) <= tq), s_n, NEG)
    sink_seq = jnp.concatenate([jnp.full((DEC_SEQ, 1), sink_ref[hr], F32) for hr in range(N_Q_HEADS)],
                               axis=0)
    sink = jnp.concatenate([sink_seq] * SA_BT, axis=0)
    m = jnp.maximum(jnp.maximum(jnp.max(s_c, axis=-1, keepdims=True),
                                jnp.max(s_n, axis=-1, keepdims=True)), sink)
    p_c = jnp.exp(s_c - m)
    p_n = jnp.exp(s_n - m)
    denom = (jnp.sum(p_c, axis=-1, keepdims=True) + jnp.sum(p_n, axis=-1, keepdims=True)
             + jnp.exp(sink - m))
    p_c = (p_c / denom).astype(BF16)
    p_n = (p_n / denom).astype(BF16)
    o = jnp.concatenate([_dot(p_c[seq_rows(b)], cv_ref[b].astype(BF16)) for b in range(SA_BT)],
                        axis=0) + _dot(p_n, v_new)

    for h in range(N_KV_HEADS):
        parts = []
        for r in range(Q_PER_KV):
            hr = h * Q_PER_KV + r
            o_hr = jnp.concatenate(
                [o[b * SA_ROWS + hr * DEC_SEQ:b * SA_ROWS + (hr + 1) * DEC_SEQ] for b in range(SA_BT)],
                axis=0)
            shift = ((r - h) % Q_PER_KV) * HEAD_DIM
            parts.append(o_hr if shift == 0 else pltpu.roll(o_hr, shift, 1))
        o_h = parts[Q_PER_KV - 1]
        for g in range(Q_PER_KV - 2, -1, -1):
            o_h = jnp.where(in_grp[g], parts[g], o_h)
        o_ref[:, h * GRP_W:(h + 1) * GRP_W] = o_h.astype(BF16)

    for b in range(SA_BT):
        kw_ref[b, 0:WINDOW - DEC_SEQ, :] = ck_ref[b, DEC_SEQ:WINDOW, :]
        kw_ref[b, WINDOW - DEC_SEQ:WINDOW, :] = kvn[b * DEC_SEQ:(b + 1) * DEC_SEQ, :KV_W]
        vw_ref[b, 0:WINDOW - DEC_SEQ, :] = cv_ref[b, DEC_SEQ:WINDOW, :]
        vw_ref[b, WINDOW - DEC_SEQ:WINDOW, :] = kvn[b * DEC_SEQ:(b + 1) * DEC_SEQ, KV_W:]


def _attn_sample(sinks, z, ck, cv, kv):
    step_tok = SA_BT * DEC_SEQ
    blk0 = N_PROMPT // step_tok
    return pl.pallas_call(
        _attn_sample_kernel,
        grid=(SA_STEPS,),
        in_specs=[
            pl.BlockSpec(memory_space=pltpu.SMEM),
            pl.BlockSpec((step_tok, Q_W), lambda s: (blk0 + s, Z_Q // Q_W)),
            pl.BlockSpec((SA_BT, WINDOW, KV_W), lambda s: (s, 0, 0)),
            pl.BlockSpec((SA_BT, WINDOW, KV_W), lambda s: (s, 0, 0)),
            pl.BlockSpec((step_tok, 2 * KV_W), lambda s: (blk0 + s, 0)),
        ],
        out_specs=[
            pl.BlockSpec((step_tok, Q_W), lambda s: (s, 0)),
            pl.BlockSpec((SA_BT, WINDOW, KV_W), lambda s: (s, 0, 0)),
            pl.BlockSpec((SA_BT, WINDOW, KV_W), lambda s: (s, 0, 0)),
        ],
        out_shape=[
            jax.ShapeDtypeStruct((N_SAMPLE, Q_W), BF16),
            jax.ShapeDtypeStruct((DEC_BATCH, WINDOW, KV_W), F32),
            jax.ShapeDtypeStruct((DEC_BATCH, WINDOW, KV_W), F32),
        ],
        compiler_params=pltpu.CompilerParams(
            dimension_semantics=("arbitrary",), vmem_limit_bytes=32 * MIB),
        name="attn_sample",
    )(sinks, z, ck, cv, kv)


def _gmlp_kernel(u_ref, gv_ref, lng_ref, w_ref, bias_ref, a_ref, *gvn_refs, nchunk, sample):
    ri = lax.broadcasted_iota(jnp.int32, (CHUNK, CHUNK), 0)
    ci = lax.broadcasted_iota(jnp.int32, (CHUNK, CHUNK), 1)
    nseq = CHUNK // DEC_SEQ
    if sample:
        wmask = ((ri >> 3) == (ci >> 3)) & ((ri & (DEC_SEQ - 1)) >= (ci & (DEC_SEQ - 1)))
        corner_lanes = lax.broadcasted_iota(jnp.int32, (DEC_SEQ, CHUNK), 1) < DEC_SEQ

        def group_w(g):
            w8 = jnp.where(corner_lanes, w_ref[g, 0:DEC_SEQ, :], 0.0)
            shift = DEC_SEQ
            while shift < CHUNK:
                w8 = w8 + pltpu.roll(w8, shift, 1)
                shift *= 2
            return jnp.concatenate([w8] * nseq, axis=0)

        bias = jnp.concatenate([bias_ref[0:DEC_SEQ, :]] * nseq, axis=0)
    else:
        wmask = ri >= ci
        group_w = lambda g: w_ref[g]
        bias = bias_ref[...]
    w = [jnp.where(wmask, group_w(g), 0.0).astype(BF16) for g in range(GROUPS)]
    lng = lng_ref[...]
    for c in range(nchunk):
        rows = slice(c * CHUNK, (c + 1) * CHUNK)
        gv = gv_ref[rows, :].astype(F32)
        xc = gv - jnp.mean(gv, axis=-1, keepdims=True)
        y = xc * lax.rsqrt(jnp.mean(xc * xc, axis=-1, keepdims=True) + EPS) * lng
        if sample:
            gvn_refs[0][rows, :] = y
        yb = y.astype(BF16)
        for g in range(GROUPS):
            cols = slice(g * GROUP_DIM, (g + 1) * GROUP_DIM)
            mixed = _dot(w[g], yb[:, cols]) + bias[:, g:g + 1]
            a_ref[rows, cols] = (u_ref[rows, cols].astype(F32) * mixed).astype(BF16)


def _gmlp(z, lng, w, bias, *, sample):
    rows = CHUNK if sample else 8 * CHUNK
    nsteps = (N_SAMPLE if sample else N_PROMPT) // rows
    blk0 = (N_PROMPT // rows) if sample else 0
    n_out = N_SAMPLE if sample else N_PROMPT
    out_specs = [pl.BlockSpec((rows, GMLP_W), lambda s: (s, 0))]
    out_shape = [jax.ShapeDtypeStruct((n_out, GMLP_W), BF16)]
    if sample:
        out_specs.append(pl.BlockSpec((rows, GMLP_W), lambda s: (s, 0)))
        out_shape.append(jax.ShapeDtypeStruct((n_out, GMLP_W), F32))
    return pl.pallas_call(
        functools.partial(_gmlp_kernel, nchunk=rows // CHUNK, sample=sample),
        grid=(nsteps,),
        in_specs=[
            pl.BlockSpec((rows, GMLP_W), lambda s: (blk0 + s, Z_U // GMLP_W)),
            pl.BlockSpec((rows, GMLP_W), lambda s: (blk0 + s, Z_GV // GMLP_W)),
            pl.BlockSpec((1, GMLP_W), lambda s: (0, 0)),
            pl.BlockSpec((GROUPS, CHUNK, CHUNK), lambda s: (0, 0, 0)),
            pl.BlockSpec((CHUNK, GROUPS), lambda s: (0, 0)),
        ],
        out_specs=out_specs,
        out_shape=out_shape,
        compiler_params=pltpu.CompilerParams(
            dimension_semantics=("arbitrary",), vmem_limit_bytes=32 * MIB),
        name="gmlp_sample" if sample else "gmlp_prompt",
    )(z, z, lng, w, bias)


MG_TM = 256
MG_NPT = N_PROMPT // MG_TM
MG_NT = N_TOK // MG_TM


def _merge_kernel(xp_ref, xs_ref, ap_ref, as_ref, op_ref, os_ref, ga_ref, gb_ref,
                  wpa_ref, wpb_ref, wo_ref, x1_ref):
    i = pl.program_id(0)

    def body(x_ref, a_ref, o_ref):
        pa = _dot(a_ref[...], wpa_ref[...])
        pb = _dot(o_ref[...], wpb_ref[...])
        mix = ga_ref[...].astype(F32) * pa + gb_ref[...].astype(F32) * pb
        x1_ref[...] = x_ref[...] + _dot(mix.astype(BF16), wo_ref[...])

    @pl.when(i < MG_NPT)
    def _():
        body(xp_ref, ap_ref, op_ref)

    @pl.when(i >= MG_NPT)
    def _():
        body(xs_ref, as_ref, os_ref)


def _merge(xp, xs, a_p, a_s, o_p, o_s, z, wpa, wpb, wo):
    p_idx = lambda i: (jnp.minimum(i, MG_NPT - 1), 0)
    s_idx = lambda i: (jnp.maximum(i - MG_NPT, 0), 0)
    const = lambda i: (0, 0)
    resident = dict(pipeline_mode=pl.Buffered(1))
    return pl.pallas_call(
        _merge_kernel,
        grid=(MG_NT,),
        in_specs=[
            pl.BlockSpec((MG_TM, D_MODEL), p_idx),
            pl.BlockSpec((MG_TM, D_MODEL), s_idx),
            pl.BlockSpec((MG_TM, GMLP_W), p_idx),
            pl.BlockSpec((MG_TM, GMLP_W), s_idx),
            pl.BlockSpec((MG_TM, Q_W), p_idx),
            pl.BlockSpec((MG_TM, Q_W), s_idx),
            pl.BlockSpec((MG_TM, D_MODEL), lambda i: (i, Z_GA // D_MODEL)),
            pl.BlockSpec((MG_TM, D_MODEL), lambda i: (i, Z_GB // D_MODEL)),
            pl.BlockSpec((GMLP_W, D_MODEL), const, **resident),
            pl.BlockSpec((Q_W, D_MODEL), const, **resident),
            pl.BlockSpec((D_MODEL, D_MODEL), const, **resident),
        ],
        out_specs=pl.BlockSpec((MG_TM, D_MODEL), lambda i: (i, 0)),
        out_shape=jax.ShapeDtypeStruct((N_TOK, D_MODEL), F32),
        compiler_params=pltpu.CompilerParams(
            dimension_semantics=("arbitrary",), vmem_limit_bytes=48 * MIB),
        name="merge",
    )(xp, xs, a_p, a_s, o_p, o_s, z, z, wpa, wpb, wo)


FF_TM = 512
FF_TF = 512
FF_NPT = N_PROMPT // FF_TM
FF_NT = N_TOK // FF_TM
FF_NJ = D_FF // FF_TF


def _ffn_kernel(x1_ref, n2_ref, wg_ref, wu_ref, wd_ref, fg_ref, yp_ref, ys_ref, hh_ref, acc_ref):
    i = pl.program_id(0)
    j = pl.program_id(1)

    @pl.when(j == 0)
    def _():
        hh_ref[...] = _rms(x1_ref[...], n2_ref[...]).astype(BF16)
        acc_ref[...] = jnp.zeros_like(acc_ref)

    hh = hh_ref[...]
    act = jax.nn.silu(_dot(hh, wg_ref[...])) * _dot(hh, wu_ref[...])
    acc_ref[...] += _dot(act.astype(BF16), wd_ref[...])

    @pl.when(j == FF_NJ - 1)
    def _():
        y = _rms(x1_ref[...] + acc_ref[...], fg_ref[...])

        @pl.when(i < FF_NPT)
        def _():
            yp_ref[...] = y

        @pl.when(i >= FF_NPT)
        def _():
            ys_ref[...] = y


def _ffn(x1, n2, wg, wu, wd, fg):
    return pl.pallas_call(
        _ffn_kernel,
        grid=(FF_NT, FF_NJ),
        in_specs=[
            pl.BlockSpec((FF_TM, D_MODEL), lambda i, j: (i, 0)),
            pl.BlockSpec((1, D_MODEL), lambda i, j: (0, 0)),
            pl.BlockSpec((None, D_MODEL, FF_TF), lambda i, j: (j, 0, 0)),
            pl.BlockSpec((None, D_MODEL, FF_TF), lambda i, j: (j, 0, 0)),
            pl.BlockSpec((FF_TF, D_MODEL), lambda i, j: (j, 0)),
            pl.BlockSpec((1, D_MODEL), lambda i, j: (0, 0)),
        ],
        out_specs=[
            pl.BlockSpec((FF_TM, D_MODEL), lambda i, j: (jnp.minimum(i, FF_NPT - 1), 0)),
            pl.BlockSpec((FF_TM, D_MODEL), lambda i, j: (jnp.maximum(i - FF_NPT, 0), 0)),
        ],
        out_shape=[
            jax.ShapeDtypeStruct((N_PROMPT, D_MODEL), F32),
            jax.ShapeDtypeStruct((N_SAMPLE, D_MODEL), F32),
        ],
        scratch_shapes=[pltpu.VMEM((FF_TM, D_MODEL), BF16), pltpu.VMEM((FF_TM, D_MODEL), F32)],
        compiler_params=pltpu.CompilerParams(
            dimension_semantics=("arbitrary", "arbitrary"), vmem_limit_bytes=56 * MIB),
        name="ffn",
    )(x1, n2, wg, wu, wd, fg)


def kernel(x_prompt, x_sample, cache_k, cache_v, norm1_g, w_in, gmlp_norm_g, w_s, b_s, sinks,
           w_pa, w_pb, w_o, norm2_g, w_ff_gate, w_ff_up, w_ff_down, final_g):
    xp = x_prompt.reshape(N_PROMPT, D_MODEL)
    xs = x_sample.reshape(N_SAMPLE, D_MODEL)

    z, kv = _inproj(xp, xs, norm1_g.reshape(1, D_MODEL),
                    _col_tiles(w_in.reshape(D_MODEL, IN_W), IP_TN))

    sinks1 = sinks.reshape(N_Q_HEADS)
    o_p, pk, pv = _attn_prompt(sinks1, z, kv)
    o_s, k_win, v_win = _attn_sample(
        sinks1, z, cache_k.reshape(DEC_BATCH, WINDOW, KV_W),
        cache_v.reshape(DEC_BATCH, WINDOW, KV_W), kv)

    lng = gmlp_norm_g.reshape(1, GMLP_W)
    ws = w_s.reshape(GROUPS, CHUNK, CHUNK)
    bias = b_s.reshape(GROUPS, CHUNK).T
    (a_p,) = _gmlp(z, lng, ws, bias, sample=False)
    a_s, gvn_s = _gmlp(z, lng, ws, bias, sample=True)

    x1 = _merge(xp, xs, a_p, a_s, o_p, o_s, z,
                w_pa.reshape(GMLP_W, D_MODEL).astype(BF16),
                w_pb.reshape(Q_W, D_MODEL).astype(BF16),
                w_o.reshape(D_MODEL, D_MODEL).astype(BF16))

    y_p, y_s = _ffn(x1, norm2_g.reshape(1, D_MODEL),
                    _col_tiles(w_ff_gate.reshape(D_MODEL, D_FF), FF_TF),
                    _col_tiles(w_ff_up.reshape(D_MODEL, D_FF), FF_TF),
                    w_ff_down.reshape(D_FF, D_MODEL).astype(BF16),
                    final_g.reshape(1, D_MODEL))

    return (y_p.reshape(BATCH, SEQ, D_MODEL),
            y_s.reshape(DEC_BATCH, DEC_SEQ, D_MODEL),
            pk.reshape(1, BATCH, WINDOW, N_KV_HEADS, HEAD_DIM),
            pv.reshape(1, BATCH, WINDOW, N_KV_HEADS, HEAD_DIM),
            k_win.reshape(1, DEC_BATCH, WINDOW, N_KV_HEADS, HEAD_DIM),
            v_win.reshape(1, DEC_BATCH, WINDOW, N_KV_HEADS, HEAD_DIM),
            gvn_s.reshape(1, DEC_BATCH, DEC_SEQ, GROUPS, GROUP_DIM))
```

```python
import functools

import jax
import jax.numpy as jnp
from jax import lax
from jax.experimental import pallas as pl
from jax.experimental.pallas import tpu as pltpu

F32 = jnp.float32
BF16 = jnp.bfloat16

D_MODEL = 2048
BATCH = 4
SEQ = 2048
DEC_BATCH = 128
DEC_SEQ = 8
HEAD_DIM = 64
N_Q_HEADS = 16
N_KV_HEADS = 4
Q_PER_KV = N_Q_HEADS // N_KV_HEADS
WINDOW = 128
CHUNK = 128
GROUPS = 8
GROUP_DIM = 128
Q_W = N_Q_HEADS * HEAD_DIM
KV_W = N_KV_HEADS * HEAD_DIM
GMLP_W = GROUPS * GROUP_DIM
D_FF = 5632
IN_W = Q_W + 2 * KV_W + 2 * GMLP_W + 2 * D_MODEL
EPS = 1e-6
NEG = -1e30

N_PROMPT = BATCH * SEQ
N_SAMPLE = DEC_BATCH * DEC_SEQ
N_TOK = N_PROMPT + N_SAMPLE

Z_W = 2 * D_MODEL + Q_W + 2 * GMLP_W
Z_GA, Z_GB, Z_Q, Z_U, Z_GV = 0, 2048, 4096, 5120, 6144

MIB = 1024 * 1024


def _rms(x, g):
    return x * lax.rsqrt(jnp.mean(x * x, axis=-1, keepdims=True) + EPS) * g


def _dot(a, b):
    return jnp.dot(a, b, preferred_element_type=F32)


def _dot_t(a, b):
    return lax.dot_general(a, b, (((1,), (1,)), ((), ())), preferred_element_type=F32)


IP_TM = 1024
IP_TN = 512
IP_CH = 256
IP_NPT = N_PROMPT // IP_TM
IP_NT = N_TOK // IP_TM
IP_NJ = IN_W // IP_TN


def _inproj_zcol(j):
    return jnp.where(j < 2, 8 + j, jnp.where(j == 2, 9, jnp.where(j < 7, j + 7, j - 7)))


def _inproj_kernel(xp_ref, xs_ref, g_ref, w_ref, z_ref, kv_ref, h_ref):
    i = pl.program_id(0)
    j = pl.program_id(1)

    @pl.when((j == 0) & (i < IP_NPT))
    def _():
        h_ref[...] = _rms(xp_ref[...], g_ref[...]).astype(BF16)

    @pl.when((j == 0) & (i >= IP_NPT))
    def _():
        h_ref[...] = _rms(xs_ref[...], g_ref[...]).astype(BF16)

    def proj(out_ref, act):
        w = w_ref[...].astype(BF16)
        for c in range(IP_TM // IP_CH):
            rows = slice(c * IP_CH, (c + 1) * IP_CH)
            out_ref[rows, :] = act(_dot(h_ref[rows, :], w)).astype(out_ref.dtype)

    @pl.when(j < 2)
    def _():
        proj(z_ref, lambda v: v)

    @pl.when(j == 2)
    def _():
        proj(kv_ref, lambda v: v)

    @pl.when((j >= 3) & (j < 7))
    def _():
        proj(z_ref, jax.nn.gelu)

    @pl.when(j >= 7)
    def _():
        proj(z_ref, jax.nn.sigmoid)


def _inproj(xp, xs, g, w):
    return pl.pallas_call(
        _inproj_kernel,
        grid=(IP_NT, IP_NJ),
        in_specs=[
            pl.BlockSpec((IP_TM, D_MODEL), lambda i, j: (jnp.minimum(i, IP_NPT - 1), 0)),
            pl.BlockSpec((IP_TM, D_MODEL), lambda i, j: (jnp.maximum(i - IP_NPT, 0), 0),
                         pipeline_mode=pl.Buffered(1)),
            pl.BlockSpec((1, D_MODEL), lambda i, j: (0, 0)),
            pl.BlockSpec((D_MODEL, IP_TN), lambda i, j: (0, j)),
        ],
        out_specs=[
            pl.BlockSpec((IP_TM, IP_TN), lambda i, j: (i, _inproj_zcol(j))),
            pl.BlockSpec((IP_TM, 2 * KV_W), lambda i, j: (i, 0)),
        ],
        out_shape=[
            jax.ShapeDtypeStruct((N_TOK, Z_W), BF16),
            jax.ShapeDtypeStruct((N_TOK, 2 * KV_W), F32),
        ],
        scratch_shapes=[pltpu.VMEM((IP_TM, D_MODEL), BF16)],
        compiler_params=pltpu.CompilerParams(
            dimension_semantics=("arbitrary", "arbitrary"), vmem_limit_bytes=52 * MIB),
        name="inproj",
    )(xp, xs, g, w)


NBLK = SEQ // WINDOW


PA_QB = 4
PA_ROWS = PA_QB * WINDOW
PA_STEPS_PER_SEQ = NBLK // PA_QB
GRP_W = Q_PER_KV * HEAD_DIM


def _lane_group_rotations(x):
    x1 = pltpu.roll(x, HEAD_DIM, 1)
    return [x.astype(BF16), x1.astype(BF16),
            pltpu.roll(x, 2 * HEAD_DIM, 1).astype(BF16), pltpu.roll(x1, 2 * HEAD_DIM, 1).astype(BF16)]


def _attn_prompt_kernel(sink_ref, q_ref, kc_ref, kp_ref, o_ref, pk_ref, pv_ref):
    t = pl.program_id(0)
    kv = jnp.concatenate([kp_ref[...], kc_ref[...]], axis=0)
    k_rot = _lane_group_rotations(kv[:, :KV_W])
    v_rot = _lane_group_rotations(kv[:, KV_W:])
    lane_grp = lax.broadcasted_iota(jnp.int32, (WINDOW, GRP_W), 1) >> 6
    in_grp = [lane_grp == g for g in range(Q_PER_KV)]
    nrow = N_KV_HEADS * WINDOW
    key = lax.broadcasted_iota(jnp.int32, (2 * WINDOW, nrow), 0)
    qry = lax.broadcasted_iota(jnp.int32, (2 * WINDOW, nrow), 1) & (WINDOW - 1)
    band = (key >= qry) & (key <= qry + WINDOW)
    bias_inner = jnp.where(band, 0.0, NEG)
    bias_first = jnp.where(band & (key >= WINDOW), 0.0, NEG)

    for qb in range(PA_QB):
        is_first = ((t * PA_QB + qb) % NBLK) == 0
        bias = jnp.where(is_first, bias_first, bias_inner)
        q = q_ref[qb * WINDOW:(qb + 1) * WINDOW, :] * (HEAD_DIM ** -0.5)
        keys = slice(qb * WINDOW, (qb + 2) * WINDOW)
        outs = [[None] * Q_PER_KV for _ in range(N_KV_HEADS)]
        for rot in range(Q_PER_KV):
            lhs = jnp.concatenate(
                [jnp.where(in_grp[(h + rot) % Q_PER_KV], q[:, h * GRP_W:(h + 1) * GRP_W], 0)
                 for h in range(N_KV_HEADS)], axis=0)
            sink = jnp.concatenate(
                [jnp.full((1, WINDOW), sink_ref[h * Q_PER_KV + (h + rot) % Q_PER_KV], F32)
                 for h in range(N_KV_HEADS)], axis=1)
            s = _dot_t(k_rot[rot][keys], lhs) + bias
            m = jnp.maximum(jnp.max(s, axis=0, keepdims=True), sink)
            p = jnp.exp(s - m)
            denom = jnp.sum(p, axis=0, keepdims=True) + jnp.exp(sink - m)
            probs = (p / denom).astype(BF16)
            o = lax.dot_general(probs, v_rot[rot][keys], (((0,), (0,)), ((), ())),
                                preferred_element_type=F32)
            for h in range(N_KV_HEADS):
                outs[h][(h + rot) % Q_PER_KV] = o[h * WINDOW:(h + 1) * WINDOW]
        for h in range(N_KV_HEADS):
            oh = outs[h][Q_PER_KV - 1]
            for g in range(Q_PER_KV - 2, -1, -1):
                oh = jnp.where(in_grp[g], outs[h][g], oh)
            o_ref[qb * WINDOW:(qb + 1) * WINDOW, h * GRP_W:(h + 1) * GRP_W] = oh.astype(BF16)

    @pl.when(t % PA_STEPS_PER_SEQ == PA_STEPS_PER_SEQ - 1)
    def _():
        pk_ref[0] = kc_ref[PA_ROWS - WINDOW:PA_ROWS, :KV_W]
        pv_ref[0] = kc_ref[PA_ROWS - WINDOW:PA_ROWS, KV_W:]


def _attn_prompt(sinks, z, kv):
    return pl.pallas_call(
        _attn_prompt_kernel,
        grid=(N_PROMPT // PA_ROWS,),
        in_specs=[
            pl.BlockSpec(memory_space=pltpu.SMEM),
            pl.BlockSpec((PA_ROWS, Q_W), lambda t: (t, Z_Q // Q_W)),
            pl.BlockSpec((PA_ROWS, 2 * KV_W), lambda t: (t, 0)),
            pl.BlockSpec((WINDOW, 2 * KV_W), lambda t: (jnp.maximum(t * PA_QB - 1, 0), 0)),
        ],
        out_specs=[
            pl.BlockSpec((PA_ROWS, Q_W), lambda t: (t, 0)),
            pl.BlockSpec((1, WINDOW, KV_W), lambda t: (t // PA_STEPS_PER_SEQ, 0, 0)),
            pl.BlockSpec((1, WINDOW, KV_W), lambda t: (t // PA_STEPS_PER_SEQ, 0, 0)),
        ],
        out_shape=[
            jax.ShapeDtypeStruct((N_PROMPT, Q_W), BF16),
            jax.ShapeDtypeStruct((BATCH, WINDOW, KV_W), F32),
            jax.ShapeDtypeStruct((BATCH, WINDOW, KV_W), F32),
        ],
        compiler_params=pltpu.CompilerParams(
            dimension_semantics=("arbitrary",), vmem_limit_bytes=32 * MIB),
        name="attn_prompt",
    )(sinks, z, kv, kv)


SA_BT = 16
SA_ROWS = N_Q_HEADS * DEC_SEQ
SA_STEPS = DEC_BATCH // SA_BT


def _attn_sample_kernel(sink_ref, q_ref, ck_ref, cv_ref, kvn_ref, o_ref, kw_ref, vw_ref):
    lane_grp = lax.broadcasted_iota(jnp.int32, (SA_BT * DEC_SEQ, GRP_W), 1) >> 6
    in_grp = [lane_grp == g for g in range(Q_PER_KV)]
    qf = q_ref[...].astype(F32) * (HEAD_DIM ** -0.5)
    q_exp = []
    for h in range(N_KV_HEADS):
        q_h = qf[:, h * GRP_W:(h + 1) * GRP_W]
        for r in range(Q_PER_KV):
            shift = ((h - r) % Q_PER_KV) * HEAD_DIM
            moved = q_h if shift == 0 else pltpu.roll(q_h, shift, 1)
            q_exp.append(jnp.where(in_grp[h], moved, 0.0))
    lhs = jnp.concatenate([q_exp[hr][b * DEC_SEQ:(b + 1) * DEC_SEQ]
                           for b in range(SA_BT) for hr in range(N_Q_HEADS)], axis=0).astype(BF16)

    kvn = kvn_ref[...]
    k_new = kvn[:, :KV_W].astype(BF16)
    v_new = kvn[:, KV_W:].astype(BF16)
    seq_rows = lambda b: slice(b * SA_ROWS, (b + 1) * SA_ROWS)
    s_c = jnp.concatenate([_dot_t(lhs[seq_rows(b)], ck_ref[b].astype(BF16)) for b in range(SA_BT)],
                          axis=0)
    s_n = _dot_t(lhs, k_new)

    nrow = SA_BT * SA_ROWS
    row = lax.broadcasted_iota(jnp.int32, (nrow, WINDOW), 0)
    col = lax.broadcasted_iota(jnp.int32, (nrow, WINDOW), 1)
    tq = row & (DEC_SEQ - 1)
    s_c = jnp.where(col >= tq, s_c, NEG)
    s_n = jnp.where(((col >> 3) == (row >> 7)) & ((col & (DEC_SEQ - 1)) <= tq), s_n, NEG)
    sink_seq = jnp.concatenate([jnp.full((DEC_SEQ, 1), sink_ref[hr], F32) for hr in range(N_Q_HEADS)],
                               axis=0)
    sink = jnp.concatenate([sink_seq] * SA_BT, axis=0)
    m = jnp.maximum(jnp.maximum(jnp.max(s_c, axis=-1, keepdims=True),
                                jnp.max(s_n, axis=-1, keepdims=True)), sink)
    p_c = jnp.exp(s_c - m)
    p_n = jnp.exp(s_n - m)
    denom = (jnp.sum(p_c, axis=-1, keepdims=True) + jnp.sum(p_n, axis=-1, keepdims=True)
             + jnp.exp(sink - m))
    p_c = (p_c / denom).astype(BF16)
    p_n = (p_n / denom).astype(BF16)
    o = jnp.concatenate([_dot(p_c[seq_rows(b)], cv_ref[b].astype(BF16)) for b in range(SA_BT)],
                        axis=0) + _dot(p_n, v_new)

    for h in range(N_KV_HEADS):
        parts = []
        for r in range(Q_PER_KV):
            hr = h * Q_PER_KV + r
            o_hr = jnp.concatenate(
                [o[b * SA_ROWS + hr * DEC_SEQ:b * SA_ROWS + (hr + 1) * DEC_SEQ] for b in range(SA_BT)],
                axis=0)
            shift = ((r - h) % Q_PER_KV) * HEAD_DIM
            parts.append(o_hr if shift == 0 else pltpu.roll(o_hr, shift, 1))
        o_h = parts[Q_PER_KV - 1]
        for g in range(Q_PER_KV - 2, -1, -1):
            o_h = jnp.where(in_grp[g], parts[g], o_h)
        o_ref[:, h * GRP_W:(h + 1) * GRP_W] = o_h.astype(BF16)

    for b in range(SA_BT):
        kw_ref[b, 0:WINDOW - DEC_SEQ, :] = ck_ref[b, DEC_SEQ:WINDOW, :]
        kw_ref[b, WINDOW - DEC_SEQ:WINDOW, :] = kvn[b * DEC_SEQ:(b + 1) * DEC_SEQ, :KV_W]
        vw_ref[b, 0:WINDOW - DEC_SEQ, :] = cv_ref[b, DEC_SEQ:WINDOW, :]
        vw_ref[b, WINDOW - DEC_SEQ:WINDOW, :] = kvn[b * DEC_SEQ:(b + 1) * DEC_SEQ, KV_W:]


def _attn_sample(sinks, z, ck, cv, kv):
    step_tok = SA_BT * DEC_SEQ
    blk0 = N_PROMPT // step_tok
    return pl.pallas_call(
        _attn_sample_kernel,
        grid=(SA_STEPS,),
        in_specs=[
            pl.BlockSpec(memory_space=pltpu.SMEM),
            pl.BlockSpec((step_tok, Q_W), lambda s: (blk0 + s, Z_Q // Q_W)),
            pl.BlockSpec((SA_BT, WINDOW, KV_W), lambda s: (s, 0, 0)),
            pl.BlockSpec((SA_BT, WINDOW, KV_W), lambda s: (s, 0, 0)),
            pl.BlockSpec((step_tok, 2 * KV_W), lambda s: (blk0 + s, 0)),
        ],
        out_specs=[
            pl.BlockSpec((step_tok, Q_W), lambda s: (s, 0)),
            pl.BlockSpec((SA_BT, WINDOW, KV_W), lambda s: (s, 0, 0)),
            pl.BlockSpec((SA_BT, WINDOW, KV_W), lambda s: (s, 0, 0)),
        ],
        out_shape=[
            jax.ShapeDtypeStruct((N_SAMPLE, Q_W), BF16),
            jax.ShapeDtypeStruct((DEC_BATCH, WINDOW, KV_W), F32),
            jax.ShapeDtypeStruct((DEC_BATCH, WINDOW, KV_W), F32),
        ],
        compiler_params=pltpu.CompilerParams(
            dimension_semantics=("arbitrary",), vmem_limit_bytes=32 * MIB),
        name="attn_sample",
    )(sinks, z, ck, cv, kv)


def _gmlp_kernel(u_ref, gv_ref, lng_ref, w_ref, bias_ref, a_ref, *gvn_refs, nchunk, sample):
    ri = lax.broadcasted_iota(jnp.int32, (CHUNK, CHUNK), 0)
    ci = lax.broadcasted_iota(jnp.int32, (CHUNK, CHUNK), 1)
    nseq = CHUNK // DEC_SEQ
    if sample:
        wmask = ((ri >> 3) == (ci >> 3)) & ((ri & (DEC_SEQ - 1)) >= (ci & (DEC_SEQ - 1)))
        corner_lanes = lax.broadcasted_iota(jnp.int32, (DEC_SEQ, CHUNK), 1) < DEC_SEQ

        def group_w(g):
            w8 = jnp.where(corner_lanes, w_ref[g, 0:DEC_SEQ, :], 0.0)
            shift = DEC_SEQ
            while shift < CHUNK:
                w8 = w8 + pltpu.roll(w8, shift, 1)
                shift *= 2
            return jnp.concatenate([w8] * nseq, axis=0)

        bias = jnp.concatenate([bias_ref[0:DEC_SEQ, :]] * nseq, axis=0)
    else:
        wmask = ri >= ci
        group_w = lambda g: w_ref[g]
        bias = bias_ref[...]
    w = [jnp.where(wmask, group_w(g), 0.0).astype(BF16) for g in range(GROUPS)]
    lng = lng_ref[...]
    for c in range(nchunk):
        rows = slice(c * CHUNK, (c + 1) * CHUNK)
        gv = gv_ref[rows, :].astype(F32)
        xc = gv - jnp.mean(gv, axis=-1, keepdims=True)
        y = xc * lax.rsqrt(jnp.mean(xc * xc, axis=-1, keepdims=True) + EPS) * lng
        if sample:
            gvn_refs[0][rows, :] = y
        yb = y.astype(BF16)
        for g in range(GROUPS):
            cols = slice(g * GROUP_DIM, (g + 1) * GROUP_DIM)
            mixed = _dot(w[g], yb[:, cols]) + bias[:, g:g + 1]
            a_ref[rows, cols] = (u_ref[rows, cols].astype(F32) * mixed).astype(BF16)


def _gmlp(z, lng, w, bias, *, sample):
    rows = CHUNK if sample else 8 * CHUNK
    nsteps = (N_SAMPLE if sample else N_PROMPT) // rows
    blk0 = (N_PROMPT // rows) if sample else 0
    n_out = N_SAMPLE if sample else N_PROMPT
    out_specs = [pl.BlockSpec((rows, GMLP_W), lambda s: (s, 0))]
    out_shape = [jax.ShapeDtypeStruct((n_out, GMLP_W), BF16)]
    if sample:
        out_specs.append(pl.BlockSpec((rows, GMLP_W), lambda s: (s, 0)))
        out_shape.append(jax.ShapeDtypeStruct((n_out, GMLP_W), F32))
    return pl.pallas_call(
        functools.partial(_gmlp_kernel, nchunk=rows // CHUNK, sample=sample),
        grid=(nsteps,),
        in_specs=[
            pl.BlockSpec((rows, GMLP_W), lambda s: (blk0 + s, Z_U // GMLP_W)),
            pl.BlockSpec((rows, GMLP_W), lambda s: (blk0 + s, Z_GV // GMLP_W)),
            pl.BlockSpec((1, GMLP_W), lambda s: (0, 0)),
            pl.BlockSpec((GROUPS, CHUNK, CHUNK), lambda s: (0, 0, 0)),
            pl.BlockSpec((CHUNK, GROUPS), lambda s: (0, 0)),
        ],
        out_specs=out_specs,
        out_shape=out_shape,
        compiler_params=pltpu.CompilerParams(
            dimension_semantics=("arbitrary",), vmem_limit_bytes=32 * MIB),
        name="gmlp_sample" if sample else "gmlp_prompt",
    )(z, z, lng, w, bias)


MG_TM = 256
MG_NPT = N_PROMPT // MG_TM
MG_NT = N_TOK // MG_TM


def _merge_kernel(xp_ref, xs_ref, ap_ref, as_ref, op_ref, os_ref, ga_ref, gb_ref,
                  wpa_ref, wpb_ref, wo_ref, x1_ref):
    i = pl.program_id(0)

    def body(x_ref, a_ref, o_ref):
        pa = _dot(a_ref[...], wpa_ref[...])
        pb = _dot(o_ref[...], wpb_ref[...])
        mix = ga_ref[...].astype(F32) * pa + gb_ref[...].astype(F32) * pb
        x1_ref[...] = x_ref[...] + _dot(mix.astype(BF16), wo_ref[...])

    @pl.when(i < MG_NPT)
    def _():
        body(xp_ref, ap_ref, op_ref)

    @pl.when(i >= MG_NPT)
    def _():
        body(xs_ref, as_ref, os_ref)


def _merge(xp, xs, a_p, a_s, o_p, o_s, z, wpa, wpb, wo):
    p_idx = lambda i: (jnp.minimum(i, MG_NPT - 1), 0)
    s_idx = lambda i: (jnp.maximum(i - MG_NPT, 0), 0)
    const = lambda i: (0, 0)
    resident = dict(pipeline_mode=pl.Buffered(1))
    return pl.pallas_call(
        _merge_kernel,
        grid=(MG_NT,),
        in_specs=[
            pl.BlockSpec((MG_TM, D_MODEL), p_idx),
            pl.BlockSpec((MG_TM, D_MODEL), s_idx),
            pl.BlockSpec((MG_TM, GMLP_W), p_idx),
            pl.BlockSpec((MG_TM, GMLP_W), s_idx),
            pl.BlockSpec((MG_TM, Q_W), p_idx),
            pl.BlockSpec((MG_TM, Q_W), s_idx),
            pl.BlockSpec((MG_TM, D_MODEL), lambda i: (i, Z_GA // D_MODEL)),
            pl.BlockSpec((MG_TM, D_MODEL), lambda i: (i, Z_GB // D_MODEL)),
            pl.BlockSpec((GMLP_W, D_MODEL), const, **resident),
            pl.BlockSpec((Q_W, D_MODEL), const, **resident),
            pl.BlockSpec((D_MODEL, D_MODEL), const, **resident),
        ],
        out_specs=pl.BlockSpec((MG_TM, D_MODEL), lambda i: (i, 0)),
        out_shape=jax.ShapeDtypeStruct((N_TOK, D_MODEL), F32),
        compiler_params=pltpu.CompilerParams(
            dimension_semantics=("arbitrary",), vmem_limit_bytes=48 * MIB),
        name="merge",
    )(xp, xs, a_p, a_s, o_p, o_s, z, z, wpa, wpb, wo)


FF_TM = 1024
FF_TF = 512
FF_CH = 256
FF_NJ = D_FF // FF_TF


def _ffn_kernel(x1_ref, n2_ref, wg_ref, wu_ref, wd_ref, fg_ref, y_ref, hh_ref):
    j = pl.program_id(1)

    def step(first, last):
        wg = wg_ref[...]
        wu = wu_ref[...]
        wd = wd_ref[...]
        for c in range(FF_TM // FF_CH):
            rows = slice(c * FF_CH, (c + 1) * FF_CH)
            if first:
                hh = _rms(x1_ref[rows, :], n2_ref[...]).astype(BF16)
                hh_ref[rows, :] = hh
            else:
                hh = hh_ref[rows, :]
            act = jax.nn.silu(_dot(hh, wg)) * _dot(hh, wu)
            acc = (x1_ref if first else y_ref)[rows, :] + _dot(act.astype(BF16), wd)
            y_ref[rows, :] = _rms(acc, fg_ref[...]) if last else acc

    @pl.when(j == 0)
    def _():
        step(True, False)

    @pl.when((j > 0) & (j < FF_NJ - 1))
    def _():
        step(False, False)

    @pl.when(j == FF_NJ - 1)
    def _():
        step(False, True)


def _ffn(x1, row0, nrows, n2, wg, wu, wd, fg, name):
    blk0 = row0 // FF_TM
    return pl.pallas_call(
        _ffn_kernel,
        grid=(nrows // FF_TM, FF_NJ),
        in_specs=[
            pl.BlockSpec((FF_TM, D_MODEL), lambda i, j: (blk0 + i, 0)),
            pl.BlockSpec((1, D_MODEL), lambda i, j: (0, 0)),
            pl.BlockSpec((D_MODEL, FF_TF), lambda i, j: (0, j)),
            pl.BlockSpec((D_MODEL, FF_TF), lambda i, j: (0, j)),
            pl.BlockSpec((FF_TF, D_MODEL), lambda i, j: (j, 0)),
            pl.BlockSpec((1, D_MODEL), lambda i, j: (0, 0)),
        ],
        out_specs=pl.BlockSpec((FF_TM, D_MODEL), lambda i, j: (i, 0)),
        out_shape=jax.ShapeDtypeStruct((nrows, D_MODEL), F32),
        scratch_shapes=[pltpu.VMEM((FF_TM, D_MODEL), BF16)],
        compiler_params=pltpu.CompilerParams(
            dimension_semantics=("arbitrary", "arbitrary"), vmem_limit_bytes=58 * MIB),
        name=name,
    )(x1, n2, wg, wu, wd, fg)


def kernel(x_prompt, x_sample, cache_k, cache_v, norm1_g, w_in, gmlp_norm_g, w_s, b_s, sinks,
           w_pa, w_pb, w_o, norm2_g, w_ff_gate, w_ff_up, w_ff_down, final_g):
    xp = x_prompt.reshape(N_PROMPT, D_MODEL)
    xs = x_sample.reshape(N_SAMPLE, D_MODEL)

    z, kv = _inproj(xp, xs, norm1_g.reshape(1, D_MODEL), w_in.reshape(D_MODEL, IN_W))

    sinks1 = sinks.reshape(N_Q_HEADS)
    o_p, pk, pv = _attn_prompt(sinks1, z, kv)
    o_s, k_win, v_win = _attn_sample(
        sinks1, z, cache_k.reshape(DEC_BATCH, WINDOW, KV_W),
        cache_v.reshape(DEC_BATCH, WINDOW, KV_W), kv)

    lng = gmlp_norm_g.reshape(1, GMLP_W)
    ws = w_s.reshape(GROUPS, CHUNK, CHUNK)
    bias = b_s.reshape(GROUPS, CHUNK).T
    (a_p,) = _gmlp(z, lng, ws, bias, sample=False)
    a_s, gvn_s = _gmlp(z, lng, ws, bias, sample=True)

    x1 = _merge(xp, xs, a_p, a_s, o_p, o_s, z,
                w_pa.reshape(GMLP_W, D_MODEL).astype(BF16),
                w_pb.reshape(Q_W, D_MODEL).astype(BF16),
                w_o.reshape(D_MODEL, D_MODEL).astype(BF16))

    ffn_w = (norm2_g.reshape(1, D_MODEL),
             w_ff_gate.reshape(D_MODEL, D_FF).astype(BF16),
             w_ff_up.reshape(D_MODEL, D_FF).astype(BF16),
             w_ff_down.reshape(D_FF, D_MODEL).astype(BF16),
             final_g.reshape(1, D_MODEL))
    y_p = _ffn(x1, 0, N_PROMPT, *ffn_w, name="ffn_prompt")
    y_s = _ffn(x1, N_PROMPT, N_SAMPLE, *ffn_w, name="ffn_sample")

    return (y_p.reshape(BATCH, SEQ, D_MODEL),
            y_s.reshape(DEC_BATCH, DEC_SEQ, D_MODEL),
            pk.reshape(1, BATCH, WINDOW, N_KV_HEADS, HEAD_DIM),
            pv.reshape(1, BATCH, WINDOW, N_KV_HEADS, HEAD_DIM),
            k_win.reshape(1, DEC_BATCH, WINDOW, N_KV_HEADS, HEAD_DIM),
            v_win.reshape(1, DEC_BATCH, WINDOW, N_KV_HEADS, HEAD_DIM),
            gvn_s.reshape(1, DEC_BATCH, DEC_SEQ, GROUPS, GROUP_DIM))
```

```python
import functools

import jax
import jax.numpy as jnp
from jax import lax
from jax.experimental import pallas as pl
from jax.experimental.pallas import tpu as pltpu

F32 = jnp.float32
BF16 = jnp.bfloat16

D_MODEL = 2048
BATCH = 4
SEQ = 2048
DEC_BATCH = 128
DEC_SEQ = 8
HEAD_DIM = 64
N_Q_HEADS = 16
N_KV_HEADS = 4
Q_PER_KV = N_Q_HEADS // N_KV_HEADS
WINDOW = 128
CHUNK = 128
GROUPS = 8
GROUP_DIM = 128
Q_W = N_Q_HEADS * HEAD_DIM
KV_W = N_KV_HEADS * HEAD_DIM
GMLP_W = GROUPS * GROUP_DIM
D_FF = 5632
IN_W = Q_W + 2 * KV_W + 2 * GMLP_W + 2 * D_MODEL
EPS = 1e-6
NEG = -1e30

N_PROMPT = BATCH * SEQ
N_SAMPLE = DEC_BATCH * DEC_SEQ
N_TOK = N_PROMPT + N_SAMPLE

Z_W = 2 * D_MODEL + Q_W + 2 * GMLP_W
Z_GA, Z_GB, Z_Q, Z_U, Z_GV = 0, 2048, 4096, 5120, 6144

MIB = 1024 * 1024


def _rms(x, g):
    return x * lax.rsqrt(jnp.mean(x * x, axis=-1, keepdims=True) + EPS) * g


def _dot(a, b):
    return jnp.dot(a, b, preferred_element_type=F32)


def _dot_t(a, b):
    return lax.dot_general(a, b, (((1,), (1,)), ((), ())), preferred_element_type=F32)


IP_TM = 1024
IP_TN = 512
IP_CH = 256
IP_NPT = N_PROMPT // IP_TM
IP_NT = N_TOK // IP_TM
IP_NJ = IN_W // IP_TN


def _inproj_zcol(j):
    return jnp.where(j < 2, 8 + j, jnp.where(j == 2, 9, jnp.where(j < 7, j + 7, j - 7)))


def _inproj_kernel(xp_ref, xs_ref, g_ref, w_ref, z_ref, kv_ref, h_ref):
    i = pl.program_id(0)
    j = pl.program_id(1)

    @pl.when((j == 0) & (i < IP_NPT))
    def _():
        h_ref[...] = _rms(xp_ref[...], g_ref[...]).astype(BF16)

    @pl.when((j == 0) & (i >= IP_NPT))
    def _():
        h_ref[...] = _rms(xs_ref[...], g_ref[...]).astype(BF16)

    def proj(out_ref, act):
        w = w_ref[...].astype(BF16)
        for c in range(IP_TM // IP_CH):
            rows = slice(c * IP_CH, (c + 1) * IP_CH)
            out_ref[rows, :] = act(_dot(h_ref[rows, :], w)).astype(out_ref.dtype)

    @pl.when(j < 2)
    def _():
        proj(z_ref, lambda v: v)

    @pl.when(j == 2)
    def _():
        proj(kv_ref, lambda v: v)

    @pl.when((j >= 3) & (j < 7))
    def _():
        proj(z_ref, jax.nn.gelu)

    @pl.when(j >= 7)
    def _():
        proj(z_ref, jax.nn.sigmoid)


def _inproj(xp, xs, g, w):
    return pl.pallas_call(
        _inproj_kernel,
        grid=(IP_NT, IP_NJ),
        in_specs=[
            pl.BlockSpec((IP_TM, D_MODEL), lambda i, j: (jnp.minimum(i, IP_NPT - 1), 0)),
            pl.BlockSpec((IP_TM, D_MODEL), lambda i, j: (jnp.maximum(i - IP_NPT, 0), 0),
                         pipeline_mode=pl.Buffered(1)),
            pl.BlockSpec((1, D_MODEL), lambda i, j: (0, 0)),
            pl.BlockSpec((D_MODEL, IP_TN), lambda i, j: (0, j)),
        ],
        out_specs=[
            pl.BlockSpec((IP_TM, IP_TN), lambda i, j: (i, _inproj_zcol(j))),
            pl.BlockSpec((IP_TM, 2 * KV_W), lambda i, j: (i, 0)),
        ],
        out_shape=[
            jax.ShapeDtypeStruct((N_TOK, Z_W), BF16),
            jax.ShapeDtypeStruct((N_TOK, 2 * KV_W), F32),
        ],
        scratch_shapes=[pltpu.VMEM((IP_TM, D_MODEL), BF16)],
        compiler_params=pltpu.CompilerParams(
            dimension_semantics=("arbitrary", "arbitrary"), vmem_limit_bytes=52 * MIB),
        name="inproj",
    )(xp, xs, g, w)


NBLK = SEQ // WINDOW


PA_QB = 4
PA_ROWS = PA_QB * WINDOW
PA_STEPS_PER_SEQ = NBLK // PA_QB
GRP_W = Q_PER_KV * HEAD_DIM


def _lane_group_rotations(x):
    x1 = pltpu.roll(x, HEAD_DIM, 1)
    return [x.astype(BF16), x1.astype(BF16),
            pltpu.roll(x, 2 * HEAD_DIM, 1).astype(BF16), pltpu.roll(x1, 2 * HEAD_DIM, 1).astype(BF16)]


def _attn_prompt_kernel(sink_ref, q_ref, kc_ref, kp_ref, o_ref, pk_ref, pv_ref):
    t = pl.program_id(0)
    kv = jnp.concatenate([kp_ref[...], kc_ref[...]], axis=0)
    k_rot = _lane_group_rotations(kv[:, :KV_W])
    v_rot = _lane_group_rotations(kv[:, KV_W:])
    lane_grp = lax.broadcasted_iota(jnp.int32, (WINDOW, GRP_W), 1) >> 6
    in_grp = [lane_grp == g for g in range(Q_PER_KV)]
    nrow = N_KV_HEADS * WINDOW
    key = lax.broadcasted_iota(jnp.int32, (2 * WINDOW, nrow), 0)
    qry = lax.broadcasted_iota(jnp.int32, (2 * WINDOW, nrow), 1) & (WINDOW - 1)
    band = (key >= qry) & (key <= qry + WINDOW)
    bias_inner = jnp.where(band, 0.0, NEG)
    bias_first = jnp.where(band & (key >= WINDOW), 0.0, NEG)

    for qb in range(PA_QB):
        is_first = ((t * PA_QB + qb) % NBLK) == 0
        bias = jnp.where(is_first, bias_first, bias_inner)
        q = q_ref[qb * WINDOW:(qb + 1) * WINDOW, :] * (HEAD_DIM ** -0.5)
        keys = slice(qb * WINDOW, (qb + 2) * WINDOW)
        outs = [[None] * Q_PER_KV for _ in range(N_KV_HEADS)]
        for rot in range(Q_PER_KV):
            lhs = jnp.concatenate(
                [jnp.where(in_grp[(h + rot) % Q_PER_KV], q[:, h * GRP_W:(h + 1) * GRP_W], 0)
                 for h in range(N_KV_HEADS)], axis=0)
            sink = jnp.concatenate(
                [jnp.full((1, WINDOW), sink_ref[h * Q_PER_KV + (h + rot) % Q_PER_KV], F32)
                 for h in range(N_KV_HEADS)], axis=1)
            s = _dot_t(k_rot[rot][keys], lhs) + bias
            m = jnp.maximum(jnp.max(s, axis=0, keepdims=True), sink)
            p = jnp.exp(s - m)
            denom = jnp.sum(p, axis=0, keepdims=True) + jnp.exp(sink - m)
            probs = (p / denom).astype(BF16)
            o = lax.dot_general(probs, v_rot[rot][keys], (((0,), (0,)), ((), ())),
                                preferred_element_type=F32)
            for h in range(N_KV_HEADS):
                outs[h][(h + rot) % Q_PER_KV] = o[h * WINDOW:(h + 1) * WINDOW]
        for h in range(N_KV_HEADS):
            oh = outs[h][Q_PER_KV - 1]
            for g in range(Q_PER_KV - 2, -1, -1):
                oh = jnp.where(in_grp[g], outs[h][g], oh)
            o_ref[qb * WINDOW:(qb + 1) * WINDOW, h * GRP_W:(h + 1) * GRP_W] = oh.astype(BF16)

    @pl.when(t % PA_STEPS_PER_SEQ == PA_STEPS_PER_SEQ - 1)
    def _():
        pk_ref[0] = kc_ref[PA_ROWS - WINDOW:PA_ROWS, :KV_W]
        pv_ref[0] = kc_ref[PA_ROWS - WINDOW:PA_ROWS, KV_W:]


def _attn_prompt(sinks, z, kv):
    return pl.pallas_call(
        _attn_prompt_kernel,
        grid=(N_PROMPT // PA_ROWS,),
        in_specs=[
            pl.BlockSpec(memory_space=pltpu.SMEM),
            pl.BlockSpec((PA_ROWS, Q_W), lambda t: (t, Z_Q // Q_W)),
            pl.BlockSpec((PA_ROWS, 2 * KV_W), lambda t: (t, 0)),
            pl.BlockSpec((WINDOW, 2 * KV_W), lambda t: (jnp.maximum(t * PA_QB - 1, 0), 0)),
        ],
        out_specs=[
            pl.BlockSpec((PA_ROWS, Q_W), lambda t: (t, 0)),
            pl.BlockSpec((1, WINDOW, KV_W), lambda t: (t // PA_STEPS_PER_SEQ, 0, 0)),
            pl.BlockSpec((1, WINDOW, KV_W), lambda t: (t // PA_STEPS_PER_SEQ, 0, 0)),
        ],
        out_shape=[
            jax.ShapeDtypeStruct((N_PROMPT, Q_W), BF16),
            jax.ShapeDtypeStruct((BATCH, WINDOW, KV_W), F32),
            jax.ShapeDtypeStruct((BATCH, WINDOW, KV_W), F32),
        ],
        compiler_params=pltpu.CompilerParams(
            dimension_semantics=("arbitrary",), vmem_limit_bytes=32 * MIB),
        name="attn_prompt",
    )(sinks, z, kv, kv)


SA_BT = 16
SA_ROWS = N_Q_HEADS * DEC_SEQ
SA_STEPS = DEC_BATCH // SA_BT


def _attn_sample_kernel(sink_ref, q_ref, ck_ref, cv_ref, kvn_ref, o_ref, kw_ref, vw_ref):
    lane_grp = lax.broadcasted_iota(jnp.int32, (SA_BT * DEC_SEQ, GRP_W), 1) >> 6
    in_grp = [lane_grp == g for g in range(Q_PER_KV)]
    qf = q_ref[...].astype(F32) * (HEAD_DIM ** -0.5)
    q_exp = []
    for h in range(N_KV_HEADS):
        q_h = qf[:, h * GRP_W:(h + 1) * GRP_W]
        for r in range(Q_PER_KV):
            shift = ((h - r) % Q_PER_KV) * HEAD_DIM
            moved = q_h if shift == 0 else pltpu.roll(q_h, shift, 1)
            q_exp.append(jnp.where(in_grp[h], moved, 0.0))
    lhs = jnp.concatenate([q_exp[hr][b * DEC_SEQ:(b + 1) * DEC_SEQ]
                           for b in range(SA_BT) for hr in range(N_Q_HEADS)], axis=0).astype(BF16)

    kvn = kvn_ref[...]
    k_new = kvn[:, :KV_W].astype(BF16)
    v_new = kvn[:, KV_W:].astype(BF16)
    seq_rows = lambda b: slice(b * SA_ROWS, (b + 1) * SA_ROWS)
    s_c = jnp.concatenate([_dot_t(lhs[seq_rows(b)], ck_ref[b].astype(BF16)) for b in range(SA_BT)],
                          axis=0)
    s_n = _dot_t(lhs, k_new)

    nrow = SA_BT * SA_ROWS
    row = lax.broadcasted_iota(jnp.int32, (nrow, WINDOW), 0)
    col = lax.broadcasted_iota(jnp.int32, (nrow, WINDOW), 1)
    tq = row & (DEC_SEQ - 1)
    s_c = jnp.where(col >= tq, s_c, NEG)
    s_n = jnp.where(((col >> 3) == (row >> 7)) & ((col & (DEC_SEQ - 1)) <= tq), s_n, NEG)
    sink_seq = jnp.concatenate([jnp.full((DEC_SEQ, 1), sink_ref[hr], F32) for hr in range(N_Q_HEADS)],
                               axis=0)
    sink = jnp.concatenate([sink_seq] * SA_BT, axis=0)
    m = jnp.maximum(jnp.maximum(jnp.max(s_c, axis=-1, keepdims=True),
                                jnp.max(s_n, axis=-1, keepdims=True)), sink)
    p_c = jnp.exp(s_c - m)
    p_n = jnp.exp(s_n - m)
    denom = (jnp.sum(p_c, axis=-1, keepdims=True) + jnp.sum(p_n, axis=-1, keepdims=True)
             + jnp.exp(sink - m))
    p_c = (p_c / denom).astype(BF16)
    p_n = (p_n / denom).astype(BF16)
    o = jnp.concatenate([_dot(p_c[seq_rows(b)], cv_ref[b].astype(BF16)) for b in range(SA_BT)],
                        axis=0) + _dot(p_n, v_new)

    for h in range(N_KV_HEADS):
        parts = []
        for r in range(Q_PER_KV):
            hr = h * Q_PER_KV + r
            o_hr = jnp.concatenate(
                [o[b * SA_ROWS + hr * DEC_SEQ:b * SA_ROWS + (hr + 1) * DEC_SEQ] for b in range(SA_BT)],
                axis=0)
            shift = ((r - h) % Q_PER_KV) * HEAD_DIM
            parts.append(o_hr if shift == 0 else pltpu.roll(o_hr, shift, 1))
        o_h = parts[Q_PER_KV - 1]
        for g in range(Q_PER_KV - 2, -1, -1):
            o_h = jnp.where(in_grp[g], parts[g], o_h)
        o_ref[:, h * GRP_W:(h + 1) * GRP_W] = o_h.astype(BF16)

    for b in range(SA_BT):
        kw_ref[b, 0:WINDOW - DEC_SEQ, :] = ck_ref[b, DEC_SEQ:WINDOW, :]
        kw_ref[b, WINDOW - DEC_SEQ:WINDOW, :] = kvn[b * DEC_SEQ:(b + 1) * DEC_SEQ, :KV_W]
        vw_ref[b, 0:WINDOW - DEC_SEQ, :] = cv_ref[b, DEC_SEQ:WINDOW, :]
        vw_ref[b, WINDOW - DEC_SEQ:WINDOW, :] = kvn[b * DEC_SEQ:(b + 1) * DEC_SEQ, KV_W:]


def _attn_sample(sinks, z, ck, cv, kv):
    step_tok = SA_BT * DEC_SEQ
    blk0 = N_PROMPT // step_tok
    return pl.pallas_call(
        _attn_sample_kernel,
        grid=(SA_STEPS,),
        in_specs=[
            pl.BlockSpec(memory_space=pltpu.SMEM),
            pl.BlockSpec((step_tok, Q_W), lambda s: (blk0 + s, Z_Q // Q_W)),
            pl.BlockSpec((SA_BT, WINDOW, KV_W), lambda s: (s, 0, 0)),
            pl.BlockSpec((SA_BT, WINDOW, KV_W), lambda s: (s, 0, 0)),
            pl.BlockSpec((step_tok, 2 * KV_W), lambda s: (blk0 + s, 0)),
        ],
        out_specs=[
            pl.BlockSpec((step_tok, Q_W), lambda s: (s, 0)),
            pl.BlockSpec((SA_BT, WINDOW, KV_W), lambda s: (s, 0, 0)),
            pl.BlockSpec((SA_BT, WINDOW, KV_W), lambda s: (s, 0, 0)),
        ],
        out_shape=[
            jax.ShapeDtypeStruct((N_SAMPLE, Q_W), BF16),
            jax.ShapeDtypeStruct((DEC_BATCH, WINDOW, KV_W), F32),
            jax.ShapeDtypeStruct((DEC_BATCH, WINDOW, KV_W), F32),
        ],
        compiler_params=pltpu.CompilerParams(
            dimension_semantics=("arbitrary",), vmem_limit_bytes=32 * MIB),
        name="attn_sample",
    )(sinks, z, ck, cv, kv)


def _gmlp_kernel(u_ref, gv_ref, lng_ref, w_ref, bias_ref, a_ref, *gvn_refs, nchunk, sample):
    ri = lax.broadcasted_iota(jnp.int32, (CHUNK, CHUNK), 0)
    ci = lax.broadcasted_iota(jnp.int32, (CHUNK, CHUNK), 1)
    nseq = CHUNK // DEC_SEQ
    if sample:
        wmask = ((ri >> 3) == (ci >> 3)) & ((ri & (DEC_SEQ - 1)) >= (ci & (DEC_SEQ - 1)))
        corner_lanes = lax.broadcasted_iota(jnp.int32, (DEC_SEQ, CHUNK), 1) < DEC_SEQ

        def group_w(g):
            w8 = jnp.where(corner_lanes, w_ref[g, 0:DEC_SEQ, :], 0.0)
            shift = DEC_SEQ
            while shift < CHUNK:
                w8 = w8 + pltpu.roll(w8, shift, 1)
                shift *= 2
            return jnp.concatenate([w8] * nseq, axis=0)

        bias = jnp.concatenate([bias_ref[0:DEC_SEQ, :]] * nseq, axis=0)
    else:
        wmask = ri >= ci
        group_w = lambda g: w_ref[g]
        bias = bias_ref[...]
    w = [jnp.where(wmask, group_w(g), 0.0).astype(BF16) for g in range(GROUPS)]
    lng = lng_ref[...]
    for c in range(nchunk):
        rows = slice(c * CHUNK, (c + 1) * CHUNK)
        gv = gv_ref[rows, :].astype(F32)
        xc = gv - jnp.mean(gv, axis=-1, keepdims=True)
        y = xc * lax.rsqrt(jnp.mean(xc * xc, axis=-1, keepdims=True) + EPS) * lng
        if sample:
            gvn_refs[0][rows, :] = y
        yb = y.astype(BF16)
        for g in range(GROUPS):
            cols = slice(g * GROUP_DIM, (g + 1) * GROUP_DIM)
            mixed = _dot(w[g], yb[:, cols]) + bias[:, g:g + 1]
            a_ref[rows, cols] = (u_ref[rows, cols].astype(F32) * mixed).astype(BF16)


def _gmlp(z, lng, w, bias, *, sample):
    rows = CHUNK if sample else 8 * CHUNK
    nsteps = (N_SAMPLE if sample else N_PROMPT) // rows
    blk0 = (N_PROMPT // rows) if sample else 0
    n_out = N_SAMPLE if sample else N_PROMPT
    out_specs = [pl.BlockSpec((rows, GMLP_W), lambda s: (s, 0))]
    out_shape = [jax.ShapeDtypeStruct((n_out, GMLP_W), BF16)]
    if sample:
        out_specs.append(pl.BlockSpec((rows, GMLP_W), lambda s: (s, 0)))
        out_shape.append(jax.ShapeDtypeStruct((n_out, GMLP_W), F32))
    return pl.pallas_call(
        functools.partial(_gmlp_kernel, nchunk=rows // CHUNK, sample=sample),
        grid=(nsteps,),
        in_specs=[
            pl.BlockSpec((rows, GMLP_W), lambda s: (blk0 + s, Z_U // GMLP_W)),
            pl.BlockSpec((rows, GMLP_W), lambda s: (blk0 + s, Z_GV // GMLP_W)),
            pl.BlockSpec((1, GMLP_W), lambda s: (0, 0)),
            pl.BlockSpec((GROUPS, CHUNK, CHUNK), lambda s: (0, 0, 0)),
            pl.BlockSpec((CHUNK, GROUPS), lambda s: (0, 0)),
        ],
        out_specs=out_specs,
        out_shape=out_shape,
        compiler_params=pltpu.CompilerParams(
            dimension_semantics=("arbitrary",), vmem_limit_bytes=32 * MIB),
        name="gmlp_sample" if sample else "gmlp_prompt",
    )(z, z, lng, w, bias)


MG_TM = 256
MG_NPT = N_PROMPT // MG_TM
MG_NT = N_TOK // MG_TM


def _merge_kernel(xp_ref, xs_ref, ap_ref, as_ref, op_ref, os_ref, ga_ref, gb_ref,
                  wpa_ref, wpb_ref, wo_ref, x1_ref):
    i = pl.program_id(0)

    def body(x_ref, a_ref, o_ref):
        pa = _dot(a_ref[...], wpa_ref[...])
        pb = _dot(o_ref[...], wpb_ref[...])
        mix = ga_ref[...].astype(F32) * pa + gb_ref[...].astype(F32) * pb
        x1_ref[...] = x_ref[...] + _dot(mix.astype(BF16), wo_ref[...])

    @pl.when(i < MG_NPT)
    def _():
        body(xp_ref, ap_ref, op_ref)

    @pl.when(i >= MG_NPT)
    def _():
        body(xs_ref, as_ref, os_ref)


def _merge(xp, xs, a_p, a_s, o_p, o_s, z, wpa, wpb, wo):
    p_idx = lambda i: (jnp.minimum(i, MG_NPT - 1), 0)
    s_idx = lambda i: (jnp.maximum(i - MG_NPT, 0), 0)
    const = lambda i: (0, 0)
    resident = dict(pipeline_mode=pl.Buffered(1))
    return pl.pallas_call(
        _merge_kernel,
        grid=(MG_NT,),
        in_specs=[
            pl.BlockSpec((MG_TM, D_MODEL), p_idx),
            pl.BlockSpec((MG_TM, D_MODEL), s_idx),
            pl.BlockSpec((MG_TM, GMLP_W), p_idx),
            pl.BlockSpec((MG_TM, GMLP_W), s_idx),
            pl.BlockSpec((MG_TM, Q_W), p_idx),
            pl.BlockSpec((MG_TM, Q_W), s_idx),
            pl.BlockSpec((MG_TM, D_MODEL), lambda i: (i, Z_GA // D_MODEL)),
            pl.BlockSpec((MG_TM, D_MODEL), lambda i: (i, Z_GB // D_MODEL)),
            pl.BlockSpec((GMLP_W, D_MODEL), const, **resident),
            pl.BlockSpec((Q_W, D_MODEL), const, **resident),
            pl.BlockSpec((D_MODEL, D_MODEL), const, **resident),
        ],
        out_specs=pl.BlockSpec((MG_TM, D_MODEL), lambda i: (i, 0)),
        out_shape=jax.ShapeDtypeStruct((N_TOK, D_MODEL), F32),
        compiler_params=pltpu.CompilerParams(
            dimension_semantics=("arbitrary",), vmem_limit_bytes=48 * MIB),
        name="merge",
    )(xp, xs, a_p, a_s, o_p, o_s, z, z, wpa, wpb, wo)


FF_TM = 1024
FF_TF = 512
FF_NJ = D_FF // FF_TF


def _ffn_kernel(x1_ref, n2_ref, wg_ref, wu_ref, wd_ref, fg_ref, y_ref, hh_ref, act_ref):
    j = pl.program_id(1)

    def gate_up(slot):
        hh = hh_ref[...]
        act = jax.nn.silu(_dot(hh, wg_ref[...])) * _dot(hh, wu_ref[...])
        act_ref[slot] = act.astype(BF16)

    def down(slot, first, last):
        acc = (x1_ref if first else y_ref)[...] + _dot(act_ref[slot], wd_ref[...])
        y_ref[...] = _rms(acc, fg_ref[...]) if last else acc

    @pl.when(j == 0)
    def _():
        hh_ref[...] = _rms(x1_ref[...], n2_ref[...]).astype(BF16)
        gate_up(0)

    @pl.when(j == 1)
    def _():
        gate_up(1)
        down(0, True, False)

    @pl.when((j > 1) & (j < FF_NJ))
    def _():
        gate_up(j % 2)
        down(1 - j % 2, False, False)

    @pl.when(j == FF_NJ)
    def _():
        down((FF_NJ - 1) % 2, False, True)


def _ffn(x1, row0, nrows, n2, wg, wu, wd, fg, name):
    blk0 = row0 // FF_TM
    return pl.pallas_call(
        _ffn_kernel,
        grid=(nrows // FF_TM, FF_NJ + 1),
        in_specs=[
            pl.BlockSpec((FF_TM, D_MODEL), lambda i, j: (blk0 + i, 0)),
            pl.BlockSpec((1, D_MODEL), lambda i, j: (0, 0)),
            pl.BlockSpec((D_MODEL, FF_TF), lambda i, j: (0, jnp.minimum(j, FF_NJ - 1))),
            pl.BlockSpec((D_MODEL, FF_TF), lambda i, j: (0, jnp.minimum(j, FF_NJ - 1))),
            pl.BlockSpec((FF_TF, D_MODEL), lambda i, j: (jnp.maximum(j - 1, 0), 0)),
            pl.BlockSpec((1, D_MODEL), lambda i, j: (0, 0)),
        ],
        out_specs=pl.BlockSpec((FF_TM, D_MODEL), lambda i, j: (i, 0)),
        out_shape=jax.ShapeDtypeStruct((nrows, D_MODEL), F32),
        scratch_shapes=[pltpu.VMEM((FF_TM, D_MODEL), BF16), pltpu.VMEM((2, FF_TM, FF_TF), BF16)],
        compiler_params=pltpu.CompilerParams(
            dimension_semantics=("arbitrary", "arbitrary"), vmem_limit_bytes=58 * MIB),
        name=name,
    )(x1, n2, wg, wu, wd, fg)


def kernel(x_prompt, x_sample, cache_k, cache_v, norm1_g, w_in, gmlp_norm_g, w_s, b_s, sinks,
           w_pa, w_pb, w_o, norm2_g, w_ff_gate, w_ff_up, w_ff_down, final_g):
    xp = x_prompt.reshape(N_PROMPT, D_MODEL)
    xs = x_sample.reshape(N_SAMPLE, D_MODEL)

    z, kv = _inproj(xp, xs, norm1_g.reshape(1, D_MODEL), w_in.reshape(D_MODEL, IN_W))

    sinks1 = sinks.reshape(N_Q_HEADS)
    o_p, pk, pv = _attn_prompt(sinks1, z, kv)
    o_s, k_win, v_win = _attn_sample(
        sinks1, z, cache_k.reshape(DEC_BATCH, WINDOW, KV_W),
        cache_v.reshape(DEC_BATCH, WINDOW, KV_W), kv)

    lng = gmlp_norm_g.reshape(1, GMLP_W)
    ws = w_s.reshape(GROUPS, CHUNK, CHUNK)
    bias = b_s.reshape(GROUPS, CHUNK).T
    (a_p,) = _gmlp(z, lng, ws, bias, sample=False)
    a_s, gvn_s = _gmlp(z, lng, ws, bias, sample=True)

    x1 = _merge(xp, xs, a_p, a_s, o_p, o_s, z,
                w_pa.reshape(GMLP_W, D_MODEL).astype(BF16),
                w_pb.reshape(Q_W, D_MODEL).astype(BF16),
                w_o.reshape(D_MODEL, D_MODEL).astype(BF16))

    ffn_w = (norm2_g.reshape(1, D_MODEL),
             w_ff_gate.reshape(D_MODEL, D_FF).astype(BF16),
             w_ff_up.reshape(D_MODEL, D_FF).astype(BF16),
             w_ff_down.reshape(D_FF, D_MODEL).astype(BF16),
             final_g.reshape(1, D_MODEL))
    y_p = _ffn(x1, 0, N_PROMPT, *ffn_w, name="ffn_prompt")
    y_s = _ffn(x1, N_PROMPT, N_SAMPLE, *ffn_w, name="ffn_sample")

    return (y_p.reshape(BATCH, SEQ, D_MODEL),
            y_s.reshape(DEC_BATCH, DEC_SEQ, D_MODEL),
            pk.reshape(1, BATCH, WINDOW, N_KV_HEADS, HEAD_DIM),
            pv.reshape(1, BATCH, WINDOW, N_KV_HEADS, HEAD_DIM),
            k_win.reshape(1, DEC_BATCH, WINDOW, N_KV_HEADS, HEAD_DIM),
            v_win.reshape(1, DEC_BATCH, WINDOW, N_KV_HEADS, HEAD_DIM),
            gvn_s.reshape(1, DEC_BATCH, DEC_SEQ, GROUPS, GROUP_DIM))
```

```python
import functools

import jax
import jax.numpy as jnp
from jax import lax
from jax.experimental import pallas as pl
from jax.experimental.pallas import tpu as pltpu

F32 = jnp.float32
BF16 = jnp.bfloat16

D_MODEL = 2048
BATCH = 4
SEQ = 2048
DEC_BATCH = 128
DEC_SEQ = 8
HEAD_DIM = 64
N_Q_HEADS = 16
N_KV_HEADS = 4
Q_PER_KV = N_Q_HEADS // N_KV_HEADS
WINDOW = 128
CHUNK = 128
GROUPS = 8
GROUP_DIM = 128
Q_W = N_Q_HEADS * HEAD_DIM
KV_W = N_KV_HEADS * HEAD_DIM
GMLP_W = GROUPS * GROUP_DIM
D_FF = 5632
IN_W = Q_W + 2 * KV_W + 2 * GMLP_W + 2 * D_MODEL
EPS = 1e-6
NEG = -1e30

N_PROMPT = BATCH * SEQ
N_SAMPLE = DEC_BATCH * DEC_SEQ
N_TOK = N_PROMPT + N_SAMPLE

Z_W = 2 * D_MODEL + Q_W + 2 * GMLP_W
Z_GA, Z_GB, Z_Q, Z_U, Z_GV = 0, 2048, 4096, 5120, 6144

MIB = 1024 * 1024


def _rms(x, g):
    return x * lax.rsqrt(jnp.mean(x * x, axis=-1, keepdims=True) + EPS) * g


def _dot(a, b):
    return jnp.dot(a, b, preferred_element_type=F32)


def _dot_t(a, b):
    return lax.dot_general(a, b, (((1,), (1,)), ((), ())), preferred_element_type=F32)


IP_TM = 1024
IP_TN = 512
IP_CH = 256
IP_NPT = N_PROMPT // IP_TM
IP_NT = N_TOK // IP_TM
IP_NJ = IN_W // IP_TN


def _inproj_zcol(j):
    return jnp.where(j < 2, 8 + j, jnp.where(j == 2, 9, jnp.where(j < 7, j + 7, j - 7)))


def _inproj_kernel(xp_ref, xs_ref, g_ref, w_ref, z_ref, kv_ref, h_ref):
    i = pl.program_id(0)
    j = pl.program_id(1)

    @pl.when((j == 0) & (i < IP_NPT))
    def _():
        h_ref[...] = _rms(xp_ref[...], g_ref[...]).astype(BF16)

    @pl.when((j == 0) & (i >= IP_NPT))
    def _():
        h_ref[...] = _rms(xs_ref[...], g_ref[...]).astype(BF16)

    def proj(out_ref, act):
        w = w_ref[...].astype(BF16)
        for c in range(IP_TM // IP_CH):
            rows = slice(c * IP_CH, (c + 1) * IP_CH)
            out_ref[rows, :] = act(_dot(h_ref[rows, :], w)).astype(out_ref.dtype)

    @pl.when(j < 2)
    def _():
        proj(z_ref, lambda v: v)

    @pl.when(j == 2)
    def _():
        proj(kv_ref, lambda v: v)

    @pl.when((j >= 3) & (j < 7))
    def _():
        proj(z_ref, jax.nn.gelu)

    @pl.when(j >= 7)
    def _():
        proj(z_ref, jax.nn.sigmoid)


def _inproj(xp, xs, g, w):
    return pl.pallas_call(
        _inproj_kernel,
        grid=(IP_NT, IP_NJ),
        in_specs=[
            pl.BlockSpec((IP_TM, D_MODEL), lambda i, j: (jnp.minimum(i, IP_NPT - 1), 0)),
            pl.BlockSpec((IP_TM, D_MODEL), lambda i, j: (jnp.maximum(i - IP_NPT, 0), 0),
                         pipeline_mode=pl.Buffered(1)),
            pl.BlockSpec((1, D_MODEL), lambda i, j: (0, 0)),
            pl.BlockSpec((D_MODEL, IP_TN), lambda i, j: (0, j)),
        ],
        out_specs=[
            pl.BlockSpec((IP_TM, IP_TN), lambda i, j: (i, _inproj_zcol(j))),
            pl.BlockSpec((IP_TM, 2 * KV_W), lambda i, j: (i, 0)),
        ],
        out_shape=[
            jax.ShapeDtypeStruct((N_TOK, Z_W), BF16),
            jax.ShapeDtypeStruct((N_TOK, 2 * KV_W), F32),
        ],
        scratch_shapes=[pltpu.VMEM((IP_TM, D_MODEL), BF16)],
        compiler_params=pltpu.CompilerParams(
            dimension_semantics=("arbitrary", "arbitrary"), vmem_limit_bytes=52 * MIB),
        name="inproj",
    )(xp, xs, g, w)


NBLK = SEQ // WINDOW


PA_QB = 4
PA_ROWS = PA_QB * WINDOW
PA_STEPS_PER_SEQ = NBLK // PA_QB
GRP_W = Q_PER_KV * HEAD_DIM


def _lane_group_rotations(x):
    x1 = pltpu.roll(x, HEAD_DIM, 1)
    return [x.astype(BF16), x1.astype(BF16),
            pltpu.roll(x, 2 * HEAD_DIM, 1).astype(BF16), pltpu.roll(x1, 2 * HEAD_DIM, 1).astype(BF16)]


def _cast_blocks(src_refs, dst_refs):
    for src, dst in zip(src_refs, dst_refs, strict=True):
        dst[...] = src[...].astype(BF16)


def _cast_specs(weights, nsteps):
    specs = [pl.BlockSpec((w.shape[0] // nsteps, w.shape[1]), lambda t: (t, 0)) for w in weights]
    shapes = [jax.ShapeDtypeStruct(w.shape, BF16) for w in weights]
    return specs, shapes


def _attn_prompt_kernel(sink_ref, q_ref, kc_ref, kp_ref, *refs):
    ncast = (len(refs) - 3) // 2
    o_ref, pk_ref, pv_ref = refs[ncast:ncast + 3]
    _cast_blocks(refs[:ncast], refs[ncast + 3:])
    t = pl.program_id(0)
    kv = jnp.concatenate([kp_ref[...], kc_ref[...]], axis=0)
    k_rot = _lane_group_rotations(kv[:, :KV_W])
    v_rot = _lane_group_rotations(kv[:, KV_W:])
    lane_grp = lax.broadcasted_iota(jnp.int32, (WINDOW, GRP_W), 1) >> 6
    in_grp = [lane_grp == g for g in range(Q_PER_KV)]
    nrow = N_KV_HEADS * WINDOW
    key = lax.broadcasted_iota(jnp.int32, (2 * WINDOW, nrow), 0)
    qry = lax.broadcasted_iota(jnp.int32, (2 * WINDOW, nrow), 1) & (WINDOW - 1)
    band = (key >= qry) & (key <= qry + WINDOW)
    bias_inner = jnp.where(band, 0.0, NEG)
    bias_first = jnp.where(band & (key >= WINDOW), 0.0, NEG)

    for qb in range(PA_QB):
        is_first = ((t * PA_QB + qb) % NBLK) == 0
        bias = jnp.where(is_first, bias_first, bias_inner)
        q = q_ref[qb * WINDOW:(qb + 1) * WINDOW, :] * (HEAD_DIM ** -0.5)
        keys = slice(qb * WINDOW, (qb + 2) * WINDOW)
        outs = [[None] * Q_PER_KV for _ in range(N_KV_HEADS)]
        for rot in range(Q_PER_KV):
            lhs = jnp.concatenate(
                [jnp.where(in_grp[(h + rot) % Q_PER_KV], q[:, h * GRP_W:(h + 1) * GRP_W], 0)
                 for h in range(N_KV_HEADS)], axis=0)
            sink = jnp.concatenate(
                [jnp.full((1, WINDOW), sink_ref[h * Q_PER_KV + (h + rot) % Q_PER_KV], F32)
                 for h in range(N_KV_HEADS)], axis=1)
            s = _dot_t(k_rot[rot][keys], lhs) + bias
            m = jnp.maximum(jnp.max(s, axis=0, keepdims=True), sink)
            p = jnp.exp(s - m)
            denom = jnp.sum(p, axis=0, keepdims=True) + jnp.exp(sink - m)
            probs = (p / denom).astype(BF16)
            o = lax.dot_general(probs, v_rot[rot][keys], (((0,), (0,)), ((), ())),
                                preferred_element_type=F32)
            for h in range(N_KV_HEADS):
                outs[h][(h + rot) % Q_PER_KV] = o[h * WINDOW:(h + 1) * WINDOW]
        for h in range(N_KV_HEADS):
            oh = outs[h][Q_PER_KV - 1]
            for g in range(Q_PER_KV - 2, -1, -1):
                oh = jnp.where(in_grp[g], outs[h][g], oh)
            o_ref[qb * WINDOW:(qb + 1) * WINDOW, h * GRP_W:(h + 1) * GRP_W] = oh.astype(BF16)

    @pl.when(t % PA_STEPS_PER_SEQ == PA_STEPS_PER_SEQ - 1)
    def _():
        pk_ref[0] = kc_ref[PA_ROWS - WINDOW:PA_ROWS, :KV_W]
        pv_ref[0] = kc_ref[PA_ROWS - WINDOW:PA_ROWS, KV_W:]


def _attn_prompt(sinks, z, kv, cast_weights):
    nsteps = N_PROMPT // PA_ROWS
    cast_specs, cast_shapes = _cast_specs(cast_weights, nsteps)
    return pl.pallas_call(
        _attn_prompt_kernel,
        grid=(nsteps,),
        in_specs=[
            pl.BlockSpec(memory_space=pltpu.SMEM),
            pl.BlockSpec((PA_ROWS, Q_W), lambda t: (t, Z_Q // Q_W)),
            pl.BlockSpec((PA_ROWS, 2 * KV_W), lambda t: (t, 0)),
            pl.BlockSpec((WINDOW, 2 * KV_W), lambda t: (jnp.maximum(t * PA_QB - 1, 0), 0)),
        ] + cast_specs,
        out_specs=[
            pl.BlockSpec((PA_ROWS, Q_W), lambda t: (t, 0)),
            pl.BlockSpec((1, WINDOW, KV_W), lambda t: (t // PA_STEPS_PER_SEQ, 0, 0)),
            pl.BlockSpec((1, WINDOW, KV_W), lambda t: (t // PA_STEPS_PER_SEQ, 0, 0)),
        ] + cast_specs,
        out_shape=[
            jax.ShapeDtypeStruct((N_PROMPT, Q_W), BF16),
            jax.ShapeDtypeStruct((BATCH, WINDOW, KV_W), F32),
            jax.ShapeDtypeStruct((BATCH, WINDOW, KV_W), F32),
        ] + cast_shapes,
        compiler_params=pltpu.CompilerParams(
            dimension_semantics=("arbitrary",), vmem_limit_bytes=48 * MIB),
        name="attn_prompt",
    )(sinks, z, kv, kv, *cast_weights)


SA_BT = 16
SA_ROWS = N_Q_HEADS * DEC_SEQ
SA_STEPS = DEC_BATCH // SA_BT


def _attn_sample_kernel(sink_ref, q_ref, ck_ref, cv_ref, kvn_ref, o_ref, kw_ref, vw_ref):
    lane_grp = lax.broadcasted_iota(jnp.int32, (SA_BT * DEC_SEQ, GRP_W), 1) >> 6
    in_grp = [lane_grp == g for g in range(Q_PER_KV)]
    qf = q_ref[...].astype(F32) * (HEAD_DIM ** -0.5)
    q_exp = []
    for h in range(N_KV_HEADS):
        q_h = qf[:, h * GRP_W:(h + 1) * GRP_W]
        for r in range(Q_PER_KV):
            shift = ((h - r) % Q_PER_KV) * HEAD_DIM
            moved = q_h if shift == 0 else pltpu.roll(q_h, shift, 1)
            q_exp.append(jnp.where(in_grp[h], moved, 0.0))
    lhs = jnp.concatenate([q_exp[hr][b * DEC_SEQ:(b + 1) * DEC_SEQ]
                           for b in range(SA_BT) for hr in range(N_Q_HEADS)], axis=0).astype(BF16)

    kvn = kvn_ref[...]
    k_new = kvn[:, :KV_W].astype(BF16)
    v_new = kvn[:, KV_W:].astype(BF16)
    seq_rows = lambda b: slice(b * SA_ROWS, (b + 1) * SA_ROWS)
    s_c = jnp.concatenate([_dot_t(lhs[seq_rows(b)], ck_ref[b].astype(BF16)) for b in range(SA_BT)],
                          axis=0)
    s_n = _dot_t(lhs, k_new)

    nrow = SA_BT * SA_ROWS
    row = lax.broadcasted_iota(jnp.int32, (nrow, WINDOW), 0)
    col = lax.broadcasted_iota(jnp.int32, (nrow, WINDOW), 1)
    tq = row & (DEC_SEQ - 1)
    s_c = jnp.where(col >= tq, s_c, NEG)
    s_n = jnp.where(((col >> 3) == (row >> 7)) & ((col & (DEC_SEQ - 1)) <= tq), s_n, NEG)
    sink_seq = jnp.concatenate([jnp.full((DEC_SEQ, 1), sink_ref[hr], F32) for hr in range(N_Q_HEADS)],
                               axis=0)
    sink = jnp.concatenate([sink_seq] * SA_BT, axis=0)
    m = jnp.maximum(jnp.maximum(jnp.max(s_c, axis=-1, keepdims=True),
                                jnp.max(s_n, axis=-1, keepdims=True)), sink)
    p_c = jnp.exp(s_c - m)
    p_n = jnp.exp(s_n - m)
    denom = (jnp.sum(p_c, axis=-1, keepdims=True) + jnp.sum(p_n, axis=-1, keepdims=True)
             + jnp.exp(sink - m))
    p_c = (p_c / denom).astype(BF16)
    p_n = (p_n / denom).astype(BF16)
    o = jnp.concatenate([_dot(p_c[seq_rows(b)], cv_ref[b].astype(BF16)) for b in range(SA_BT)],
                        axis=0) + _dot(p_n, v_new)

    for h in range(N_KV_HEADS):
        parts = []
        for r in range(Q_PER_KV):
            hr = h * Q_PER_KV + r
            o_hr = jnp.concatenate(
                [o[b * SA_ROWS + hr * DEC_SEQ:b * SA_ROWS + (hr + 1) * DEC_SEQ] for b in range(SA_BT)],
                axis=0)
            shift = ((r - h) % Q_PER_KV) * HEAD_DIM
            parts.append(o_hr if shift == 0 else pltpu.roll(o_hr, shift, 1))
        o_h = parts[Q_PER_KV - 1]
        for g in range(Q_PER_KV - 2, -1, -1):
            o_h = jnp.where(in_grp[g], parts[g], o_h)
        o_ref[:, h * GRP_W:(h + 1) * GRP_W] = o_h.astype(BF16)

    for b in range(SA_BT):
        kw_ref[b, 0:WINDOW - DEC_SEQ, :] = ck_ref[b, DEC_SEQ:WINDOW, :]
        kw_ref[b, WINDOW - DEC_SEQ:WINDOW, :] = kvn[b * DEC_SEQ:(b + 1) * DEC_SEQ, :KV_W]
        vw_ref[b, 0:WINDOW - DEC_SEQ, :] = cv_ref[b, DEC_SEQ:WINDOW, :]
        vw_ref[b, WINDOW - DEC_SEQ:WINDOW, :] = kvn[b * DEC_SEQ:(b + 1) * DEC_SEQ, KV_W:]


def _attn_sample(sinks, z, ck, cv, kv):
    step_tok = SA_BT * DEC_SEQ
    blk0 = N_PROMPT // step_tok
    return pl.pallas_call(
        _attn_sample_kernel,
        grid=(SA_STEPS,),
        in_specs=[
            pl.BlockSpec(memory_space=pltpu.SMEM),
            pl.BlockSpec((step_tok, Q_W), lambda s: (blk0 + s, Z_Q // Q_W)),
            pl.BlockSpec((SA_BT, WINDOW, KV_W), lambda s: (s, 0, 0)),
            pl.BlockSpec((SA_BT, WINDOW, KV_W), lambda s: (s, 0, 0)),
            pl.BlockSpec((step_tok, 2 * KV_W), lambda s: (blk0 + s, 0)),
        ],
        out_specs=[
            pl.BlockSpec((step_tok, Q_W), lambda s: (s, 0)),
            pl.BlockSpec((SA_BT, WINDOW, KV_W), lambda s: (s, 0, 0)),
            pl.BlockSpec((SA_BT, WINDOW, KV_W), lambda s: (s, 0, 0)),
        ],
        out_shape=[
            jax.ShapeDtypeStruct((N_SAMPLE, Q_W), BF16),
            jax.ShapeDtypeStruct((DEC_BATCH, WINDOW, KV_W), F32),
            jax.ShapeDtypeStruct((DEC_BATCH, WINDOW, KV_W), F32),
        ],
        compiler_params=pltpu.CompilerParams(
            dimension_semantics=("arbitrary",), vmem_limit_bytes=32 * MIB),
        name="attn_sample",
    )(sinks, z, ck, cv, kv)


def _gmlp_kernel(u_ref, gv_ref, lng_ref, w_ref, bias_ref, *refs, nchunk, sample, ncast):
    nout = 2 if sample else 1
    a_ref, *gvn_refs = refs[ncast:ncast + nout]
    _cast_blocks(refs[:ncast], refs[ncast + nout:])
    ri = lax.broadcasted_iota(jnp.int32, (CHUNK, CHUNK), 0)
    ci = lax.broadcasted_iota(jnp.int32, (CHUNK, CHUNK), 1)
    nseq = CHUNK // DEC_SEQ
    if sample:
        wmask = ((ri >> 3) == (ci >> 3)) & ((ri & (DEC_SEQ - 1)) >= (ci & (DEC_SEQ - 1)))
        corner_lanes = lax.broadcasted_iota(jnp.int32, (DEC_SEQ, CHUNK), 1) < DEC_SEQ

        def group_w(g):
            w8 = jnp.where(corner_lanes, w_ref[g, 0:DEC_SEQ, :], 0.0)
            shift = DEC_SEQ
            while shift < CHUNK:
                w8 = w8 + pltpu.roll(w8, shift, 1)
                shift *= 2
            return jnp.concatenate([w8] * nseq, axis=0)

        bias = jnp.concatenate([bias_ref[0:DEC_SEQ, :]] * nseq, axis=0)
    else:
        wmask = ri >= ci
        group_w = lambda g: w_ref[g]
        bias = bias_ref[...]
    w = [jnp.where(wmask, group_w(g), 0.0).astype(BF16) for g in range(GROUPS)]
    lng = lng_ref[...]
    for c in range(nchunk):
        rows = slice(c * CHUNK, (c + 1) * CHUNK)
        gv = gv_ref[rows, :].astype(F32)
        xc = gv - jnp.mean(gv, axis=-1, keepdims=True)
        y = xc * lax.rsqrt(jnp.mean(xc * xc, axis=-1, keepdims=True) + EPS) * lng
        if sample:
            gvn_refs[0][rows, :] = y
        yb = y.astype(BF16)
        for g in range(GROUPS):
            cols = slice(g * GROUP_DIM, (g + 1) * GROUP_DIM)
            mixed = _dot(w[g], yb[:, cols]) + bias[:, g:g + 1]
            a_ref[rows, cols] = (u_ref[rows, cols].astype(F32) * mixed).astype(BF16)


def _gmlp(z, lng, w, bias, *, sample, cast_weights=()):
    rows = CHUNK if sample else 8 * CHUNK
    nsteps = (N_SAMPLE if sample else N_PROMPT) // rows
    blk0 = (N_PROMPT // rows) if sample else 0
    n_out = N_SAMPLE if sample else N_PROMPT
    out_specs = [pl.BlockSpec((rows, GMLP_W), lambda s: (s, 0))]
    out_shape = [jax.ShapeDtypeStruct((n_out, GMLP_W), BF16)]
    if sample:
        out_specs.append(pl.BlockSpec((rows, GMLP_W), lambda s: (s, 0)))
        out_shape.append(jax.ShapeDtypeStruct((n_out, GMLP_W), F32))
    cast_specs, cast_shapes = _cast_specs(cast_weights, nsteps)
    return pl.pallas_call(
        functools.partial(_gmlp_kernel, nchunk=rows // CHUNK, sample=sample, ncast=len(cast_weights)),
        grid=(nsteps,),
        in_specs=[
            pl.BlockSpec((rows, GMLP_W), lambda s: (blk0 + s, Z_U // GMLP_W)),
            pl.BlockSpec((rows, GMLP_W), lambda s: (blk0 + s, Z_GV // GMLP_W)),
            pl.BlockSpec((1, GMLP_W), lambda s: (0, 0)),
            pl.BlockSpec((GROUPS, CHUNK, CHUNK), lambda s: (0, 0, 0)),
            pl.BlockSpec((CHUNK, GROUPS), lambda s: (0, 0)),
        ] + cast_specs,
        out_specs=out_specs + cast_specs,
        out_shape=out_shape + cast_shapes,
        compiler_params=pltpu.CompilerParams(
            dimension_semantics=("arbitrary",), vmem_limit_bytes=40 * MIB),
        name="gmlp_sample" if sample else "gmlp_prompt",
    )(z, z, lng, w, bias, *cast_weights)


MG_TM = 256
MG_NPT = N_PROMPT // MG_TM
MG_NT = N_TOK // MG_TM


def _merge_kernel(xp_ref, xs_ref, ap_ref, as_ref, op_ref, os_ref, ga_ref, gb_ref,
                  wpa_ref, wpb_ref, wo_ref, x1_ref):
    i = pl.program_id(0)

    def body(x_ref, a_ref, o_ref):
        pa = _dot(a_ref[...], wpa_ref[...])
        pb = _dot(o_ref[...], wpb_ref[...])
        mix = ga_ref[...].astype(F32) * pa + gb_ref[...].astype(F32) * pb
        x1_ref[...] = x_ref[...] + _dot(mix.astype(BF16), wo_ref[...])

    @pl.when(i < MG_NPT)
    def _():
        body(xp_ref, ap_ref, op_ref)

    @pl.when(i >= MG_NPT)
    def _():
        body(xs_ref, as_ref, os_ref)


def _merge(xp, xs, a_p, a_s, o_p, o_s, z, wpa, wpb, wo):
    p_idx = lambda i: (jnp.minimum(i, MG_NPT - 1), 0)
    s_idx = lambda i: (jnp.maximum(i - MG_NPT, 0), 0)
    const = lambda i: (0, 0)
    resident = dict(pipeline_mode=pl.Buffered(1))
    return pl.pallas_call(
        _merge_kernel,
        grid=(MG_NT,),
        in_specs=[
            pl.BlockSpec((MG_TM, D_MODEL), p_idx),
            pl.BlockSpec((MG_TM, D_MODEL), s_idx),
            pl.BlockSpec((MG_TM, GMLP_W), p_idx),
            pl.BlockSpec((MG_TM, GMLP_W), s_idx),
            pl.BlockSpec((MG_TM, Q_W), p_idx),
            pl.BlockSpec((MG_TM, Q_W), s_idx),
            pl.BlockSpec((MG_TM, D_MODEL), lambda i: (i, Z_GA // D_MODEL)),
            pl.BlockSpec((MG_TM, D_MODEL), lambda i: (i, Z_GB // D_MODEL)),
            pl.BlockSpec((GMLP_W, D_MODEL), const, **resident),
            pl.BlockSpec((Q_W, D_MODEL), const, **resident),
            pl.BlockSpec((D_MODEL, D_MODEL), const, **resident),
        ],
        out_specs=pl.BlockSpec((MG_TM, D_MODEL), lambda i: (i, 0)),
        out_shape=jax.ShapeDtypeStruct((N_TOK, D_MODEL), F32),
        compiler_params=pltpu.CompilerParams(
            dimension_semantics=("arbitrary",), vmem_limit_bytes=48 * MIB),
        name="merge",
    )(xp, xs, a_p, a_s, o_p, o_s, z, z, wpa, wpb, wo)


FF_TM = 1024
FF_TF = 512
FF_NJ = D_FF // FF_TF


def _ffn_kernel(x1_ref, n2_ref, wg_ref, wu_ref, wd_ref, fg_ref, y_ref, hh_ref, act_ref):
    j = pl.program_id(1)

    def gate_up(slot):
        hh = hh_ref[...]
        act = jax.nn.silu(_dot(hh, wg_ref[...])) * _dot(hh, wu_ref[...])
        act_ref[slot] = act.astype(BF16)

    def down(slot, first, last):
        acc = (x1_ref if first else y_ref)[...] + _dot(act_ref[slot], wd_ref[...])
        y_ref[...] = _rms(acc, fg_ref[...]) if last else acc

    @pl.when(j == 0)
    def _():
        hh_ref[...] = _rms(x1_ref[...], n2_ref[...]).astype(BF16)
        gate_up(0)

    @pl.when(j == 1)
    def _():
        gate_up(1)
        down(0, True, False)

    @pl.when((j > 1) & (j < FF_NJ))
    def _():
        gate_up(j % 2)
        down(1 - j % 2, False, False)

    @pl.when(j == FF_NJ)
    def _():
        down((FF_NJ - 1) % 2, False, True)


def _ffn(x1, row0, nrows, n2, wg, wu, wd, fg, name):
    blk0 = row0 // FF_TM
    return pl.pallas_call(
        _ffn_kernel,
        grid=(nrows // FF_TM, FF_NJ + 1),
        in_specs=[
            pl.BlockSpec((FF_TM, D_MODEL), lambda i, j: (blk0 + i, 0)),
            pl.BlockSpec((1, D_MODEL), lambda i, j: (0, 0)),
            pl.BlockSpec((D_MODEL, FF_TF), lambda i, j: (0, jnp.minimum(j, FF_NJ - 1))),
            pl.BlockSpec((D_MODEL, FF_TF), lambda i, j: (0, jnp.minimum(j, FF_NJ - 1))),
            pl.BlockSpec((FF_TF, D_MODEL), lambda i, j: (jnp.maximum(j - 1, 0), 0)),
            pl.BlockSpec((1, D_MODEL), lambda i, j: (0, 0)),
        ],
        out_specs=pl.BlockSpec((FF_TM, D_MODEL), lambda i, j: (i, 0)),
        out_shape=jax.ShapeDtypeStruct((nrows, D_MODEL), F32),
        scratch_shapes=[pltpu.VMEM((FF_TM, D_MODEL), BF16), pltpu.VMEM((2, FF_TM, FF_TF), BF16)],
        compiler_params=pltpu.CompilerParams(
            dimension_semantics=("arbitrary", "arbitrary"), vmem_limit_bytes=58 * MIB),
        name=name,
    )(x1, n2, wg, wu, wd, fg)


def kernel(x_prompt, x_sample, cache_k, cache_v, norm1_g, w_in, gmlp_norm_g, w_s, b_s, sinks,
           w_pa, w_pb, w_o, norm2_g, w_ff_gate, w_ff_up, w_ff_down, final_g):
    xp = x_prompt.reshape(N_PROMPT, D_MODEL)
    xs = x_sample.reshape(N_SAMPLE, D_MODEL)

    z, kv = _inproj(xp, xs, norm1_g.reshape(1, D_MODEL), w_in.reshape(D_MODEL, IN_W))

    sinks1 = sinks.reshape(N_Q_HEADS)
    o_p, pk, pv, wg, wu, wd = _attn_prompt(
        sinks1, z, kv, (w_ff_gate.reshape(D_MODEL, D_FF), w_ff_up.reshape(D_MODEL, D_FF),
                        w_ff_down.reshape(D_FF, D_MODEL)))
    o_s, k_win, v_win = _attn_sample(
        sinks1, z, cache_k.reshape(DEC_BATCH, WINDOW, KV_W),
        cache_v.reshape(DEC_BATCH, WINDOW, KV_W), kv)

    lng = gmlp_norm_g.reshape(1, GMLP_W)
    ws = w_s.reshape(GROUPS, CHUNK, CHUNK)
    bias = b_s.reshape(GROUPS, CHUNK).T
    a_p, wpa, wpb, wo = _gmlp(
        z, lng, ws, bias, sample=False,
        cast_weights=(w_pa.reshape(GMLP_W, D_MODEL), w_pb.reshape(Q_W, D_MODEL),
                      w_o.reshape(D_MODEL, D_MODEL)))
    a_s, gvn_s = _gmlp(z, lng, ws, bias, sample=True)

    x1 = _merge(xp, xs, a_p, a_s, o_p, o_s, z, wpa, wpb, wo)

    ffn_w = (norm2_g.reshape(1, D_MODEL), wg, wu, wd, final_g.reshape(1, D_MODEL))
    y_p = _ffn(x1, 0, N_PROMPT, *ffn_w, name="ffn_prompt")
    y_s = _ffn(x1, N_PROMPT, N_SAMPLE, *ffn_w, name="ffn_sample")

    return (y_p.reshape(BATCH, SEQ, D_MODEL),
            y_s.reshape(DEC_BATCH, DEC_SEQ, D_MODEL),
            pk.reshape(1, BATCH, WINDOW, N_KV_HEADS, HEAD_DIM),
            pv.reshape(1, BATCH, WINDOW, N_KV_HEADS, HEAD_DIM),
            k_win.reshape(1, DEC_BATCH, WINDOW, N_KV_HEADS, HEAD_DIM),
            v_win.reshape(1, DEC_BATCH, WINDOW, N_KV_HEADS, HEAD_DIM),
            gvn_s.reshape(1, DEC_BATCH, DEC_SEQ, GROUPS, GROUP_DIM))
```

```python
import functools

import jax
import jax.numpy as jnp
from jax import lax
from jax.experimental import pallas as pl
from jax.experimental.pallas import tpu as pltpu

F32 = jnp.float32
BF16 = jnp.bfloat16

D_MODEL = 2048
BATCH = 4
SEQ = 2048
DEC_BATCH = 128
DEC_SEQ = 8
HEAD_DIM = 64
N_Q_HEADS = 16
N_KV_HEADS = 4
Q_PER_KV = N_Q_HEADS // N_KV_HEADS
WINDOW = 128
CHUNK = 128
GROUPS = 8
GROUP_DIM = 128
Q_W = N_Q_HEADS * HEAD_DIM
KV_W = N_KV_HEADS * HEAD_DIM
GMLP_W = GROUPS * GROUP_DIM
D_FF = 5632
IN_W = Q_W + 2 * KV_W + 2 * GMLP_W + 2 * D_MODEL
EPS = 1e-6
NEG = -1e30

N_PROMPT = BATCH * SEQ
N_SAMPLE = DEC_BATCH * DEC_SEQ
N_TOK = N_PROMPT + N_SAMPLE

Z_W = 2 * D_MODEL + Q_W + 2 * GMLP_W
Z_GA, Z_GB, Z_Q, Z_U, Z_GV = 0, 2048, 4096, 5120, 6144

MIB = 1024 * 1024


def _rms(x, g):
    return x * lax.rsqrt(jnp.mean(x * x, axis=-1, keepdims=True) + EPS) * g


def _dot(a, b):
    return jnp.dot(a, b, preferred_element_type=F32)


def _dot_t(a, b):
    return lax.dot_general(a, b, (((1,), (1,)), ((), ())), preferred_element_type=F32)


IP_TM = 1024
IP_TN = 512
IP_CH = 256
IP_NPT = N_PROMPT // IP_TM
IP_NT = N_TOK // IP_TM
IP_NJ = IN_W // IP_TN


def _inproj_zcol(j):
    return jnp.where(j < 2, 8 + j, jnp.where(j == 2, 9, jnp.where(j < 7, j + 7, j - 7)))


def _inproj_kernel(xp_ref, xs_ref, g_ref, w_ref, z_ref, kv_ref, h_ref):
    i = pl.program_id(0)
    j = pl.program_id(1)

    @pl.when((j == 0) & (i < IP_NPT))
    def _():
        h_ref[...] = _rms(xp_ref[...], g_ref[...]).astype(BF16)

    @pl.when((j == 0) & (i >= IP_NPT))
    def _():
        h_ref[...] = _rms(xs_ref[...], g_ref[...]).astype(BF16)

    def proj(out_ref, act):
        w = w_ref[...].astype(BF16)
        for c in range(IP_TM // IP_CH):
            rows = slice(c * IP_CH, (c + 1) * IP_CH)
            out_ref[rows, :] = act(_dot(h_ref[rows, :], w)).astype(out_ref.dtype)

    @pl.when(j < 2)
    def _():
        proj(z_ref, lambda v: v)

    @pl.when(j == 2)
    def _():
        proj(kv_ref, lambda v: v)

    @pl.when((j >= 3) & (j < 7))
    def _():
        proj(z_ref, jax.nn.gelu)

    @pl.when(j >= 7)
    def _():
        proj(z_ref, jax.nn.sigmoid)


def _inproj(xp, xs, g, w):
    return pl.pallas_call(
        _inproj_kernel,
        grid=(IP_NT, IP_NJ),
        in_specs=[
            pl.BlockSpec((IP_TM, D_MODEL), lambda i, j: (jnp.minimum(i, IP_NPT - 1), 0)),
            pl.BlockSpec((IP_TM, D_MODEL), lambda i, j: (jnp.maximum(i - IP_NPT, 0), 0),
                         pipeline_mode=pl.Buffered(1)),
            pl.BlockSpec((1, D_MODEL), lambda i, j: (0, 0)),
            pl.BlockSpec((D_MODEL, IP_TN), lambda i, j: (0, j)),
        ],
        out_specs=[
            pl.BlockSpec((IP_TM, IP_TN), lambda i, j: (i, _inproj_zcol(j))),
            pl.BlockSpec((IP_TM, 2 * KV_W), lambda i, j: (i, 0)),
        ],
        out_shape=[
            jax.ShapeDtypeStruct((N_TOK, Z_W), BF16),
            jax.ShapeDtypeStruct((N_TOK, 2 * KV_W), F32),
        ],
        scratch_shapes=[pltpu.VMEM((IP_TM, D_MODEL), BF16)],
        compiler_params=pltpu.CompilerParams(
            dimension_semantics=("arbitrary", "arbitrary"), vmem_limit_bytes=52 * MIB),
        name="inproj",
    )(xp, xs, g, w)


NBLK = SEQ // WINDOW


PA_QB = 4
PA_ROWS = PA_QB * WINDOW
PA_STEPS_PER_SEQ = NBLK // PA_QB
GRP_W = Q_PER_KV * HEAD_DIM


def _lane_group_rotations(x):
    x1 = pltpu.roll(x, HEAD_DIM, 1)
    return [x.astype(BF16), x1.astype(BF16),
            pltpu.roll(x, 2 * HEAD_DIM, 1).astype(BF16), pltpu.roll(x1, 2 * HEAD_DIM, 1).astype(BF16)]


def _cast_blocks(src_refs, dst_refs):
    for src, dst in zip(src_refs, dst_refs, strict=True):
        dst[...] = src[...].astype(BF16)


def _cast_specs(weights, nsteps):
    specs = [pl.BlockSpec((w.shape[0] // nsteps, w.shape[1]), lambda t: (t, 0)) for w in weights]
    shapes = [jax.ShapeDtypeStruct(w.shape, BF16) for w in weights]
    return specs, shapes


def _attn_prompt_kernel(sink_ref, q_ref, kc_ref, kp_ref, *refs):
    ncast = (len(refs) - 3) // 2
    o_ref, pk_ref, pv_ref = refs[ncast:ncast + 3]
    _cast_blocks(refs[:ncast], refs[ncast + 3:])
    t = pl.program_id(0)
    kv = jnp.concatenate([kp_ref[...], kc_ref[...]], axis=0)
    k_rot = _lane_group_rotations(kv[:, :KV_W])
    v_rot = _lane_group_rotations(kv[:, KV_W:])
    lane_grp = lax.broadcasted_iota(jnp.int32, (WINDOW, GRP_W), 1) >> 6
    in_grp = [lane_grp == g for g in range(Q_PER_KV)]
    nrow = N_KV_HEADS * WINDOW
    key = lax.broadcasted_iota(jnp.int32, (2 * WINDOW, nrow), 0)
    qry = lax.broadcasted_iota(jnp.int32, (2 * WINDOW, nrow), 1) & (WINDOW - 1)
    band = (key >= qry) & (key <= qry + WINDOW)
    bias_inner = jnp.where(band, 0.0, NEG)
    bias_first = jnp.where(band & (key >= WINDOW), 0.0, NEG)

    for qb in range(PA_QB):
        is_first = ((t * PA_QB + qb) % NBLK) == 0
        bias = jnp.where(is_first, bias_first, bias_inner)
        q = q_ref[qb * WINDOW:(qb + 1) * WINDOW, :] * (HEAD_DIM ** -0.5)
        keys = slice(qb * WINDOW, (qb + 2) * WINDOW)
        outs = [[None] * Q_PER_KV for _ in range(N_KV_HEADS)]
        for rot in range(Q_PER_KV):
            lhs = jnp.concatenate(
                [jnp.where(in_grp[(h + rot) % Q_PER_KV], q[:, h * GRP_W:(h + 1) * GRP_W], 0)
                 for h in range(N_KV_HEADS)], axis=0)
            sink = jnp.concatenate(
                [jnp.full((1, WINDOW), sink_ref[h * Q_PER_KV + (h + rot) % Q_PER_KV], F32)
                 for h in range(N_KV_HEADS)], axis=1)
            s = _dot_t(k_rot[rot][keys], lhs) + bias
            m = jnp.maximum(jnp.max(s, axis=0, keepdims=True), sink)
            p = jnp.exp(s - m)
            denom = jnp.sum(p, axis=0, keepdims=True) + jnp.exp(sink - m)
            probs = (p / denom).astype(BF16)
            o = lax.dot_general(probs, v_rot[rot][keys], (((0,), (0,)), ((), ())),
                                preferred_element_type=F32)
            for h in range(N_KV_HEADS):
                outs[h][(h + rot) % Q_PER_KV] = o[h * WINDOW:(h + 1) * WINDOW]
        for h in range(N_KV_HEADS):
            oh = outs[h][Q_PER_KV - 1]
            for g in range(Q_PER_KV - 2, -1, -1):
                oh = jnp.where(in_grp[g], outs[h][g], oh)
            o_ref[qb * WINDOW:(qb + 1) * WINDOW, h * GRP_W:(h + 1) * GRP_W] = oh.astype(BF16)

    @pl.when(t % PA_STEPS_PER_SEQ == PA_STEPS_PER_SEQ - 1)
    def _():
        pk_ref[0] = kc_ref[PA_ROWS - WINDOW:PA_ROWS, :KV_W]
        pv_ref[0] = kc_ref[PA_ROWS - WINDOW:PA_ROWS, KV_W:]


def _attn_prompt(sinks, z, kv, cast_weights):
    nsteps = N_PROMPT // PA_ROWS
    cast_specs, cast_shapes = _cast_specs(cast_weights, nsteps)
    return pl.pallas_call(
        _attn_prompt_kernel,
        grid=(nsteps,),
        in_specs=[
            pl.BlockSpec(memory_space=pltpu.SMEM),
            pl.BlockSpec((PA_ROWS, Q_W), lambda t: (t, Z_Q // Q_W)),
            pl.BlockSpec((PA_ROWS, 2 * KV_W), lambda t: (t, 0)),
            pl.BlockSpec((WINDOW, 2 * KV_W), lambda t: (jnp.maximum(t * PA_QB - 1, 0), 0)),
        ] + cast_specs,
        out_specs=[
            pl.BlockSpec((PA_ROWS, Q_W), lambda t: (t, 0)),
            pl.BlockSpec((1, WINDOW, KV_W), lambda t: (t // PA_STEPS_PER_SEQ, 0, 0)),
            pl.BlockSpec((1, WINDOW, KV_W), lambda t: (t // PA_STEPS_PER_SEQ, 0, 0)),
        ] + cast_specs,
        out_shape=[
            jax.ShapeDtypeStruct((N_PROMPT, Q_W), BF16),
            jax.ShapeDtypeStruct((BATCH, WINDOW, KV_W), F32),
            jax.ShapeDtypeStruct((BATCH, WINDOW, KV_W), F32),
        ] + cast_shapes,
        compiler_params=pltpu.CompilerParams(
            dimension_semantics=("arbitrary",), vmem_limit_bytes=48 * MIB),
        name="attn_prompt",
    )(sinks, z, kv, kv, *cast_weights)


SA_BT = 16
SA_ROWS = N_Q_HEADS * DEC_SEQ
SA_STEPS = DEC_BATCH // SA_BT


def _attn_sample_kernel(sink_ref, q_ref, ck_ref, cv_ref, kvn_ref, *refs):
    ncast = (len(refs) - 3) // 2
    o_ref, kw_ref, vw_ref = refs[ncast:ncast + 3]
    _cast_blocks(refs[:ncast], refs[ncast + 3:])
    lane_grp = lax.broadcasted_iota(jnp.int32, (SA_BT * DEC_SEQ, GRP_W), 1) >> 6
    in_grp = [lane_grp == g for g in range(Q_PER_KV)]
    qf = q_ref[...].astype(F32) * (HEAD_DIM ** -0.5)
    q_exp = []
    for h in range(N_KV_HEADS):
        q_h = qf[:, h * GRP_W:(h + 1) * GRP_W]
        for r in range(Q_PER_KV):
            shift = ((h - r) % Q_PER_KV) * HEAD_DIM
            moved = q_h if shift == 0 else pltpu.roll(q_h, shift, 1)
            q_exp.append(jnp.where(in_grp[h], moved, 0.0))
    lhs = jnp.concatenate([q_exp[hr][b * DEC_SEQ:(b + 1) * DEC_SEQ]
                           for b in range(SA_BT) for hr in range(N_Q_HEADS)], axis=0).astype(BF16)

    kvn = kvn_ref[...]
    k_new = kvn[:, :KV_W].astype(BF16)
    v_new = kvn[:, KV_W:].astype(BF16)
    seq_rows = lambda b: slice(b * SA_ROWS, (b + 1) * SA_ROWS)
    s_c = jnp.concatenate([_dot_t(lhs[seq_rows(b)], ck_ref[b].astype(BF16)) for b in range(SA_BT)],
                          axis=0)
    s_n = _dot_t(lhs, k_new)

    nrow = SA_BT * SA_ROWS
    row = lax.broadcasted_iota(jnp.int32, (nrow, WINDOW), 0)
    col = lax.broadcasted_iota(jnp.int32, (nrow, WINDOW), 1)
    tq = row & (DEC_SEQ - 1)
    s_c = jnp.where(col >= tq, s_c, NEG)
    s_n = jnp.where(((col >> 3) == (row >> 7)) & ((col & (DEC_SEQ - 1)) <= tq), s_n, NEG)
    sink_seq = jnp.concatenate([jnp.full((DEC_SEQ, 1), sink_ref[hr], F32) for hr in range(N_Q_HEADS)],
                               axis=0)
    sink = jnp.concatenate([sink_seq] * SA_BT, axis=0)
    m = jnp.maximum(jnp.maximum(jnp.max(s_c, axis=-1, keepdims=True),
                                jnp.max(s_n, axis=-1, keepdims=True)), sink)
    p_c = jnp.exp(s_c - m)
    p_n = jnp.exp(s_n - m)
    denom = (jnp.sum(p_c, axis=-1, keepdims=True) + jnp.sum(p_n, axis=-1, keepdims=True)
             + jnp.exp(sink - m))
    p_c = (p_c / denom).astype(BF16)
    p_n = (p_n / denom).astype(BF16)
    o = jnp.concatenate([_dot(p_c[seq_rows(b)], cv_ref[b].astype(BF16)) for b in range(SA_BT)],
                        axis=0) + _dot(p_n, v_new)

    for h in range(N_KV_HEADS):
        parts = []
        for r in range(Q_PER_KV):
            hr = h * Q_PER_KV + r
            o_hr = jnp.concatenate(
                [o[b * SA_ROWS + hr * DEC_SEQ:b * SA_ROWS + (hr + 1) * DEC_SEQ] for b in range(SA_BT)],
                axis=0)
            shift = ((r - h) % Q_PER_KV) * HEAD_DIM
            parts.append(o_hr if shift == 0 else pltpu.roll(o_hr, shift, 1))
        o_h = parts[Q_PER_KV - 1]
        for g in range(Q_PER_KV - 2, -1, -1):
            o_h = jnp.where(in_grp[g], parts[g], o_h)
        o_ref[:, h * GRP_W:(h + 1) * GRP_W] = o_h.astype(BF16)

    for b in range(SA_BT):
        kw_ref[b, 0:WINDOW - DEC_SEQ, :] = ck_ref[b, DEC_SEQ:WINDOW, :]
        kw_ref[b, WINDOW - DEC_SEQ:WINDOW, :] = kvn[b * DEC_SEQ:(b + 1) * DEC_SEQ, :KV_W]
        vw_ref[b, 0:WINDOW - DEC_SEQ, :] = cv_ref[b, DEC_SEQ:WINDOW, :]
        vw_ref[b, WINDOW - DEC_SEQ:WINDOW, :] = kvn[b * DEC_SEQ:(b + 1) * DEC_SEQ, KV_W:]


def _attn_sample(sinks, z, ck, cv, kv, cast_weights):
    step_tok = SA_BT * DEC_SEQ
    blk0 = N_PROMPT // step_tok
    cast_specs, cast_shapes = _cast_specs(cast_weights, SA_STEPS)
    return pl.pallas_call(
        _attn_sample_kernel,
        grid=(SA_STEPS,),
        in_specs=[
            pl.BlockSpec(memory_space=pltpu.SMEM),
            pl.BlockSpec((step_tok, Q_W), lambda s: (blk0 + s, Z_Q // Q_W)),
            pl.BlockSpec((SA_BT, WINDOW, KV_W), lambda s: (s, 0, 0)),
            pl.BlockSpec((SA_BT, WINDOW, KV_W), lambda s: (s, 0, 0)),
            pl.BlockSpec((step_tok, 2 * KV_W), lambda s: (blk0 + s, 0)),
        ] + cast_specs,
        out_specs=[
            pl.BlockSpec((step_tok, Q_W), lambda s: (s, 0)),
            pl.BlockSpec((SA_BT, WINDOW, KV_W), lambda s: (s, 0, 0)),
            pl.BlockSpec((SA_BT, WINDOW, KV_W), lambda s: (s, 0, 0)),
        ] + cast_specs,
        out_shape=[
            jax.ShapeDtypeStruct((N_SAMPLE, Q_W), BF16),
            jax.ShapeDtypeStruct((DEC_BATCH, WINDOW, KV_W), F32),
            jax.ShapeDtypeStruct((DEC_BATCH, WINDOW, KV_W), F32),
        ] + cast_shapes,
        compiler_params=pltpu.CompilerParams(
            dimension_semantics=("arbitrary",), vmem_limit_bytes=48 * MIB),
        name="attn_sample",
    )(sinks, z, ck, cv, kv, *cast_weights)


def _gmlp_kernel(u_ref, gv_ref, lng_ref, w_ref, bias_ref, *refs, nchunk, sample, ncast):
    nout = 2 if sample else 1
    a_ref, *gvn_refs = refs[ncast:ncast + nout]
    _cast_blocks(refs[:ncast], refs[ncast + nout:])
    ri = lax.broadcasted_iota(jnp.int32, (CHUNK, CHUNK), 0)
    ci = lax.broadcasted_iota(jnp.int32, (CHUNK, CHUNK), 1)
    nseq = CHUNK // DEC_SEQ
    if sample:
        wmask = ((ri >> 3) == (ci >> 3)) & ((ri & (DEC_SEQ - 1)) >= (ci & (DEC_SEQ - 1)))
        corner_lanes = lax.broadcasted_iota(jnp.int32, (DEC_SEQ, CHUNK), 1) < DEC_SEQ

        def group_w(g):
            w8 = jnp.where(corner_lanes, w_ref[g, 0:DEC_SEQ, :], 0.0)
            shift = DEC_SEQ
            while shift < CHUNK:
                w8 = w8 + pltpu.roll(w8, shift, 1)
                shift *= 2
            return jnp.concatenate([w8] * nseq, axis=0)

        bias = jnp.concatenate([bias_ref[0:DEC_SEQ, :]] * nseq, axis=0)
    else:
        wmask = ri >= ci
        group_w = lambda g: w_ref[g]
        bias = bias_ref[...]
    w = [jnp.where(wmask, group_w(g), 0.0).astype(BF16) for g in range(GROUPS)]
    lng = lng_ref[...]
    for c in range(nchunk):
        rows = slice(c * CHUNK, (c + 1) * CHUNK)
        gv = gv_ref[rows, :].astype(F32)
        xc = gv - jnp.mean(gv, axis=-1, keepdims=True)
        y = xc * lax.rsqrt(jnp.mean(xc * xc, axis=-1, keepdims=True) + EPS) * lng
        if sample:
            gvn_refs[0][rows, :] = y
        yb = y.astype(BF16)
        for g in range(GROUPS):
            cols = slice(g * GROUP_DIM, (g + 1) * GROUP_DIM)
            mixed = _dot(w[g], yb[:, cols]) + bias[:, g:g + 1]
            a_ref[rows, cols] = (u_ref[rows, cols].astype(F32) * mixed).astype(BF16)


def _gmlp(z, lng, w, bias, *, sample, cast_weights=()):
    rows = CHUNK if sample else 8 * CHUNK
    nsteps = (N_SAMPLE if sample else N_PROMPT) // rows
    blk0 = (N_PROMPT // rows) if sample else 0
    n_out = N_SAMPLE if sample else N_PROMPT
    out_specs = [pl.BlockSpec((rows, GMLP_W), lambda s: (s, 0))]
    out_shape = [jax.ShapeDtypeStruct((n_out, GMLP_W), BF16)]
    if sample:
        out_specs.append(pl.BlockSpec((rows, GMLP_W), lambda s: (s, 0)))
        out_shape.append(jax.ShapeDtypeStruct((n_out, GMLP_W), F32))
    cast_specs, cast_shapes = _cast_specs(cast_weights, nsteps)
    return pl.pallas_call(
        functools.partial(_gmlp_kernel, nchunk=rows // CHUNK, sample=sample, ncast=len(cast_weights)),
        grid=(nsteps,),
        in_specs=[
            pl.BlockSpec((rows, GMLP_W), lambda s: (blk0 + s, Z_U // GMLP_W)),
            pl.BlockSpec((rows, GMLP_W), lambda s: (blk0 + s, Z_GV // GMLP_W)),
            pl.BlockSpec((1, GMLP_W), lambda s: (0, 0)),
            pl.BlockSpec((GROUPS, CHUNK, CHUNK), lambda s: (0, 0, 0)),
            pl.BlockSpec((CHUNK, GROUPS), lambda s: (0, 0)),
        ] + cast_specs,
        out_specs=out_specs + cast_specs,
        out_shape=out_shape + cast_shapes,
        compiler_params=pltpu.CompilerParams(
            dimension_semantics=("arbitrary",), vmem_limit_bytes=40 * MIB),
        name="gmlp_sample" if sample else "gmlp_prompt",
    )(z, z, lng, w, bias, *cast_weights)


MG_TM = 256
MG_NPT = N_PROMPT // MG_TM
MG_NT = N_TOK // MG_TM


def _merge_kernel(xp_ref, xs_ref, ap_ref, as_ref, op_ref, os_ref, ga_ref, gb_ref,
                  wpa_ref, wpb_ref, wo_ref, x1_ref):
    i = pl.program_id(0)

    def body(x_ref, a_ref, o_ref):
        pa = _dot(a_ref[...], wpa_ref[...])
        pb = _dot(o_ref[...], wpb_ref[...])
        mix = ga_ref[...].astype(F32) * pa + gb_ref[...].astype(F32) * pb
        x1_ref[...] = x_ref[...] + _dot(mix.astype(BF16), wo_ref[...])

    @pl.when(i < MG_NPT)
    def _():
        body(xp_ref, ap_ref, op_ref)

    @pl.when(i >= MG_NPT)
    def _():
        body(xs_ref, as_ref, os_ref)


def _merge(xp, xs, a_p, a_s, o_p, o_s, z, wpa, wpb, wo):
    p_idx = lambda i: (jnp.minimum(i, MG_NPT - 1), 0)
    s_idx = lambda i: (jnp.maximum(i - MG_NPT, 0), 0)
    const = lambda i: (0, 0)
    resident = dict(pipeline_mode=pl.Buffered(1))
    return pl.pallas_call(
        _merge_kernel,
        grid=(MG_NT,),
        in_specs=[
            pl.BlockSpec((MG_TM, D_MODEL), p_idx),
            pl.BlockSpec((MG_TM, D_MODEL), s_idx),
            pl.BlockSpec((MG_TM, GMLP_W), p_idx),
            pl.BlockSpec((MG_TM, GMLP_W), s_idx),
            pl.BlockSpec((MG_TM, Q_W), p_idx),
            pl.BlockSpec((MG_TM, Q_W), s_idx),
            pl.BlockSpec((MG_TM, D_MODEL), lambda i: (i, Z_GA // D_MODEL)),
            pl.BlockSpec((MG_TM, D_MODEL), lambda i: (i, Z_GB // D_MODEL)),
            pl.BlockSpec((GMLP_W, D_MODEL), const, **resident),
            pl.BlockSpec((Q_W, D_MODEL), const, **resident),
            pl.BlockSpec((D_MODEL, D_MODEL), const, **resident),
        ],
        out_specs=pl.BlockSpec((MG_TM, D_MODEL), lambda i: (i, 0)),
        out_shape=jax.ShapeDtypeStruct((N_TOK, D_MODEL), F32),
        compiler_params=pltpu.CompilerParams(
            dimension_semantics=("arbitrary",), vmem_limit_bytes=48 * MIB),
        name="merge",
    )(xp, xs, a_p, a_s, o_p, o_s, z, z, wpa, wpb, wo)


FF_TM = 1024
FF_TF = 512
FF_NJ = D_FF // FF_TF


def _ffn_kernel(x1_ref, n2_ref, wg_ref, wu_ref, wd_ref, fg_ref, y_ref, hh_ref, act_ref):
    j = pl.program_id(1)

    def gate_up(slot):
        hh = hh_ref[...]
        act = jax.nn.silu(_dot(hh, wg_ref[...])) * _dot(hh, wu_ref[...])
        act_ref[slot] = act.astype(BF16)

    def down(slot, first, last):
        acc = (x1_ref if first else y_ref)[...] + _dot(act_ref[slot], wd_ref[...])
        y_ref[...] = _rms(acc, fg_ref[...]) if last else acc

    @pl.when(j == 0)
    def _():
        hh_ref[...] = _rms(x1_ref[...], n2_ref[...]).astype(BF16)
        gate_up(0)

    @pl.when(j == 1)
    def _():
        gate_up(1)
        down(0, True, False)

    @pl.when((j > 1) & (j < FF_NJ))
    def _():
        gate_up(j % 2)
        down(1 - j % 2, False, False)

    @pl.when(j == FF_NJ)
    def _():
        down((FF_NJ - 1) % 2, False, True)


def _ffn(x1, row0, nrows, n2, wg, wu, wd, fg, name):
    blk0 = row0 // FF_TM
    return pl.pallas_call(
        _ffn_kernel,
        grid=(nrows // FF_TM, FF_NJ + 1),
        in_specs=[
            pl.BlockSpec((FF_TM, D_MODEL), lambda i, j: (blk0 + i, 0)),
            pl.BlockSpec((1, D_MODEL), lambda i, j: (0, 0)),
            pl.BlockSpec((D_MODEL, FF_TF), lambda i, j: (0, jnp.minimum(j, FF_NJ - 1))),
            pl.BlockSpec((D_MODEL, FF_TF), lambda i, j: (0, jnp.minimum(j, FF_NJ - 1))),
            pl.BlockSpec((FF_TF, D_MODEL), lambda i, j: (jnp.maximum(j - 1, 0), 0)),
            pl.BlockSpec((1, D_MODEL), lambda i, j: (0, 0)),
        ],
        out_specs=pl.BlockSpec((FF_TM, D_MODEL), lambda i, j: (i, 0)),
        out_shape=jax.ShapeDtypeStruct((nrows, D_MODEL), F32),
        scratch_shapes=[pltpu.VMEM((FF_TM, D_MODEL), BF16), pltpu.VMEM((2, FF_TM, FF_TF), BF16)],
        compiler_params=pltpu.CompilerParams(
            dimension_semantics=("arbitrary", "arbitrary"), vmem_limit_bytes=58 * MIB),
        name=name,
    )(x1, n2, wg, wu, wd, fg)


def kernel(x_prompt, x_sample, cache_k, cache_v, norm1_g, w_in, gmlp_norm_g, w_s, b_s, sinks,
           w_pa, w_pb, w_o, norm2_g, w_ff_gate, w_ff_up, w_ff_down, final_g):
    xp = x_prompt.reshape(N_PROMPT, D_MODEL)
    xs = x_sample.reshape(N_SAMPLE, D_MODEL)

    z, kv = _inproj(xp, xs, norm1_g.reshape(1, D_MODEL), w_in.reshape(D_MODEL, IN_W))

    sinks1 = sinks.reshape(N_Q_HEADS)
    o_p, pk, pv, wg, wu = _attn_prompt(
        sinks1, z, kv, (w_ff_gate.reshape(D_MODEL, D_FF), w_ff_up.reshape(D_MODEL, D_FF)))
    o_s, k_win, v_win, wd, wpa, wpb = _attn_sample(
        sinks1, z, cache_k.reshape(DEC_BATCH, WINDOW, KV_W),
        cache_v.reshape(DEC_BATCH, WINDOW, KV_W), kv,
        (w_ff_down.reshape(D_FF, D_MODEL), w_pa.reshape(GMLP_W, D_MODEL), w_pb.reshape(Q_W, D_MODEL)))

    lng = gmlp_norm_g.reshape(1, GMLP_W)
    ws = w_s.reshape(GROUPS, CHUNK, CHUNK)
    bias = b_s.reshape(GROUPS, CHUNK).T
    a_p, wo = _gmlp(z, lng, ws, bias, sample=False, cast_weights=(w_o.reshape(D_MODEL, D_MODEL),))
    a_s, gvn_s = _gmlp(z, lng, ws, bias, sample=True)

    x1 = _merge(xp, xs, a_p, a_s, o_p, o_s, z, wpa, wpb, wo)

    ffn_w = (norm2_g.reshape(1, D_MODEL), wg, wu, wd, final_g.reshape(1, D_MODEL))
    y_p = _ffn(x1, 0, N_PROMPT, *ffn_w, name="ffn_prompt")
    y_s = _ffn(x1, N_PROMPT, N_SAMPLE, *ffn_w, name="ffn_sample")

    return (y_p.reshape(BATCH, SEQ, D_MODEL),
            y_s.reshape(DEC_BATCH, DEC_SEQ, D_MODEL),
            pk.reshape(1, BATCH, WINDOW, N_KV_HEADS, HEAD_DIM),
            pv.reshape(1, BATCH, WINDOW, N_KV_HEADS, HEAD_DIM),
            k_win.reshape(1, DEC_BATCH, WINDOW, N_KV_HEADS, HEAD_DIM),
            v_win.reshape(1, DEC_BATCH, WINDOW, N_KV_HEADS, HEAD_DIM),
            gvn_s.reshape(1, DEC_BATCH, DEC_SEQ, GROUPS, GROUP_DIM))
```

```python
import functools

import jax
import jax.numpy as jnp
from jax import lax
from jax.experimental import pallas as pl
from jax.experimental.pallas import tpu as pltpu

F32 = jnp.float32
BF16 = jnp.bfloat16

D_MODEL = 2048
BATCH = 4
SEQ = 2048
DEC_BATCH = 128
DEC_SEQ = 8
HEAD_DIM = 64
N_Q_HEADS = 16
N_KV_HEADS = 4
Q_PER_KV = N_Q_HEADS // N_KV_HEADS
WINDOW = 128
CHUNK = 128
GROUPS = 8
GROUP_DIM = 128
Q_W = N_Q_HEADS * HEAD_DIM
KV_W = N_KV_HEADS * HEAD_DIM
GMLP_W = GROUPS * GROUP_DIM
D_FF = 5632
IN_W = Q_W + 2 * KV_W + 2 * GMLP_W + 2 * D_MODEL
EPS = 1e-6
NEG = -1e30

N_PROMPT = BATCH * SEQ
N_SAMPLE = DEC_BATCH * DEC_SEQ
N_TOK = N_PROMPT + N_SAMPLE

Z_W = 2 * D_MODEL + Q_W + 2 * GMLP_W
Z_GA, Z_GB, Z_Q, Z_U, Z_GV = 0, 2048, 4096, 5120, 6144

MIB = 1024 * 1024


def _rms(x, g):
    return x * lax.rsqrt(jnp.mean(x * x, axis=-1, keepdims=True) + EPS) * g


def _dot(a, b):
    return jnp.dot(a, b, preferred_element_type=F32)


def _dot_t(a, b):
    return lax.dot_general(a, b, (((1,), (1,)), ((), ())), preferred_element_type=F32)


IP_TM = 1024
IP_TN = 512
IP_CH = 256
IP_NPT = N_PROMPT // IP_TM
IP_NT = N_TOK // IP_TM
IP_NJ = 8


def _inproj_wa(j):
    return jnp.where(j == 0, 0, jnp.where(j == 1, 2, 2 * j - 1))


def _inproj_wb(j):
    return jnp.where(j == 0, 1, jnp.where(j == 1, 4, 2 * j))


def _inproj_zcol(j):
    return jnp.where(j < 2, Z_Q // (2 * IP_TN), jnp.where(j < 4, j + 3, j - 4))


def _inproj_kernel(xp_ref, xs_ref, g_ref, wa_ref, wb_ref, z_ref, kv_ref, h_ref):
    i = pl.program_id(0)
    j = pl.program_id(1)
    nch = IP_TM // IP_CH

    def first_step(x_ref):
        wa = wa_ref[...].astype(BF16)
        wb = wb_ref[...].astype(BF16)
        for c in range(nch):
            rows = slice(c * IP_CH, (c + 1) * IP_CH)
            h = _rms(x_ref[rows, :], g_ref[...]).astype(BF16)
            h_ref[rows, :] = h
            z_ref[rows, 0:IP_TN] = _dot(h, wa).astype(BF16)
            z_ref[rows, IP_TN:2 * IP_TN] = _dot(h, wb).astype(BF16)

    def pair_step(act):
        wa = wa_ref[...].astype(BF16)
        wb = wb_ref[...].astype(BF16)
        for c in range(nch):
            rows = slice(c * IP_CH, (c + 1) * IP_CH)
            h = h_ref[rows, :]
            z_ref[rows, 0:IP_TN] = act(_dot(h, wa)).astype(BF16)
            z_ref[rows, IP_TN:2 * IP_TN] = act(_dot(h, wb)).astype(BF16)

    @pl.when((j == 0) & (i < IP_NPT))
    def _():
        first_step(xp_ref)

    @pl.when((j == 0) & (i >= IP_NPT))
    def _():
        first_step(xs_ref)

    @pl.when(j == 1)
    def _():
        wa = wa_ref[...].astype(BF16)
        for c in range(nch):
            rows = slice(c * IP_CH, (c + 1) * IP_CH)
            kv_ref[rows, :] = _dot(h_ref[rows, :], wa)

    @pl.when((j >= 2) & (j < 4))
    def _():
        pair_step(jax.nn.gelu)

    @pl.when(j >= 4)
    def _():
        pair_step(jax.nn.sigmoid)


def _inproj(xp, xs, g, w):
    return pl.pallas_call(
        _inproj_kernel,
        grid=(IP_NT, IP_NJ),
        in_specs=[
            pl.BlockSpec((IP_TM, D_MODEL), lambda i, j: (jnp.minimum(i, IP_NPT - 1), 0)),
            pl.BlockSpec((IP_TM, D_MODEL), lambda i, j: (jnp.maximum(i - IP_NPT, 0), 0),
                         pipeline_mode=pl.Buffered(1)),
            pl.BlockSpec((1, D_MODEL), lambda i, j: (0, 0)),
            pl.BlockSpec((D_MODEL, IP_TN), lambda i, j: (0, _inproj_wa(j))),
            pl.BlockSpec((D_MODEL, IP_TN), lambda i, j: (0, _inproj_wb(j))),
        ],
        out_specs=[
            pl.BlockSpec((IP_TM, 2 * IP_TN), lambda i, j: (i, _inproj_zcol(j))),
            pl.BlockSpec((IP_TM, 2 * KV_W), lambda i, j: (i, 0)),
        ],
        out_shape=[
            jax.ShapeDtypeStruct((N_TOK, Z_W), BF16),
            jax.ShapeDtypeStruct((N_TOK, 2 * KV_W), F32),
        ],
        scratch_shapes=[pltpu.VMEM((IP_TM, D_MODEL), BF16)],
        compiler_params=pltpu.CompilerParams(
            dimension_semantics=("arbitrary", "arbitrary"), vmem_limit_bytes=60 * MIB),
        name="inproj",
    )(xp, xs, g, w, w)


NBLK = SEQ // WINDOW


PA_QB = 4
PA_ROWS = PA_QB * WINDOW
PA_STEPS_PER_SEQ = NBLK // PA_QB
GRP_W = Q_PER_KV * HEAD_DIM


def _lane_group_rotations(x):
    x1 = pltpu.roll(x, HEAD_DIM, 1)
    return [x.astype(BF16), x1.astype(BF16),
            pltpu.roll(x, 2 * HEAD_DIM, 1).astype(BF16), pltpu.roll(x1, 2 * HEAD_DIM, 1).astype(BF16)]


def _cast_blocks(src_refs, dst_refs):
    for src, dst in zip(src_refs, dst_refs, strict=True):
        dst[...] = src[...].astype(BF16)


def _cast_specs(weights, nsteps):
    specs = [pl.BlockSpec((w.shape[0] // nsteps, w.shape[1]), lambda t: (t, 0)) for w in weights]
    shapes = [jax.ShapeDtypeStruct(w.shape, BF16) for w in weights]
    return specs, shapes


def _attn_prompt_kernel(sink_ref, q_ref, kc_ref, kp_ref, *refs):
    ncast = (len(refs) - 3) // 2
    o_ref, pk_ref, pv_ref = refs[ncast:ncast + 3]
    _cast_blocks(refs[:ncast], refs[ncast + 3:])
    t = pl.program_id(0)
    kv = jnp.concatenate([kp_ref[...], kc_ref[...]], axis=0)
    k_rot = _lane_group_rotations(kv[:, :KV_W])
    v_rot = _lane_group_rotations(kv[:, KV_W:])
    lane_grp = lax.broadcasted_iota(jnp.int32, (WINDOW, GRP_W), 1) >> 6
    in_grp = [lane_grp == g for g in range(Q_PER_KV)]
    nrow = N_KV_HEADS * WINDOW
    key = lax.broadcasted_iota(jnp.int32, (2 * WINDOW, nrow), 0)
    qry = lax.broadcasted_iota(jnp.int32, (2 * WINDOW, nrow), 1) & (WINDOW - 1)
    band = (key >= qry) & (key <= qry + WINDOW)
    bias_inner = jnp.where(band, 0.0, NEG)
    bias_first = jnp.where(band & (key >= WINDOW), 0.0, NEG)

    for qb in range(PA_QB):
        is_first = ((t * PA_QB + qb) % NBLK) == 0
        bias = jnp.where(is_first, bias_first, bias_inner)
        q = q_ref[qb * WINDOW:(qb + 1) * WINDOW, :] * (HEAD_DIM ** -0.5)
        keys = slice(qb * WINDOW, (qb + 2) * WINDOW)
        outs = [[None] * Q_PER_KV for _ in range(N_KV_HEADS)]
        for rot in range(Q_PER_KV):
            lhs = jnp.concatenate(
                [jnp.where(in_grp[(h + rot) % Q_PER_KV], q[:, h * GRP_W:(h + 1) * GRP_W], 0)
                 for h in range(N_KV_HEADS)], axis=0)
            sink = jnp.concatenate(
                [jnp.full((1, WINDOW), sink_ref[h * Q_PER_KV + (h + rot) % Q_PER_KV], F32)
                 for h in range(N_KV_HEADS)], axis=1)
            s = _dot_t(k_rot[rot][keys], lhs) + bias
            m = jnp.maximum(jnp.max(s, axis=0, keepdims=True), sink)
            p = jnp.exp(s - m)
            denom = jnp.sum(p, axis=0, keepdims=True) + jnp.exp(sink - m)
            probs = (p / denom).astype(BF16)
            o = lax.dot_general(probs, v_rot[rot][keys], (((0,), (0,)), ((), ())),
                                preferred_element_type=F32)
            for h in range(N_KV_HEADS):
                outs[h][(h + rot) % Q_PER_KV] = o[h * WINDOW:(h + 1) * WINDOW]
        for h in range(N_KV_HEADS):
            oh = outs[h][Q_PER_KV - 1]
            for g in range(Q_PER_KV - 2, -1, -1):
                oh = jnp.where(in_grp[g], outs[h][g], oh)
            o_ref[qb * WINDOW:(qb + 1) * WINDOW, h * GRP_W:(h + 1) * GRP_W] = oh.astype(BF16)

    @pl.when(t % PA_STEPS_PER_SEQ == PA_STEPS_PER_SEQ - 1)
    def _():
        pk_ref[0] = kc_ref[PA_ROWS - WINDOW:PA_ROWS, :KV_W]
        pv_ref[0] = kc_ref[PA_ROWS - WINDOW:PA_ROWS, KV_W:]


def _attn_prompt(sinks, z, kv, cast_weights):
    nsteps = N_PROMPT // PA_ROWS
    cast_specs, cast_shapes = _cast_specs(cast_weights, nsteps)
    return pl.pallas_call(
        _attn_prompt_kernel,
        grid=(nsteps,),
        in_specs=[
            pl.BlockSpec(memory_space=pltpu.SMEM),
            pl.BlockSpec((PA_ROWS, Q_W), lambda t: (t, Z_Q // Q_W)),
            pl.BlockSpec((PA_ROWS, 2 * KV_W), lambda t: (t, 0)),
            pl.BlockSpec((WINDOW, 2 * KV_W), lambda t: (jnp.maximum(t * PA_QB - 1, 0), 0)),
        ] + cast_specs,
        out_specs=[
            pl.BlockSpec((PA_ROWS, Q_W), lambda t: (t, 0)),
            pl.BlockSpec((1, WINDOW, KV_W), lambda t: (t // PA_STEPS_PER_SEQ, 0, 0)),
            pl.BlockSpec((1, WINDOW, KV_W), lambda t: (t // PA_STEPS_PER_SEQ, 0, 0)),
        ] + cast_specs,
        out_shape=[
            jax.ShapeDtypeStruct((N_PROMPT, Q_W), BF16),
            jax.ShapeDtypeStruct((BATCH, WINDOW, KV_W), F32),
            jax.ShapeDtypeStruct((BATCH, WINDOW, KV_W), F32),
        ] + cast_shapes,
        compiler_params=pltpu.CompilerParams(
            dimension_semantics=("arbitrary",), vmem_limit_bytes=48 * MIB),
        name="attn_prompt",
    )(sinks, z, kv, kv, *cast_weights)


SA_BT = 16
SA_ROWS = N_Q_HEADS * DEC_SEQ
SA_STEPS = DEC_BATCH // SA_BT


def _attn_sample_kernel(sink_ref, q_ref, ck_ref, cv_ref, kvn_ref, *refs):
    ncast = (len(refs) - 3) // 2
    o_ref, kw_ref, vw_ref = refs[ncast:ncast + 3]
    _cast_blocks(refs[:ncast], refs[ncast + 3:])
    lane_grp = lax.broadcasted_iota(jnp.int32, (SA_BT * DEC_SEQ, GRP_W), 1) >> 6
    in_grp = [lane_grp == g for g in range(Q_PER_KV)]
    qf = q_ref[...].astype(F32) * (HEAD_DIM ** -0.5)
    q_exp = []
    for h in range(N_KV_HEADS):
        q_h = qf[:, h * GRP_W:(h + 1) * GRP_W]
        for r in range(Q_PER_KV):
            shift = ((h - r) % Q_PER_KV) * HEAD_DIM
            moved = q_h if shift == 0 else pltpu.roll(q_h, shift, 1)
            q_exp.append(jnp.where(in_grp[h], moved, 0.0))
    lhs = jnp.concatenate([q_exp[hr][b * DEC_SEQ:(b + 1) * DEC_SEQ]
                           for b in range(SA_BT) for hr in range(N_Q_HEADS)], axis=0).astype(BF16)

    kvn = kvn_ref[...]
    k_new = kvn[:, :KV_W].astype(BF16)
    v_new = kvn[:, KV_W:].astype(BF16)
    seq_rows = lambda b: slice(b * SA_ROWS, (b + 1) * SA_ROWS)
    s_c = jnp.concatenate([_dot_t(lhs[seq_rows(b)], ck_ref[b].astype(BF16)) for b in range(SA_BT)],
                          axis=0)
    s_n = _dot_t(lhs, k_new)

    nrow = SA_BT * SA_ROWS
    row = lax.broadcasted_iota(jnp.int32, (nrow, WINDOW), 0)
    col = lax.broadcasted_iota(jnp.int32, (nrow, WINDOW), 1)
    tq = row & (DEC_SEQ - 1)
    s_c = jnp.where(col >= tq, s_c, NEG)
    s_n = jnp.where(((col >> 3) == (row >> 7)) & ((col & (DEC_SEQ - 1)) <= tq), s_n, NEG)
    sink_seq = jnp.concatenate([jnp.full((DEC_SEQ, 1), sink_ref[hr], F32) for hr in range(N_Q_HEADS)],
                               axis=0)
    sink = jnp.concatenate([sink_seq] * SA_BT, axis=0)
    m = jnp.maximum(jnp.maximum(jnp.max(s_c, axis=-1, keepdims=True),
                                jnp.max(s_n, axis=-1, keepdims=True)), sink)
    p_c = jnp.exp(s_c - m)
    p_n = jnp.exp(s_n - m)
    denom = (jnp.sum(p_c, axis=-1, keepdims=True) + jnp.sum(p_n, axis=-1, keepdims=True)
             + jnp.exp(sink - m))
    p_c = (p_c / denom).astype(BF16)
    p_n = (p_n / denom).astype(BF16)
    o = jnp.concatenate([_dot(p_c[seq_rows(b)], cv_ref[b].astype(BF16)) for b in range(SA_BT)],
                        axis=0) + _dot(p_n, v_new)

    for h in range(N_KV_HEADS):
        parts = []
        for r in range(Q_PER_KV):
            hr = h * Q_PER_KV + r
            o_hr = jnp.concatenate(
                [o[b * SA_ROWS + hr * DEC_SEQ:b * SA_ROWS + (hr + 1) * DEC_SEQ] for b in range(SA_BT)],
                axis=0)
            shift = ((r - h) % Q_PER_KV) * HEAD_DIM
            parts.append(o_hr if shift == 0 else pltpu.roll(o_hr, shift, 1))
        o_h = parts[Q_PER_KV - 1]
        for g in range(Q_PER_KV - 2, -1, -1):
            o_h = jnp.where(in_grp[g], parts[g], o_h)
        o_ref[:, h * GRP_W:(h + 1) * GRP_W] = o_h.astype(BF16)

    for b in range(SA_BT):
        kw_ref[b, 0:WINDOW - DEC_SEQ, :] = ck_ref[b, DEC_SEQ:WINDOW, :]
        kw_ref[b, WINDOW - DEC_SEQ:WINDOW, :] = kvn[b * DEC_SEQ:(b + 1) * DEC_SEQ, :KV_W]
        vw_ref[b, 0:WINDOW - DEC_SEQ, :] = cv_ref[b, DEC_SEQ:WINDOW, :]
        vw_ref[b, WINDOW - DEC_SEQ:WINDOW, :] = kvn[b * DEC_SEQ:(b + 1) * DEC_SEQ, KV_W:]


def _attn_sample(sinks, z, ck, cv, kv, cast_weights):
    step_tok = SA_BT * DEC_SEQ
    blk0 = N_PROMPT // step_tok
    cast_specs, cast_shapes = _cast_specs(cast_weights, SA_STEPS)
    return pl.pallas_call(
        _attn_sample_kernel,
        grid=(SA_STEPS,),
        in_specs=[
            pl.BlockSpec(memory_space=pltpu.SMEM),
            pl.BlockSpec((step_tok, Q_W), lambda s: (blk0 + s, Z_Q // Q_W)),
            pl.BlockSpec((SA_BT, WINDOW, KV_W), lambda s: (s, 0, 0)),
            pl.BlockSpec((SA_BT, WINDOW, KV_W), lambda s: (s, 0, 0)),
            pl.BlockSpec((step_tok, 2 * KV_W), lambda s: (blk0 + s, 0)),
        ] + cast_specs,
        out_specs=[
            pl.BlockSpec((step_tok, Q_W), lambda s: (s, 0)),
            pl.BlockSpec((SA_BT, WINDOW, KV_W), lambda s: (s, 0, 0)),
            pl.BlockSpec((SA_BT, WINDOW, KV_W), lambda s: (s, 0, 0)),
        ] + cast_specs,
        out_shape=[
            jax.ShapeDtypeStruct((N_SAMPLE, Q_W), BF16),
            jax.ShapeDtypeStruct((DEC_BATCH, WINDOW, KV_W), F32),
            jax.ShapeDtypeStruct((DEC_BATCH, WINDOW, KV_W), F32),
        ] + cast_shapes,
        compiler_params=pltpu.CompilerParams(
            dimension_semantics=("arbitrary",), vmem_limit_bytes=48 * MIB),
        name="attn_sample",
    )(sinks, z, ck, cv, kv, *cast_weights)


def _gmlp_kernel(u_ref, gv_ref, lng_ref, w_ref, bias_ref, *refs, nchunk, sample, ncast):
    nout = 2 if sample else 1
    a_ref, *gvn_refs = refs[ncast:ncast + nout]
    _cast_blocks(refs[:ncast], refs[ncast + nout:])
    ri = lax.broadcasted_iota(jnp.int32, (CHUNK, CHUNK), 0)
    ci = lax.broadcasted_iota(jnp.int32, (CHUNK, CHUNK), 1)
    nseq = CHUNK // DEC_SEQ
    if sample:
        wmask = ((ri >> 3) == (ci >> 3)) & ((ri & (DEC_SEQ - 1)) >= (ci & (DEC_SEQ - 1)))
        corner_lanes = lax.broadcasted_iota(jnp.int32, (DEC_SEQ, CHUNK), 1) < DEC_SEQ

        def group_w(g):
            w8 = jnp.where(corner_lanes, w_ref[g, 0:DEC_SEQ, :], 0.0)
            shift = DEC_SEQ
            while shift < CHUNK:
                w8 = w8 + pltpu.roll(w8, shift, 1)
                shift *= 2
            return jnp.concatenate([w8] * nseq, axis=0)

        bias = jnp.concatenate([bias_ref[0:DEC_SEQ, :]] * nseq, axis=0)
    else:
        wmask = ri >= ci
        group_w = lambda g: w_ref[g]
        bias = bias_ref[...]
    w = [jnp.where(wmask, group_w(g), 0.0).astype(BF16) for g in range(GROUPS)]
    lng = lng_ref[...]
    for c in range(nchunk):
        rows = slice(c * CHUNK, (c + 1) * CHUNK)
        gv = gv_ref[rows, :].astype(F32)
        xc = gv - jnp.mean(gv, axis=-1, keepdims=True)
        y = xc * lax.rsqrt(jnp.mean(xc * xc, axis=-1, keepdims=True) + EPS) * lng
        if sample:
            gvn_refs[0][rows, :] = y
        yb = y.astype(BF16)
        for g in range(GROUPS):
            cols = slice(g * GROUP_DIM, (g + 1) * GROUP_DIM)
            mixed = _dot(w[g], yb[:, cols]) + bias[:, g:g + 1]
            a_ref[rows, cols] = (u_ref[rows, cols].astype(F32) * mixed).astype(BF16)


def _gmlp(z, lng, w, bias, *, sample, cast_weights=()):
    rows = CHUNK if sample else 8 * CHUNK
    nsteps = (N_SAMPLE if sample else N_PROMPT) // rows
    blk0 = (N_PROMPT // rows) if sample else 0
    n_out = N_SAMPLE if sample else N_PROMPT
    out_specs = [pl.BlockSpec((rows, GMLP_W), lambda s: (s, 0))]
    out_shape = [jax.ShapeDtypeStruct((n_out, GMLP_W), BF16)]
    if sample:
        out_specs.append(pl.BlockSpec((rows, GMLP_W), lambda s: (s, 0)))
        out_shape.append(jax.ShapeDtypeStruct((n_out, GMLP_W), F32))
    cast_specs, cast_shapes = _cast_specs(cast_weights, nsteps)
    return pl.pallas_call(
        functools.partial(_gmlp_kernel, nchunk=rows // CHUNK, sample=sample, ncast=len(cast_weights)),
        grid=(nsteps,),
        in_specs=[
            pl.BlockSpec((rows, GMLP_W), lambda s: (blk0 + s, Z_U // GMLP_W)),
            pl.BlockSpec((rows, GMLP_W), lambda s: (blk0 + s, Z_GV // GMLP_W)),
            pl.BlockSpec((1, GMLP_W), lambda s: (0, 0)),
            pl.BlockSpec((GROUPS, CHUNK, CHUNK), lambda s: (0, 0, 0)),
            pl.BlockSpec((CHUNK, GROUPS), lambda s: (0, 0)),
        ] + cast_specs,
        out_specs=out_specs + cast_specs,
        out_shape=out_shape + cast_shapes,
        compiler_params=pltpu.CompilerParams(
            dimension_semantics=("arbitrary",), vmem_limit_bytes=40 * MIB),
        name="gmlp_sample" if sample else "gmlp_prompt",
    )(z, z, lng, w, bias, *cast_weights)


MG_TM = 256
MG_NPT = N_PROMPT // MG_TM
MG_NT = N_TOK // MG_TM


def _merge_kernel(xp_ref, xs_ref, ap_ref, as_ref, op_ref, os_ref, ga_ref, gb_ref,
                  wpa_ref, wpb_ref, wo_ref, x1_ref):
    i = pl.program_id(0)

    def body(x_ref, a_ref, o_ref):
        pa = _dot(a_ref[...], wpa_ref[...])
        pb = _dot(o_ref[...], wpb_ref[...])
        mix = ga_ref[...].astype(F32) * pa + gb_ref[...].astype(F32) * pb
        x1_ref[...] = x_ref[...] + _dot(mix.astype(BF16), wo_ref[...])

    @pl.when(i < MG_NPT)
    def _():
        body(xp_ref, ap_ref, op_ref)

    @pl.when(i >= MG_NPT)
    def _():
        body(xs_ref, as_ref, os_ref)


def _merge(xp, xs, a_p, a_s, o_p, o_s, z, wpa, wpb, wo):
    p_idx = lambda i: (jnp.minimum(i, MG_NPT - 1), 0)
    s_idx = lambda i: (jnp.maximum(i - MG_NPT, 0), 0)
    const = lambda i: (0, 0)
    resident = dict(pipeline_mode=pl.Buffered(1))
    return pl.pallas_call(
        _merge_kernel,
        grid=(MG_NT,),
        in_specs=[
            pl.BlockSpec((MG_TM, D_MODEL), p_idx),
            pl.BlockSpec((MG_TM, D_MODEL), s_idx),
            pl.BlockSpec((MG_TM, GMLP_W), p_idx),
            pl.BlockSpec((MG_TM, GMLP_W), s_idx),
            pl.BlockSpec((MG_TM, Q_W), p_idx),
            pl.BlockSpec((MG_TM, Q_W), s_idx),
            pl.BlockSpec((MG_TM, D_MODEL), lambda i: (i, Z_GA // D_MODEL)),
            pl.BlockSpec((MG_TM, D_MODEL), lambda i: (i, Z_GB // D_MODEL)),
            pl.BlockSpec((GMLP_W, D_MODEL), const, **resident),
            pl.BlockSpec((Q_W, D_MODEL), const, **resident),
            pl.BlockSpec((D_MODEL, D_MODEL), const, **resident),
        ],
        out_specs=pl.BlockSpec((MG_TM, D_MODEL), lambda i: (i, 0)),
        out_shape=jax.ShapeDtypeStruct((N_TOK, D_MODEL), F32),
        compiler_params=pltpu.CompilerParams(
            dimension_semantics=("arbitrary",), vmem_limit_bytes=48 * MIB),
        name="merge",
    )(xp, xs, a_p, a_s, o_p, o_s, z, z, wpa, wpb, wo)


FF_TM = 1024
FF_TF = 512
FF_NJ = D_FF // FF_TF


def _ffn_kernel(x1_ref, n2_ref, wg_ref, wu_ref, wd_ref, fg_ref, y_ref, hh_ref, act_ref):
    j = pl.program_id(1)

    def gate_up(slot):
        hh = hh_ref[...]
        act = jax.nn.silu(_dot(hh, wg_ref[...])) * _dot(hh, wu_ref[...])
        act_ref[slot] = act.astype(BF16)

    def down(slot, first, last):
        acc = (x1_ref if first else y_ref)[...] + _dot(act_ref[slot], wd_ref[...])
        y_ref[...] = _rms(acc, fg_ref[...]) if last else acc

    @pl.when(j == 0)
    def _():
        hh_ref[...] = _rms(x1_ref[...], n2_ref[...]).astype(BF16)
        gate_up(0)

    @pl.when(j == 1)
    def _():
        gate_up(1)
        down(0, True, False)

    @pl.when((j > 1) & (j < FF_NJ))
    def _():
        gate_up(j % 2)
        down(1 - j % 2, False, False)

    @pl.when(j == FF_NJ)
    def _():
        down((FF_NJ - 1) % 2, False, True)


def _ffn(x1, row0, nrows, n2, wg, wu, wd, fg, name):
    blk0 = row0 // FF_TM
    return pl.pallas_call(
        _ffn_kernel,
        grid=(nrows // FF_TM, FF_NJ + 1),
        in_specs=[
            pl.BlockSpec((FF_TM, D_MODEL), lambda i, j: (blk0 + i, 0)),
            pl.BlockSpec((1, D_MODEL), lambda i, j: (0, 0)),
            pl.BlockSpec((D_MODEL, FF_TF), lambda i, j: (0, jnp.minimum(j, FF_NJ - 1))),
            pl.BlockSpec((D_MODEL, FF_TF), lambda i, j: (0, jnp.minimum(j, FF_NJ - 1))),
            pl.BlockSpec((FF_TF, D_MODEL), lambda i, j: (jnp.maximum(j - 1, 0), 0)),
            pl.BlockSpec((1, D_MODEL), lambda i, j: (0, 0)),
        ],
        out_specs=pl.BlockSpec((FF_TM, D_MODEL), lambda i, j: (i, 0)),
        out_shape=jax.ShapeDtypeStruct((nrows, D_MODEL), F32),
        scratch_shapes=[pltpu.VMEM((FF_TM, D_MODEL), BF16), pltpu.VMEM((2, FF_TM, FF_TF), BF16)],
        compiler_params=pltpu.CompilerParams(
            dimension_semantics=("arbitrary", "arbitrary"), vmem_limit_bytes=58 * MIB),
        name=name,
    )(x1, n2, wg, wu, wd, fg)


def kernel(x_prompt, x_sample, cache_k, cache_v, norm1_g, w_in, gmlp_norm_g, w_s, b_s, sinks,
           w_pa, w_pb, w_o, norm2_g, w_ff_gate, w_ff_up, w_ff_down, final_g):
    xp = x_prompt.reshape(N_PROMPT, D_MODEL)
    xs = x_sample.reshape(N_SAMPLE, D_MODEL)

    z, kv = _inproj(xp, xs, norm1_g.reshape(1, D_MODEL), w_in.reshape(D_MODEL, IN_W))

    sinks1 = sinks.reshape(N_Q_HEADS)
    o_p, pk, pv, wg, wu = _attn_prompt(
        sinks1, z, kv, (w_ff_gate.reshape(D_MODEL, D_FF), w_ff_up.reshape(D_MODEL, D_FF)))
    o_s, k_win, v_win, wd, wpa, wpb = _attn_sample(
        sinks1, z, cache_k.reshape(DEC_BATCH, WINDOW, KV_W),
        cache_v.reshape(DEC_BATCH, WINDOW, KV_W), kv,
        (w_ff_down.reshape(D_FF, D_MODEL), w_pa.reshape(GMLP_W, D_MODEL), w_pb.reshape(Q_W, D_MODEL)))

    lng = gmlp_norm_g.reshape(1, GMLP_W)
    ws = w_s.reshape(GROUPS, CHUNK, CHUNK)
    bias = b_s.reshape(GROUPS, CHUNK).T
    a_p, wo = _gmlp(z, lng, ws, bias, sample=False, cast_weights=(w_o.reshape(D_MODEL, D_MODEL),))
    a_s, gvn_s = _gmlp(z, lng, ws, bias, sample=True)

    x1 = _merge(xp, xs, a_p, a_s, o_p, o_s, z, wpa, wpb, wo)

    ffn_w = (norm2_g.reshape(1, D_MODEL), wg, wu, wd, final_g.reshape(1, D_MODEL))
    y_p = _ffn(x1, 0, N_PROMPT, *ffn_w, name="ffn_prompt")
    y_s = _ffn(x1, N_PROMPT, N_SAMPLE, *ffn_w, name="ffn_sample")

    return (y_p.reshape(BATCH, SEQ, D_MODEL),
            y_s.reshape(DEC_BATCH, DEC_SEQ, D_MODEL),
            pk.reshape(1, BATCH, WINDOW, N_KV_HEADS, HEAD_DIM),
            pv.reshape(1, BATCH, WINDOW, N_KV_HEADS, HEAD_DIM),
            k_win.reshape(1, DEC_BATCH, WINDOW, N_KV_HEADS, HEAD_DIM),
            v_win.reshape(1, DEC_BATCH, WINDOW, N_KV_HEADS, HEAD_DIM),
            gvn_s.reshape(1, DEC_BATCH, DEC_SEQ, GROUPS, GROUP_DIM))
```

```python
import functools

import jax
import jax.numpy as jnp
from jax import lax
from jax.experimental import pallas as pl
from jax.experimental.pallas import tpu as pltpu

F32 = jnp.float32
BF16 = jnp.bfloat16

D_MODEL = 2048
BATCH = 4
SEQ = 2048
DEC_BATCH = 128
DEC_SEQ = 8
HEAD_DIM = 64
N_Q_HEADS = 16
N_KV_HEADS = 4
Q_PER_KV = N_Q_HEADS // N_KV_HEADS
WINDOW = 128
CHUNK = 128
GROUPS = 8
GROUP_DIM = 128
Q_W = N_Q_HEADS * HEAD_DIM
KV_W = N_KV_HEADS * HEAD_DIM
GMLP_W = GROUPS * GROUP_DIM
D_FF = 5632
IN_W = Q_W + 2 * KV_W + 2 * GMLP_W + 2 * D_MODEL
EPS = 1e-6
NEG = -1e30

N_PROMPT = BATCH * SEQ
N_SAMPLE = DEC_BATCH * DEC_SEQ
N_TOK = N_PROMPT + N_SAMPLE

Z_W = 2 * D_MODEL + Q_W + 2 * GMLP_W
Z_GA, Z_GB, Z_Q, Z_U, Z_GV = 0, 2048, 4096, 5120, 6144

MIB = 1024 * 1024


def _rms(x, g):
    return x * lax.rsqrt(jnp.mean(x * x, axis=-1, keepdims=True) + EPS) * g


def _dot(a, b):
    return jnp.dot(a, b, preferred_element_type=F32)


def _dot_t(a, b):
    return lax.dot_general(a, b, (((1,), (1,)), ((), ())), preferred_element_type=F32)


IP_TM = 1024
IP_TN = 512
IP_CH = 256
IP_NPT = N_PROMPT // IP_TM
IP_NT = N_TOK // IP_TM
IP_NJ = 8


def _inproj_wa(j):
    return jnp.where(j == 0, 0, jnp.where(j == 1, 2, 2 * j - 1))


def _inproj_wb(j):
    return jnp.where(j == 0, 1, jnp.where(j == 1, 4, 2 * j))


def _inproj_zcol(j):
    return jnp.where(j < 2, Z_Q // (2 * IP_TN), jnp.where(j < 4, j + 3, j - 4))


def _inproj_kernel(xp_ref, xs_ref, g_ref, wa_ref, wb_ref, z_ref, kv_ref, h_ref):
    i = pl.program_id(0)
    j = pl.program_id(1)
    nch = IP_TM // IP_CH

    def first_step(x_ref):
        wa = wa_ref[...].astype(BF16)
        wb = wb_ref[...].astype(BF16)
        for c in range(nch):
            rows = slice(c * IP_CH, (c + 1) * IP_CH)
            h = _rms(x_ref[rows, :], g_ref[...]).astype(BF16)
            h_ref[rows, :] = h
            z_ref[rows, 0:IP_TN] = _dot(h, wa).astype(BF16)
            z_ref[rows, IP_TN:2 * IP_TN] = _dot(h, wb).astype(BF16)

    def pair_step(act):
        wa = wa_ref[...].astype(BF16)
        wb = wb_ref[...].astype(BF16)
        for c in range(nch):
            rows = slice(c * IP_CH, (c + 1) * IP_CH)
            h = h_ref[rows, :]
            z_ref[rows, 0:IP_TN] = act(_dot(h, wa)).astype(BF16)
            z_ref[rows, IP_TN:2 * IP_TN] = act(_dot(h, wb)).astype(BF16)

    @pl.when((j == 0) & (i < IP_NPT))
    def _():
        first_step(xp_ref)

    @pl.when((j == 0) & (i >= IP_NPT))
    def _():
        first_step(xs_ref)

    @pl.when(j == 1)
    def _():
        wa = wa_ref[...].astype(BF16)
        for c in range(nch):
            rows = slice(c * IP_CH, (c + 1) * IP_CH)
            kv_ref[rows, :] = _dot(h_ref[rows, :], wa)

    @pl.when((j >= 2) & (j < 4))
    def _():
        pair_step(jax.nn.gelu)

    @pl.when(j >= 4)
    def _():
        pair_step(jax.nn.sigmoid)


def _inproj(xp, xs, g, w):
    return pl.pallas_call(
        _inproj_kernel,
        grid=(IP_NT, IP_NJ),
        in_specs=[
            pl.BlockSpec((IP_TM, D_MODEL), lambda i, j: (jnp.minimum(i, IP_NPT - 1), 0)),
            pl.BlockSpec((IP_TM, D_MODEL), lambda i, j: (jnp.maximum(i - IP_NPT, 0), 0),
                         pipeline_mode=pl.Buffered(1)),
            pl.BlockSpec((1, D_MODEL), lambda i, j: (0, 0)),
            pl.BlockSpec((D_MODEL, IP_TN), lambda i, j: (0, _inproj_wa(j))),
            pl.BlockSpec((D_MODEL, IP_TN), lambda i, j: (0, _inproj_wb(j))),
        ],
        out_specs=[
            pl.BlockSpec((IP_TM, 2 * IP_TN), lambda i, j: (i, _inproj_zcol(j))),
            pl.BlockSpec((IP_TM, 2 * KV_W), lambda i, j: (i, 0)),
        ],
        out_shape=[
            jax.ShapeDtypeStruct((N_TOK, Z_W), BF16),
            jax.ShapeDtypeStruct((N_TOK, 2 * KV_W), F32),
        ],
        scratch_shapes=[pltpu.VMEM((IP_TM, D_MODEL), BF16)],
        compiler_params=pltpu.CompilerParams(
            dimension_semantics=("arbitrary", "arbitrary"), vmem_limit_bytes=60 * MIB),
        name="inproj",
    )(xp, xs, g, w, w)


NBLK = SEQ // WINDOW


PA_QB = 4
PA_ROWS = PA_QB * WINDOW
PA_STEPS_PER_SEQ = NBLK // PA_QB
GRP_W = Q_PER_KV * HEAD_DIM


def _lane_group_rotations(x):
    x1 = pltpu.roll(x, HEAD_DIM, 1)
    return [x.astype(BF16), x1.astype(BF16),
            pltpu.roll(x, 2 * HEAD_DIM, 1).astype(BF16), pltpu.roll(x1, 2 * HEAD_DIM, 1).astype(BF16)]


def _cast_blocks(src_refs, dst_refs):
    for src, dst in zip(src_refs, dst_refs, strict=True):
        dst[...] = src[...].astype(BF16)


def _cast_specs(weights, nblocks):
    specs = [pl.BlockSpec((w.shape[0] // nblocks, w.shape[1]), lambda t: (jnp.minimum(t, nblocks - 1), 0))
             for w in weights]
    shapes = [jax.ShapeDtypeStruct(w.shape, BF16) for w in weights]
    return specs, shapes


def _attn_prompt_kernel(sink_ref, q_ref, kc_ref, kp_ref, *refs):
    ncast = (len(refs) - 3) // 2
    o_ref, pk_ref, pv_ref = refs[ncast:ncast + 3]
    _cast_blocks(refs[:ncast], refs[ncast + 3:])
    t = pl.program_id(0)
    kv = jnp.concatenate([kp_ref[...], kc_ref[...]], axis=0)
    k_rot = _lane_group_rotations(kv[:, :KV_W])
    v_rot = _lane_group_rotations(kv[:, KV_W:])
    lane_grp = lax.broadcasted_iota(jnp.int32, (WINDOW, GRP_W), 1) >> 6
    in_grp = [lane_grp == g for g in range(Q_PER_KV)]
    nrow = N_KV_HEADS * WINDOW
    key = lax.broadcasted_iota(jnp.int32, (2 * WINDOW, nrow), 0)
    qry = lax.broadcasted_iota(jnp.int32, (2 * WINDOW, nrow), 1) & (WINDOW - 1)
    band = (key >= qry) & (key <= qry + WINDOW)
    bias_inner = jnp.where(band, 0.0, NEG)
    bias_first = jnp.where(band & (key >= WINDOW), 0.0, NEG)

    for qb in range(PA_QB):
        is_first = ((t * PA_QB + qb) % NBLK) == 0
        bias = jnp.where(is_first, bias_first, bias_inner)
        q = q_ref[qb * WINDOW:(qb + 1) * WINDOW, :] * (HEAD_DIM ** -0.5)
        keys = slice(qb * WINDOW, (qb + 2) * WINDOW)
        outs = [[None] * Q_PER_KV for _ in range(N_KV_HEADS)]
        for rot in range(Q_PER_KV):
            lhs = jnp.concatenate(
                [jnp.where(in_grp[(h + rot) % Q_PER_KV], q[:, h * GRP_W:(h + 1) * GRP_W], 0)
                 for h in range(N_KV_HEADS)], axis=0)
            sink = jnp.concatenate(
                [jnp.full((1, WINDOW), sink_ref[h * Q_PER_KV + (h + rot) % Q_PER_KV], F32)
                 for h in range(N_KV_HEADS)], axis=1)
            s = _dot_t(k_rot[rot][keys], lhs) + bias
            m = jnp.maximum(jnp.max(s, axis=0, keepdims=True), sink)
            p = jnp.exp(s - m)
            denom = jnp.sum(p, axis=0, keepdims=True) + jnp.exp(sink - m)
            probs = (p / denom).astype(BF16)
            o = lax.dot_general(probs, v_rot[rot][keys], (((0,), (0,)), ((), ())),
                                preferred_element_type=F32)
            for h in range(N_KV_HEADS):
                outs[h][(h + rot) % Q_PER_KV] = o[h * WINDOW:(h + 1) * WINDOW]
        for h in range(N_KV_HEADS):
            oh = outs[h][Q_PER_KV - 1]
            for g in range(Q_PER_KV - 2, -1, -1):
                oh = jnp.where(in_grp[g], outs[h][g], oh)
            o_ref[qb * WINDOW:(qb + 1) * WINDOW, h * GRP_W:(h + 1) * GRP_W] = oh.astype(BF16)

    @pl.when(t % PA_STEPS_PER_SEQ == PA_STEPS_PER_SEQ - 1)
    def _():
        pk_ref[0] = kc_ref[PA_ROWS - WINDOW:PA_ROWS, :KV_W]
        pv_ref[0] = kc_ref[PA_ROWS - WINDOW:PA_ROWS, KV_W:]


def _attn_prompt(sinks, z, kv, cast_weights):
    nsteps = N_PROMPT // PA_ROWS
    cast_specs, cast_shapes = _cast_specs(cast_weights, nsteps)
    return pl.pallas_call(
        _attn_prompt_kernel,
        grid=(nsteps,),
        in_specs=[
            pl.BlockSpec(memory_space=pltpu.SMEM),
            pl.BlockSpec((PA_ROWS, Q_W), lambda t: (t, Z_Q // Q_W)),
            pl.BlockSpec((PA_ROWS, 2 * KV_W), lambda t: (t, 0)),
            pl.BlockSpec((WINDOW, 2 * KV_W), lambda t: (jnp.maximum(t * PA_QB - 1, 0), 0)),
        ] + cast_specs,
        out_specs=[
            pl.BlockSpec((PA_ROWS, Q_W), lambda t: (t, 0)),
            pl.BlockSpec((1, WINDOW, KV_W), lambda t: (t // PA_STEPS_PER_SEQ, 0, 0)),
            pl.BlockSpec((1, WINDOW, KV_W), lambda t: (t // PA_STEPS_PER_SEQ, 0, 0)),
        ] + cast_specs,
        out_shape=[
            jax.ShapeDtypeStruct((N_PROMPT, Q_W), BF16),
            jax.ShapeDtypeStruct((BATCH, WINDOW, KV_W), F32),
            jax.ShapeDtypeStruct((BATCH, WINDOW, KV_W), F32),
        ] + cast_shapes,
        compiler_params=pltpu.CompilerParams(
            dimension_semantics=("arbitrary",), vmem_limit_bytes=48 * MIB),
        name="attn_prompt",
    )(sinks, z, kv, kv, *cast_weights)


SA_BT = 16
SA_ROWS = N_Q_HEADS * DEC_SEQ
SA_STEPS = DEC_BATCH // SA_BT
SA_GROUP = 16


def _attn_sample_kernel(sink_ref, q_ref, ck_ref, cv_ref, kvn_ref, *refs):
    ncast = (len(refs) - 3) // 2
    o_ref, kw_ref, vw_ref = refs[ncast:ncast + 3]
    _cast_blocks(refs[:ncast], refs[ncast + 3:])
    kvn = kvn_ref[...]
    k_new = kvn[:, :KV_W].astype(BF16)
    v_new = kvn[:, KV_W:].astype(BF16)
    head_of_lane = lax.broadcasted_iota(jnp.int32, (1, SA_ROWS), 1) >> 3
    sink_seq = jnp.zeros((1, SA_ROWS), F32)
    for hr in range(N_Q_HEADS):
        sink_seq = jnp.where(head_of_lane == hr, sink_ref[hr], sink_seq)

    def group(b0, nb):
        ntok = nb * DEC_SEQ
        nrow = nb * SA_ROWS
        tok = slice(b0 * DEC_SEQ, b0 * DEC_SEQ + ntok)
        lane_grp = lax.broadcasted_iota(jnp.int32, (ntok, GRP_W), 1) >> 6
        in_grp = [lane_grp == g for g in range(Q_PER_KV)]
        qf = q_ref[tok, :].astype(F32) * (HEAD_DIM ** -0.5)
        q_exp = []
        for h in range(N_KV_HEADS):
            q_h = qf[:, h * GRP_W:(h + 1) * GRP_W]
            for r in range(Q_PER_KV):
                shift = ((h - r) % Q_PER_KV) * HEAD_DIM
                moved = q_h if shift == 0 else pltpu.roll(q_h, shift, 1)
                q_exp.append(jnp.where(in_grp[h], moved, 0.0))
        lhs = jnp.concatenate([q_exp[hr][b * DEC_SEQ:(b + 1) * DEC_SEQ]
                               for b in range(nb) for hr in range(N_Q_HEADS)], axis=0).astype(BF16)
        seq_rows = lambda b: slice(b * SA_ROWS, (b + 1) * SA_ROWS)
        s_c = jnp.concatenate(
            [_dot_t(ck_ref[b0 + b].astype(BF16), lhs[seq_rows(b)]) for b in range(nb)], axis=1)
        s_n = _dot_t(k_new, lhs)

        key = lax.broadcasted_iota(jnp.int32, (WINDOW, nrow), 0)
        qry = lax.broadcasted_iota(jnp.int32, (WINDOW, nrow), 1)
        tq = qry & (DEC_SEQ - 1)
        s_c = jnp.where(key >= tq, s_c, NEG)
        s_n = jnp.where(((key >> 3) == (qry >> 7) + b0) & ((key & (DEC_SEQ - 1)) <= tq), s_n, NEG)
        sink = jnp.concatenate([sink_seq] * nb, axis=1)
        m = jnp.maximum(jnp.maximum(jnp.max(s_c, axis=0, keepdims=True),
                                    jnp.max(s_n, axis=0, keepdims=True)), sink)
        p_c = jnp.exp(s_c - m)
        p_n = jnp.exp(s_n - m)
        denom = (jnp.sum(p_c, axis=0, keepdims=True) + jnp.sum(p_n, axis=0, keepdims=True)
                 + jnp.exp(sink - m))
        p_c = (p_c / denom).astype(BF16)
        p_n = (p_n / denom).astype(BF16)
        contract0 = (((0,), (0,)), ((), ()))
        o = jnp.concatenate(
            [lax.dot_general(p_c[:, seq_rows(b)], cv_ref[b0 + b].astype(BF16), contract0,
                             preferred_element_type=F32) for b in range(nb)], axis=0)
        o = o + lax.dot_general(p_n, v_new, contract0, preferred_element_type=F32)

        for h in range(N_KV_HEADS):
            parts = []
            for r in range(Q_PER_KV):
                hr = h * Q_PER_KV + r
                o_hr = jnp.concatenate(
                    [o[b * SA_ROWS + hr * DEC_SEQ:b * SA_ROWS + (hr + 1) * DEC_SEQ] for b in range(nb)],
                    axis=0)
                shift = ((r - h) % Q_PER_KV) * HEAD_DIM
                parts.append(o_hr if shift == 0 else pltpu.roll(o_hr, shift, 1))
            o_h = parts[Q_PER_KV - 1]
            for g in range(Q_PER_KV - 2, -1, -1):
                o_h = jnp.where(in_grp[g], parts[g], o_h)
            o_ref[tok, h * GRP_W:(h + 1) * GRP_W] = o_h.astype(BF16)

    for g0 in range(0, SA_BT, SA_GROUP):
        group(g0, SA_GROUP)

    for b in range(SA_BT):
        kw_ref[b, 0:WINDOW - DEC_SEQ, :] = ck_ref[b, DEC_SEQ:WINDOW, :]
        kw_ref[b, WINDOW - DEC_SEQ:WINDOW, :] = kvn[b * DEC_SEQ:(b + 1) * DEC_SEQ, :KV_W]
        vw_ref[b, 0:WINDOW - DEC_SEQ, :] = cv_ref[b, DEC_SEQ:WINDOW, :]
        vw_ref[b, WINDOW - DEC_SEQ:WINDOW, :] = kvn[b * DEC_SEQ:(b + 1) * DEC_SEQ, KV_W:]


def _attn_sample(sinks, z, ck, cv, kv, cast_weights):
    step_tok = SA_BT * DEC_SEQ
    blk0 = N_PROMPT // step_tok
    cast_specs, cast_shapes = _cast_specs(cast_weights, SA_STEPS)
    return pl.pallas_call(
        _attn_sample_kernel,
        grid=(SA_STEPS,),
        in_specs=[
            pl.BlockSpec(memory_space=pltpu.SMEM),
            pl.BlockSpec((step_tok, Q_W), lambda s: (blk0 + s, Z_Q // Q_W)),
            pl.BlockSpec((SA_BT, WINDOW, KV_W), lambda s: (s, 0, 0)),
            pl.BlockSpec((SA_BT, WINDOW, KV_W), lambda s: (s, 0, 0)),
            pl.BlockSpec((step_tok, 2 * KV_W), lambda s: (blk0 + s, 0)),
        ] + cast_specs,
        out_specs=[
            pl.BlockSpec((step_tok, Q_W), lambda s: (s, 0)),
            pl.BlockSpec((SA_BT, WINDOW, KV_W), lambda s: (s, 0, 0)),
            pl.BlockSpec((SA_BT, WINDOW, KV_W), lambda s: (s, 0, 0)),
        ] + cast_specs,
        out_shape=[
            jax.ShapeDtypeStruct((N_SAMPLE, Q_W), BF16),
            jax.ShapeDtypeStruct((DEC_BATCH, WINDOW, KV_W), F32),
            jax.ShapeDtypeStruct((DEC_BATCH, WINDOW, KV_W), F32),
        ] + cast_shapes,
        compiler_params=pltpu.CompilerParams(
            dimension_semantics=("arbitrary",), vmem_limit_bytes=48 * MIB),
        name="attn_sample",
    )(sinks, z, ck, cv, kv, *cast_weights)


def _gmlp_kernel(u_ref, gv_ref, lng_ref, w_ref, bias_ref, *refs, nchunk, sample, ncast):
    nout = 2 if sample else 1
    a_ref, *gvn_refs = refs[ncast:ncast + nout]
    _cast_blocks(refs[:ncast], refs[ncast + nout:])
    ri = lax.broadcasted_iota(jnp.int32, (CHUNK, CHUNK), 0)
    ci = lax.broadcasted_iota(jnp.int32, (CHUNK, CHUNK), 1)
    nseq = CHUNK // DEC_SEQ
    if sample:
        wmask = ((ri >> 3) == (ci >> 3)) & ((ri & (DEC_SEQ - 1)) >= (ci & (DEC_SEQ - 1)))
        corner_lanes = lax.broadcasted_iota(jnp.int32, (DEC_SEQ, CHUNK), 1) < DEC_SEQ

        def group_w(g):
            w8 = jnp.where(corner_lanes, w_ref[g, 0:DEC_SEQ, :], 0.0)
            shift = DEC_SEQ
            while shift < CHUNK:
                w8 = w8 + pltpu.roll(w8, shift, 1)
                shift *= 2
            return jnp.concatenate([w8] * nseq, axis=0)

        bias = jnp.concatenate([bias_ref[0:DEC_SEQ, :]] * nseq, axis=0)
    else:
        wmask = ri >= ci
        group_w = lambda g: w_ref[g]
        bias = bias_ref[...]
    w = [jnp.where(wmask, group_w(g), 0.0).astype(BF16) for g in range(GROUPS)]
    lng = lng_ref[...]
    for c in range(nchunk):
        rows = slice(c * CHUNK, (c + 1) * CHUNK)
        gv = gv_ref[rows, :].astype(F32)
        xc = gv - jnp.mean(gv, axis=-1, keepdims=True)
        y = xc * lax.rsqrt(jnp.mean(xc * xc, axis=-1, keepdims=True) + EPS) * lng
        if sample:
            gvn_refs[0][rows, :] = y
        yb = y.astype(BF16)
        for g in range(GROUPS):
            cols = slice(g * GROUP_DIM, (g + 1) * GROUP_DIM)
            mixed = _dot(w[g], yb[:, cols]) + bias[:, g:g + 1]
            a_ref[rows, cols] = (u_ref[rows, cols].astype(F32) * mixed).astype(BF16)


def _gmlp(z, lng, w, bias, *, sample, cast_weights=()):
    rows = CHUNK if sample else 8 * CHUNK
    nsteps = (N_SAMPLE if sample else N_PROMPT) // rows
    blk0 = (N_PROMPT // rows) if sample else 0
    n_out = N_SAMPLE if sample else N_PROMPT
    out_specs = [pl.BlockSpec((rows, GMLP_W), lambda s: (s, 0))]
    out_shape = [jax.ShapeDtypeStruct((n_out, GMLP_W), BF16)]
    if sample:
        out_specs.append(pl.BlockSpec((rows, GMLP_W), lambda s: (s, 0)))
        out_shape.append(jax.ShapeDtypeStruct((n_out, GMLP_W), F32))
    cast_specs, cast_shapes = _cast_specs(cast_weights, nsteps)
    return pl.pallas_call(
        functools.partial(_gmlp_kernel, nchunk=rows // CHUNK, sample=sample, ncast=len(cast_weights)),
        grid=(nsteps,),
        in_specs=[
            pl.BlockSpec((rows, GMLP_W), lambda s: (blk0 + s, Z_U // GMLP_W)),
            pl.BlockSpec((rows, GMLP_W), lambda s: (blk0 + s, Z_GV // GMLP_W)),
            pl.BlockSpec((1, GMLP_W), lambda s: (0, 0)),
            pl.BlockSpec((GROUPS, CHUNK, CHUNK), lambda s: (0, 0, 0)),
            pl.BlockSpec((CHUNK, GROUPS), lambda s: (0, 0)),
        ] + cast_specs,
        out_specs=out_specs + cast_specs,
        out_shape=out_shape + cast_shapes,
        compiler_params=pltpu.CompilerParams(
            dimension_semantics=("arbitrary",), vmem_limit_bytes=40 * MIB),
        name="gmlp_sample" if sample else "gmlp_prompt",
    )(z, z, lng, w, bias, *cast_weights)


MG_TM = 256
MG_NPT = N_PROMPT // MG_TM
MG_NT = N_TOK // MG_TM
MG_CAST_BLOCKS = 32


def _merge_kernel(xp_ref, xs_ref, ap_ref, as_ref, op_ref, os_ref, ga_ref, gb_ref,
                  wpa_ref, wpb_ref, wo_ref, *refs):
    ncast = (len(refs) - 1) // 2
    x1_ref = refs[ncast]
    i = pl.program_id(0)

    @pl.when(i < MG_CAST_BLOCKS)
    def _():
        _cast_blocks(refs[:ncast], refs[ncast + 1:])

    def body(x_ref, a_ref, o_ref):
        pa = _dot(a_ref[...], wpa_ref[...])
        pb = _dot(o_ref[...], wpb_ref[...])
        mix = ga_ref[...].astype(F32) * pa + gb_ref[...].astype(F32) * pb
        x1_ref[...] = x_ref[...] + _dot(mix.astype(BF16), wo_ref[...])

    @pl.when(i < MG_NPT)
    def _():
        body(xp_ref, ap_ref, op_ref)

    @pl.when(i >= MG_NPT)
    def _():
        body(xs_ref, as_ref, os_ref)


def _merge(xp, xs, a_p, a_s, o_p, o_s, z, wpa, wpb, wo, cast_weights):
    cast_specs, cast_shapes = _cast_specs(cast_weights, MG_CAST_BLOCKS)
    p_idx = lambda i: (jnp.minimum(i, MG_NPT - 1), 0)
    s_idx = lambda i: (jnp.maximum(i - MG_NPT, 0), 0)
    const = lambda i: (0, 0)
    resident = dict(pipeline_mode=pl.Buffered(1))
    return pl.pallas_call(
        _merge_kernel,
        grid=(MG_NT,),
        in_specs=[
            pl.BlockSpec((MG_TM, D_MODEL), p_idx),
            pl.BlockSpec((MG_TM, D_MODEL), s_idx),
            pl.BlockSpec((MG_TM, GMLP_W), p_idx),
            pl.BlockSpec((MG_TM, GMLP_W), s_idx),
            pl.BlockSpec((MG_TM, Q_W), p_idx),
            pl.BlockSpec((MG_TM, Q_W), s_idx),
            pl.BlockSpec((MG_TM, D_MODEL), lambda i: (i, Z_GA // D_MODEL)),
            pl.BlockSpec((MG_TM, D_MODEL), lambda i: (i, Z_GB // D_MODEL)),
            pl.BlockSpec((GMLP_W, D_MODEL), const, **resident),
            pl.BlockSpec((Q_W, D_MODEL), const, **resident),
            pl.BlockSpec((D_MODEL, D_MODEL), const, **resident),
        ] + cast_specs,
        out_specs=[pl.BlockSpec((MG_TM, D_MODEL), lambda i: (i, 0))] + cast_specs,
        out_shape=[jax.ShapeDtypeStruct((N_TOK, D_MODEL), F32)] + cast_shapes,
        compiler_params=pltpu.CompilerParams(
            dimension_semantics=("arbitrary",), vmem_limit_bytes=56 * MIB),
        name="merge",
    )(xp, xs, a_p, a_s, o_p, o_s, z, z, wpa, wpb, wo, *cast_weights)


FF_TM = 1024
FF_TF = 512
FF_NJ = D_FF // FF_TF


def _ffn_kernel(x1_ref, n2_ref, wg_ref, wu_ref, wd_ref, fg_ref, y_ref, hh_ref, act_ref):
    j = pl.program_id(1)

    def gate_up(slot):
        hh = hh_ref[...]
        act = jax.nn.silu(_dot(hh, wg_ref[...])) * _dot(hh, wu_ref[...])
        act_ref[slot] = act.astype(BF16)

    def down(slot, first, last):
        acc = (x1_ref if first else y_ref)[...] + _dot(act_ref[slot], wd_ref[...])
        y_ref[...] = _rms(acc, fg_ref[...]) if last else acc

    @pl.when(j == 0)
    def _():
        hh_ref[...] = _rms(x1_ref[...], n2_ref[...]).astype(BF16)
        gate_up(0)

    @pl.when(j == 1)
    def _():
        gate_up(1)
        down(0, True, False)

    @pl.when((j > 1) & (j < FF_NJ))
    def _():
        gate_up(j % 2)
        down(1 - j % 2, False, False)

    @pl.when(j == FF_NJ)
    def _():
        down((FF_NJ - 1) % 2, False, True)


def _ffn(x1, row0, nrows, n2, wg, wu, wd, fg, name):
    blk0 = row0 // FF_TM
    return pl.pallas_call(
        _ffn_kernel,
        grid=(nrows // FF_TM, FF_NJ + 1),
        in_specs=[
            pl.BlockSpec((FF_TM, D_MODEL), lambda i, j: (blk0 + i, 0)),
            pl.BlockSpec((1, D_MODEL), lambda i, j: (0, 0)),
            pl.BlockSpec((D_MODEL, FF_TF), lambda i, j: (0, jnp.minimum(j, FF_NJ - 1))),
            pl.BlockSpec((D_MODEL, FF_TF), lambda i, j: (0, jnp.minimum(j, FF_NJ - 1))),
            pl.BlockSpec((FF_TF, D_MODEL), lambda i, j: (jnp.maximum(j - 1, 0), 0)),
            pl.BlockSpec((1, D_MODEL), lambda i, j: (0, 0)),
        ],
        out_specs=pl.BlockSpec((FF_TM, D_MODEL), lambda i, j: (i, 0)),
        out_shape=jax.ShapeDtypeStruct((nrows, D_MODEL), F32),
        scratch_shapes=[pltpu.VMEM((FF_TM, D_MODEL), BF16), pltpu.VMEM((2, FF_TM, FF_TF), BF16)],
        compiler_params=pltpu.CompilerParams(
            dimension_semantics=("arbitrary", "arbitrary"), vmem_limit_bytes=58 * MIB),
        name=name,
    )(x1, n2, wg, wu, wd, fg)


def kernel(x_prompt, x_sample, cache_k, cache_v, norm1_g, w_in, gmlp_norm_g, w_s, b_s, sinks,
           w_pa, w_pb, w_o, norm2_g, w_ff_gate, w_ff_up, w_ff_down, final_g):
    xp = x_prompt.reshape(N_PROMPT, D_MODEL)
    xs = x_sample.reshape(N_SAMPLE, D_MODEL)

    z, kv = _inproj(xp, xs, norm1_g.reshape(1, D_MODEL), w_in.reshape(D_MODEL, IN_W))

    sinks1 = sinks.reshape(N_Q_HEADS)
    o_p, pk, pv, wpa, wpb, wo = _attn_prompt(
        sinks1, z, kv, (w_pa.reshape(GMLP_W, D_MODEL), w_pb.reshape(Q_W, D_MODEL),
                        w_o.reshape(D_MODEL, D_MODEL)))
    o_s, k_win, v_win = _attn_sample(
        sinks1, z, cache_k.reshape(DEC_BATCH, WINDOW, KV_W),
        cache_v.reshape(DEC_BATCH, WINDOW, KV_W), kv, ())

    lng = gmlp_norm_g.reshape(1, GMLP_W)
    ws = w_s.reshape(GROUPS, CHUNK, CHUNK)
    bias = b_s.reshape(GROUPS, CHUNK).T
    (a_p,) = _gmlp(z, lng, ws, bias, sample=False)
    a_s, gvn_s = _gmlp(z, lng, ws, bias, sample=True)

    x1, wg, wu, wd = _merge(
        xp, xs, a_p, a_s, o_p, o_s, z, wpa, wpb, wo,
        (w_ff_gate.reshape(D_MODEL, D_FF), w_ff_up.reshape(D_MODEL, D_FF), w_ff_down.reshape(D_FF, D_MODEL)))

    ffn_w = (norm2_g.reshape(1, D_MODEL), wg, wu, wd, final_g.reshape(1, D_MODEL))
    y_p = _ffn(x1, 0, N_PROMPT, *ffn_w, name="ffn_prompt")
    y_s = _ffn(x1, N_PROMPT, N_SAMPLE, *ffn_w, name="ffn_sample")

    return (y_p.reshape(BATCH, SEQ, D_MODEL),
            y_s.reshape(DEC_BATCH, DEC_SEQ, D_MODEL),
            pk.reshape(1, BATCH, WINDOW, N_KV_HEADS, HEAD_DIM),
            pv.reshape(1, BATCH, WINDOW, N_KV_HEADS, HEAD_DIM),
            k_win.reshape(1, DEC_BATCH, WINDOW, N_KV_HEADS, HEAD_DIM),
            v_win.reshape(1, DEC_BATCH, WINDOW, N_KV_HEADS, HEAD_DIM),
            gvn_s.reshape(1, DEC_BATCH, DEC_SEQ, GROUPS, GROUP_DIM))
```

```python
import functools

import jax
import jax.numpy as jnp
from jax import lax
from jax.experimental import pallas as pl
from jax.experimental.pallas import tpu as pltpu

F32 = jnp.float32
BF16 = jnp.bfloat16

D_MODEL = 2048
BATCH = 4
SEQ = 2048
DEC_BATCH = 128
DEC_SEQ = 8
HEAD_DIM = 64
N_Q_HEADS = 16
N_KV_HEADS = 4
Q_PER_KV = N_Q_HEADS // N_KV_HEADS
WINDOW = 128
CHUNK = 128
GROUPS = 8
GROUP_DIM = 128
Q_W = N_Q_HEADS * HEAD_DIM
KV_W = N_KV_HEADS * HEAD_DIM
GMLP_W = GROUPS * GROUP_DIM
D_FF = 5632
IN_W = Q_W + 2 * KV_W + 2 * GMLP_W + 2 * D_MODEL
EPS = 1e-6
NEG = -1e30

N_PROMPT = BATCH * SEQ
N_SAMPLE = DEC_BATCH * DEC_SEQ
N_TOK = N_PROMPT + N_SAMPLE

Z_W = 2 * D_MODEL + Q_W + 2 * GMLP_W
Z_GA, Z_GB, Z_Q, Z_U, Z_GV = 0, 2048, 4096, 5120, 6144

MIB = 1024 * 1024


def _rms(x, g):
    return x * lax.rsqrt(jnp.mean(x * x, axis=-1, keepdims=True) + EPS) * g


def _dot(a, b):
    return jnp.dot(a, b, preferred_element_type=F32)


def _dot_t(a, b):
    return lax.dot_general(a, b, (((1,), (1,)), ((), ())), preferred_element_type=F32)


IP_TM = 1024
IP_TN = 512
IP_CH = 256
IP_NPT = N_PROMPT // IP_TM
IP_NT = N_TOK // IP_TM
IP_NJ = 8


def _inproj_wa(j):
    return jnp.where(j == 0, 0, jnp.where(j == 1, 2, 2 * j - 1))


def _inproj_wb(j):
    return jnp.where(j == 0, 1, jnp.where(j == 1, 4, 2 * j))


def _inproj_zcol(j):
    return jnp.where(j < 2, Z_Q // (2 * IP_TN), jnp.where(j < 4, j + 3, j - 4))


def _inproj_kernel(xp_ref, xs_ref, g_ref, wa_ref, wb_ref, z_ref, kv_ref, h_ref):
    i = pl.program_id(0)
    j = pl.program_id(1)
    nch = IP_TM // IP_CH

    def first_step(x_ref):
        wa = wa_ref[...].astype(BF16)
        wb = wb_ref[...].astype(BF16)
        for c in range(nch):
            rows = slice(c * IP_CH, (c + 1) * IP_CH)
            h = _rms(x_ref[rows, :], g_ref[...]).astype(BF16)
            h_ref[rows, :] = h
            z_ref[rows, 0:IP_TN] = _dot(h, wa).astype(BF16)
            z_ref[rows, IP_TN:2 * IP_TN] = _dot(h, wb).astype(BF16)

    def pair_step(act):
        wa = wa_ref[...].astype(BF16)
        wb = wb_ref[...].astype(BF16)
        for c in range(nch):
            rows = slice(c * IP_CH, (c + 1) * IP_CH)
            h = h_ref[rows, :]
            z_ref[rows, 0:IP_TN] = act(_dot(h, wa)).astype(BF16)
            z_ref[rows, IP_TN:2 * IP_TN] = act(_dot(h, wb)).astype(BF16)

    @pl.when((j == 0) & (i < IP_NPT))
    def _():
        first_step(xp_ref)

    @pl.when((j == 0) & (i >= IP_NPT))
    def _():
        first_step(xs_ref)

    @pl.when(j == 1)
    def _():
        wa = wa_ref[...].astype(BF16)
        for c in range(nch):
            rows = slice(c * IP_CH, (c + 1) * IP_CH)
            kv_ref[rows, :] = _dot(h_ref[rows, :], wa)

    @pl.when((j >= 2) & (j < 4))
    def _():
        pair_step(jax.nn.gelu)

    @pl.when(j >= 4)
    def _():
        pair_step(jax.nn.sigmoid)


def _inproj(xp, xs, g, w):
    return pl.pallas_call(
        _inproj_kernel,
        grid=(IP_NT, IP_NJ),
        in_specs=[
            pl.BlockSpec((IP_TM, D_MODEL), lambda i, j: (jnp.minimum(i, IP_NPT - 1), 0)),
            pl.BlockSpec((IP_TM, D_MODEL), lambda i, j: (jnp.maximum(i - IP_NPT, 0), 0),
                         pipeline_mode=pl.Buffered(1)),
            pl.BlockSpec((1, D_MODEL), lambda i, j: (0, 0)),
            pl.BlockSpec((D_MODEL, IP_TN), lambda i, j: (0, _inproj_wa(j))),
            pl.BlockSpec((D_MODEL, IP_TN), lambda i, j: (0, _inproj_wb(j))),
        ],
        out_specs=[
            pl.BlockSpec((IP_TM, 2 * IP_TN), lambda i, j: (i, _inproj_zcol(j))),
            pl.BlockSpec((IP_TM, 2 * KV_W), lambda i, j: (i, 0)),
        ],
        out_shape=[
            jax.ShapeDtypeStruct((N_TOK, Z_W), BF16),
            jax.ShapeDtypeStruct((N_TOK, 2 * KV_W), F32),
        ],
        scratch_shapes=[pltpu.VMEM((IP_TM, D_MODEL), BF16)],
        compiler_params=pltpu.CompilerParams(
            dimension_semantics=("arbitrary", "arbitrary"), vmem_limit_bytes=60 * MIB),
        name="inproj",
    )(xp, xs, g, w, w)


NBLK = SEQ // WINDOW


PA_QB = 4
PA_ROWS = PA_QB * WINDOW
PA_STEPS_PER_SEQ = NBLK // PA_QB
GRP_W = Q_PER_KV * HEAD_DIM


def _lane_group_rotations(x):
    x1 = pltpu.roll(x, HEAD_DIM, 1)
    return [x.astype(BF16), x1.astype(BF16),
            pltpu.roll(x, 2 * HEAD_DIM, 1).astype(BF16), pltpu.roll(x1, 2 * HEAD_DIM, 1).astype(BF16)]


def _cast_blocks(src_refs, dst_refs):
    for src, dst in zip(src_refs, dst_refs, strict=True):
        dst[...] = src[...].astype(BF16)


def _cast_specs(weights, nblocks):
    specs = [pl.BlockSpec((w.shape[0] // nblocks, w.shape[1]), lambda t: (jnp.minimum(t, nblocks - 1), 0))
             for w in weights]
    shapes = [jax.ShapeDtypeStruct(w.shape, BF16) for w in weights]
    return specs, shapes


def _attn_prompt_kernel(sink_ref, q_ref, kc_ref, kp_ref, *refs):
    ncast = (len(refs) - 3) // 2
    o_ref, pk_ref, pv_ref = refs[ncast:ncast + 3]
    _cast_blocks(refs[:ncast], refs[ncast + 3:])
    t = pl.program_id(0)
    kv = jnp.concatenate([kp_ref[...], kc_ref[...]], axis=0)
    k_rot = _lane_group_rotations(kv[:, :KV_W])
    v_rot = _lane_group_rotations(kv[:, KV_W:])
    lane_grp = lax.broadcasted_iota(jnp.int32, (WINDOW, GRP_W), 1) >> 6
    in_grp = [lane_grp == g for g in range(Q_PER_KV)]
    nrow = N_KV_HEADS * WINDOW
    key = lax.broadcasted_iota(jnp.int32, (2 * WINDOW, nrow), 0)
    qry = lax.broadcasted_iota(jnp.int32, (2 * WINDOW, nrow), 1) & (WINDOW - 1)
    band = (key >= qry) & (key <= qry + WINDOW)
    bias_inner = jnp.where(band, 0.0, NEG)
    bias_first = jnp.where(band & (key >= WINDOW), 0.0, NEG)

    for qb in range(PA_QB):
        is_first = ((t * PA_QB + qb) % NBLK) == 0
        bias = jnp.where(is_first, bias_first, bias_inner)
        q = q_ref[qb * WINDOW:(qb + 1) * WINDOW, :] * (HEAD_DIM ** -0.5)
        keys = slice(qb * WINDOW, (qb + 2) * WINDOW)
        outs = [[None] * Q_PER_KV for _ in range(N_KV_HEADS)]
        for rot in range(Q_PER_KV):
            lhs = jnp.concatenate(
                [jnp.where(in_grp[(h + rot) % Q_PER_KV], q[:, h * GRP_W:(h + 1) * GRP_W], 0)
                 for h in range(N_KV_HEADS)], axis=0)
            sink = jnp.concatenate(
                [jnp.full((1, WINDOW), sink_ref[h * Q_PER_KV + (h + rot) % Q_PER_KV], F32)
                 for h in range(N_KV_HEADS)], axis=1)
            s = _dot_t(k_rot[rot][keys], lhs) + bias
            m = jnp.maximum(jnp.max(s, axis=0, keepdims=True), sink)
            p = jnp.exp(s - m)
            denom = jnp.sum(p, axis=0, keepdims=True) + jnp.exp(sink - m)
            probs = (p / denom).astype(BF16)
            o = lax.dot_general(probs, v_rot[rot][keys], (((0,), (0,)), ((), ())),
                                preferred_element_type=F32)
            for h in range(N_KV_HEADS):
                outs[h][(h + rot) % Q_PER_KV] = o[h * WINDOW:(h + 1) * WINDOW]
        for h in range(N_KV_HEADS):
            oh = outs[h][Q_PER_KV - 1]
            for g in range(Q_PER_KV - 2, -1, -1):
                oh = jnp.where(in_grp[g], outs[h][g], oh)
            o_ref[qb * WINDOW:(qb + 1) * WINDOW, h * GRP_W:(h + 1) * GRP_W] = oh.astype(BF16)

    @pl.when(t % PA_STEPS_PER_SEQ == PA_STEPS_PER_SEQ - 1)
    def _():
        pk_ref[0] = kc_ref[PA_ROWS - WINDOW:PA_ROWS, :KV_W]
        pv_ref[0] = kc_ref[PA_ROWS - WINDOW:PA_ROWS, KV_W:]


def _attn_prompt(sinks, z, kv, cast_weights):
    nsteps = N_PROMPT // PA_ROWS
    cast_specs, cast_shapes = _cast_specs(cast_weights, nsteps)
    return pl.pallas_call(
        _attn_prompt_kernel,
        grid=(nsteps,),
        in_specs=[
            pl.BlockSpec(memory_space=pltpu.SMEM),
            pl.BlockSpec((PA_ROWS, Q_W), lambda t: (t, Z_Q // Q_W)),
            pl.BlockSpec((PA_ROWS, 2 * KV_W), lambda t: (t, 0)),
            pl.BlockSpec((WINDOW, 2 * KV_W), lambda t: (jnp.maximum(t * PA_QB - 1, 0), 0)),
        ] + cast_specs,
        out_specs=[
            pl.BlockSpec((PA_ROWS, Q_W), lambda t: (t, 0)),
            pl.BlockSpec((1, WINDOW, KV_W), lambda t: (t // PA_STEPS_PER_SEQ, 0, 0)),
            pl.BlockSpec((1, WINDOW, KV_W), lambda t: (t // PA_STEPS_PER_SEQ, 0, 0)),
        ] + cast_specs,
        out_shape=[
            jax.ShapeDtypeStruct((N_PROMPT, Q_W), BF16),
            jax.ShapeDtypeStruct((BATCH, WINDOW, KV_W), F32),
            jax.ShapeDtypeStruct((BATCH, WINDOW, KV_W), F32),
        ] + cast_shapes,
        compiler_params=pltpu.CompilerParams(
            dimension_semantics=("arbitrary",), vmem_limit_bytes=48 * MIB),
        name="attn_prompt",
    )(sinks, z, kv, kv, *cast_weights)


SA_BT = 16
SA_ROWS = N_Q_HEADS * DEC_SEQ
SA_STEPS = DEC_BATCH // SA_BT
SA_GROUP = 16


def _attn_sample_kernel(sink_ref, q_ref, ck_ref, cv_ref, kvn_ref, *refs):
    ncast = (len(refs) - 3) // 2
    o_ref, kw_ref, vw_ref = refs[ncast:ncast + 3]
    _cast_blocks(refs[:ncast], refs[ncast + 3:])
    kvn = kvn_ref[...]
    k_new = kvn[:, :KV_W].astype(BF16)
    v_new = kvn[:, KV_W:].astype(BF16)
    head_of_lane = lax.broadcasted_iota(jnp.int32, (1, SA_ROWS), 1) >> 3
    sink_seq = jnp.zeros((1, SA_ROWS), F32)
    for hr in range(N_Q_HEADS):
        sink_seq = jnp.where(head_of_lane == hr, sink_ref[hr], sink_seq)

    def group(b0, nb):
        ntok = nb * DEC_SEQ
        nrow = nb * SA_ROWS
        tok = slice(b0 * DEC_SEQ, b0 * DEC_SEQ + ntok)
        lane_grp = lax.broadcasted_iota(jnp.int32, (ntok, GRP_W), 1) >> 6
        in_grp = [lane_grp == g for g in range(Q_PER_KV)]
        qf = q_ref[tok, :].astype(F32) * (HEAD_DIM ** -0.5)
        q_exp = []
        for h in range(N_KV_HEADS):
            q_h = qf[:, h * GRP_W:(h + 1) * GRP_W]
            for r in range(Q_PER_KV):
                shift = ((h - r) % Q_PER_KV) * HEAD_DIM
                moved = q_h if shift == 0 else pltpu.roll(q_h, shift, 1)
                q_exp.append(jnp.where(in_grp[h], moved, 0.0))
        lhs = jnp.concatenate([q_exp[hr][b * DEC_SEQ:(b + 1) * DEC_SEQ]
                               for b in range(nb) for hr in range(N_Q_HEADS)], axis=0).astype(BF16)
        seq_rows = lambda b: slice(b * SA_ROWS, (b + 1) * SA_ROWS)
        s_c = jnp.concatenate(
            [_dot_t(ck_ref[b0 + b].astype(BF16), lhs[seq_rows(b)]) for b in range(nb)], axis=1)
        s_n = _dot_t(k_new, lhs)

        key = lax.broadcasted_iota(jnp.int32, (WINDOW, nrow), 0)
        qry = lax.broadcasted_iota(jnp.int32, (WINDOW, nrow), 1)
        tq = qry & (DEC_SEQ - 1)
        s_c = jnp.where(key >= tq, s_c, NEG)
        s_n = jnp.where(((key >> 3) == (qry >> 7) + b0) & ((key & (DEC_SEQ - 1)) <= tq), s_n, NEG)
        sink = jnp.concatenate([sink_seq] * nb, axis=1)
        m = jnp.maximum(jnp.maximum(jnp.max(s_c, axis=0, keepdims=True),
                                    jnp.max(s_n, axis=0, keepdims=True)), sink)
        p_c = jnp.exp(s_c - m)
        p_n = jnp.exp(s_n - m)
        denom = (jnp.sum(p_c, axis=0, keepdims=True) + jnp.sum(p_n, axis=0, keepdims=True)
                 + jnp.exp(sink - m))
        p_c = (p_c / denom).astype(BF16)
        p_n = (p_n / denom).astype(BF16)
        contract0 = (((0,), (0,)), ((), ()))
        o = jnp.concatenate(
            [lax.dot_general(p_c[:, seq_rows(b)], cv_ref[b0 + b].astype(BF16), contract0,
                             preferred_element_type=F32) for b in range(nb)], axis=0)
        o = o + lax.dot_general(p_n, v_new, contract0, preferred_element_type=F32)

        for h in range(N_KV_HEADS):
            parts = []
            for r in range(Q_PER_KV):
                hr = h * Q_PER_KV + r
                o_hr = jnp.concatenate(
                    [o[b * SA_ROWS + hr * DEC_SEQ:b * SA_ROWS + (hr + 1) * DEC_SEQ] for b in range(nb)],
                    axis=0)
                shift = ((r - h) % Q_PER_KV) * HEAD_DIM
                parts.append(o_hr if shift == 0 else pltpu.roll(o_hr, shift, 1))
            o_h = parts[Q_PER_KV - 1]
            for g in range(Q_PER_KV - 2, -1, -1):
                o_h = jnp.where(in_grp[g], parts[g], o_h)
            o_ref[tok, h * GRP_W:(h + 1) * GRP_W] = o_h.astype(BF16)

    for g0 in range(0, SA_BT, SA_GROUP):
        group(g0, SA_GROUP)

    for b in range(SA_BT):
        kw_ref[b, 0:WINDOW - DEC_SEQ, :] = ck_ref[b, DEC_SEQ:WINDOW, :]
        kw_ref[b, WINDOW - DEC_SEQ:WINDOW, :] = kvn[b * DEC_SEQ:(b + 1) * DEC_SEQ, :KV_W]
        vw_ref[b, 0:WINDOW - DEC_SEQ, :] = cv_ref[b, DEC_SEQ:WINDOW, :]
        vw_ref[b, WINDOW - DEC_SEQ:WINDOW, :] = kvn[b * DEC_SEQ:(b + 1) * DEC_SEQ, KV_W:]


def _attn_sample(sinks, z, ck, cv, kv, cast_weights):
    step_tok = SA_BT * DEC_SEQ
    blk0 = N_PROMPT // step_tok
    cast_specs, cast_shapes = _cast_specs(cast_weights, SA_STEPS)
    return pl.pallas_call(
        _attn_sample_kernel,
        grid=(SA_STEPS,),
        in_specs=[
            pl.BlockSpec(memory_space=pltpu.SMEM),
            pl.BlockSpec((step_tok, Q_W), lambda s: (blk0 + s, Z_Q // Q_W)),
            pl.BlockSpec((SA_BT, WINDOW, KV_W), lambda s: (s, 0, 0)),
            pl.BlockSpec((SA_BT, WINDOW, KV_W), lambda s: (s, 0, 0)),
            pl.BlockSpec((step_tok, 2 * KV_W), lambda s: (blk0 + s, 0)),
        ] + cast_specs,
        out_specs=[
            pl.BlockSpec((step_tok, Q_W), lambda s: (s, 0)),
            pl.BlockSpec((SA_BT, WINDOW, KV_W), lambda s: (s, 0, 0)),
            pl.BlockSpec((SA_BT, WINDOW, KV_W), lambda s: (s, 0, 0)),
        ] + cast_specs,
        out_shape=[
            jax.ShapeDtypeStruct((N_SAMPLE, Q_W), BF16),
            jax.ShapeDtypeStruct((DEC_BATCH, WINDOW, KV_W), F32),
            jax.ShapeDtypeStruct((DEC_BATCH, WINDOW, KV_W), F32),
        ] + cast_shapes,
        compiler_params=pltpu.CompilerParams(
            dimension_semantics=("arbitrary",), vmem_limit_bytes=48 * MIB),
        name="attn_sample",
    )(sinks, z, ck, cv, kv, *cast_weights)


def _gmlp_kernel(u_ref, gv_ref, lng_ref, w_ref, bias_ref, *refs, nchunk, sample, ncast):
    nout = 2 if sample else 1
    a_ref, *gvn_refs = refs[ncast:ncast + nout]
    _cast_blocks(refs[:ncast], refs[ncast + nout:])
    ri = lax.broadcasted_iota(jnp.int32, (CHUNK, CHUNK), 0)
    ci = lax.broadcasted_iota(jnp.int32, (CHUNK, CHUNK), 1)
    nseq = CHUNK // DEC_SEQ
    if sample:
        wmask = ((ri >> 3) == (ci >> 3)) & ((ri & (DEC_SEQ - 1)) >= (ci & (DEC_SEQ - 1)))
        corner_lanes = lax.broadcasted_iota(jnp.int32, (DEC_SEQ, CHUNK), 1) < DEC_SEQ

        def group_w(g):
            w8 = jnp.where(corner_lanes, w_ref[g, 0:DEC_SEQ, :], 0.0)
            shift = DEC_SEQ
            while shift < CHUNK:
                w8 = w8 + pltpu.roll(w8, shift, 1)
                shift *= 2
            return jnp.concatenate([w8] * nseq, axis=0)

        bias = jnp.concatenate([bias_ref[0:DEC_SEQ, :]] * nseq, axis=0)
    else:
        wmask = ri >= ci
        group_w = lambda g: w_ref[g]
        bias = bias_ref[...]
    w = [jnp.where(wmask, group_w(g), 0.0).astype(BF16) for g in range(GROUPS)]
    lng = lng_ref[...]
    for c in range(nchunk):
        rows = slice(c * CHUNK, (c + 1) * CHUNK)
        gv = gv_ref[rows, :].astype(F32)
        xc = gv - jnp.mean(gv, axis=-1, keepdims=True)
        y = xc * lax.rsqrt(jnp.mean(xc * xc, axis=-1, keepdims=True) + EPS) * lng
        if sample:
            gvn_refs[0][rows, :] = y
        yb = y.astype(BF16)
        for g in range(GROUPS):
            cols = slice(g * GROUP_DIM, (g + 1) * GROUP_DIM)
            mixed = _dot(w[g], yb[:, cols]) + bias[:, g:g + 1]
            a_ref[rows, cols] = (u_ref[rows, cols].astype(F32) * mixed).astype(BF16)


def _gmlp(z, lng, w, bias, *, sample, cast_weights=()):
    rows = CHUNK if sample else 8 * CHUNK
    nsteps = (N_SAMPLE if sample else N_PROMPT) // rows
    blk0 = (N_PROMPT // rows) if sample else 0
    n_out = N_SAMPLE if sample else N_PROMPT
    out_specs = [pl.BlockSpec((rows, GMLP_W), lambda s: (s, 0))]
    out_shape = [jax.ShapeDtypeStruct((n_out, GMLP_W), BF16)]
    if sample:
        out_specs.append(pl.BlockSpec((rows, GMLP_W), lambda s: (s, 0)))
        out_shape.append(jax.ShapeDtypeStruct((n_out, GMLP_W), F32))
    cast_specs, cast_shapes = _cast_specs(cast_weights, nsteps)
    return pl.pallas_call(
        functools.partial(_gmlp_kernel, nchunk=rows // CHUNK, sample=sample, ncast=len(cast_weights)),
        grid=(nsteps,),
        in_specs=[
            pl.BlockSpec((rows, GMLP_W), lambda s: (blk0 + s, Z_U // GMLP_W)),
            pl.BlockSpec((rows, GMLP_W), lambda s: (blk0 + s, Z_GV // GMLP_W)),
            pl.BlockSpec((1, GMLP_W), lambda s: (0, 0)),
            pl.BlockSpec((GROUPS, CHUNK, CHUNK), lambda s: (0, 0, 0)),
            pl.BlockSpec((CHUNK, GROUPS), lambda s: (0, 0)),
        ] + cast_specs,
        out_specs=out_specs + cast_specs,
        out_shape=out_shape + cast_shapes,
        compiler_params=pltpu.CompilerParams(
            dimension_semantics=("arbitrary",), vmem_limit_bytes=40 * MIB),
        name="gmlp_sample" if sample else "gmlp_prompt",
    )(z, z, lng, w, bias, *cast_weights)


MG_TM = 256
MG_NPT = N_PROMPT // MG_TM
MG_NT = N_TOK // MG_TM
MG_CAST_BLOCKS = 32


def _merge_kernel(xp_ref, xs_ref, ap_ref, as_ref, op_ref, os_ref, ga_ref, gb_ref,
                  wpa_ref, wpb_ref, wo_ref, *refs):
    ncast = (len(refs) - 1) // 2
    x1_ref = refs[ncast]
    i = pl.program_id(0)

    @pl.when(i < MG_CAST_BLOCKS)
    def _():
        _cast_blocks(refs[:ncast], refs[ncast + 1:])

    def body(x_ref, a_ref, o_ref):
        pa = _dot(a_ref[...], wpa_ref[...])
        pb = _dot(o_ref[...], wpb_ref[...])
        mix = ga_ref[...].astype(F32) * pa + gb_ref[...].astype(F32) * pb
        x1_ref[...] = x_ref[...] + _dot(mix.astype(BF16), wo_ref[...])

    @pl.when(i < MG_NPT)
    def _():
        body(xp_ref, ap_ref, op_ref)

    @pl.when(i >= MG_NPT)
    def _():
        body(xs_ref, as_ref, os_ref)


def _merge(xp, xs, a_p, a_s, o_p, o_s, z, wpa, wpb, wo, cast_weights):
    cast_specs, cast_shapes = _cast_specs(cast_weights, MG_CAST_BLOCKS)
    p_idx = lambda i: (jnp.minimum(i, MG_NPT - 1), 0)
    s_idx = lambda i: (jnp.maximum(i - MG_NPT, 0), 0)
    const = lambda i: (0, 0)
    resident = dict(pipeline_mode=pl.Buffered(1))
    return pl.pallas_call(
        _merge_kernel,
        grid=(MG_NT,),
        in_specs=[
            pl.BlockSpec((MG_TM, D_MODEL), p_idx),
            pl.BlockSpec((MG_TM, D_MODEL), s_idx),
            pl.BlockSpec((MG_TM, GMLP_W), p_idx),
            pl.BlockSpec((MG_TM, GMLP_W), s_idx),
            pl.BlockSpec((MG_TM, Q_W), p_idx),
            pl.BlockSpec((MG_TM, Q_W), s_idx),
            pl.BlockSpec((MG_TM, D_MODEL), lambda i: (i, Z_GA // D_MODEL)),
            pl.BlockSpec((MG_TM, D_MODEL), lambda i: (i, Z_GB // D_MODEL)),
            pl.BlockSpec((GMLP_W, D_MODEL), const, **resident),
            pl.BlockSpec((Q_W, D_MODEL), const, **resident),
            pl.BlockSpec((D_MODEL, D_MODEL), const, **resident),
        ] + cast_specs,
        out_specs=[pl.BlockSpec((MG_TM, D_MODEL), lambda i: (i, 0))] + cast_specs,
        out_shape=[jax.ShapeDtypeStruct((N_TOK, D_MODEL), F32)] + cast_shapes,
        compiler_params=pltpu.CompilerParams(
            dimension_semantics=("arbitrary",), vmem_limit_bytes=56 * MIB),
        name="merge",
    )(xp, xs, a_p, a_s, o_p, o_s, z, z, wpa, wpb, wo, *cast_weights)


FF_TM = 1024
FF_TF = 512
FF_NJ = D_FF // FF_TF


def _ffn_kernel(x1_ref, n2_ref, wg_ref, wu_ref, wd_ref, fg_ref, y_ref, hh_ref, act_ref):
    j = pl.program_id(1)

    def gate_up(slot):
        hh = hh_ref[...]
        act = jax.nn.silu(_dot(hh, wg_ref[...])) * _dot(hh, wu_ref[...])
        act_ref[slot] = act.astype(BF16)

    def down(slot, first, last):
        acc = (x1_ref if first else y_ref)[...] + _dot(act_ref[slot], wd_ref[...])
        y_ref[...] = _rms(acc, fg_ref[...]) if last else acc

    @pl.when(j == 0)
    def _():
        hh_ref[...] = _rms(x1_ref[...], n2_ref[...]).astype(BF16)
        gate_up(0)

    @pl.when(j == 1)
    def _():
        gate_up(1)
        down(0, True, False)

    @pl.when((j > 1) & (j < FF_NJ))
    def _():
        gate_up(j % 2)
        down(1 - j % 2, False, False)

    @pl.when(j == FF_NJ)
    def _():
        down((FF_NJ - 1) % 2, False, True)


def _ffn(x1, row0, nrows, n2, wg, wu, wd, fg, name):
    blk0 = row0 // FF_TM
    return pl.pallas_call(
        _ffn_kernel,
        grid=(nrows // FF_TM, FF_NJ + 1),
        in_specs=[
            pl.BlockSpec((FF_TM, D_MODEL), lambda i, j: (blk0 + i, 0)),
            pl.BlockSpec((1, D_MODEL), lambda i, j: (0, 0)),
            pl.BlockSpec((D_MODEL, FF_TF), lambda i, j: (0, jnp.minimum(j, FF_NJ - 1))),
            pl.BlockSpec((D_MODEL, FF_TF), lambda i, j: (0, jnp.minimum(j, FF_NJ - 1))),
            pl.BlockSpec((FF_TF, D_MODEL), lambda i, j: (jnp.maximum(j - 1, 0), 0)),
            pl.BlockSpec((1, D_MODEL), lambda i, j: (0, 0)),
        ],
        out_specs=pl.BlockSpec((FF_TM, D_MODEL), lambda i, j: (i, 0)),
        out_shape=jax.ShapeDtypeStruct((nrows, D_MODEL), F32),
        scratch_shapes=[pltpu.VMEM((FF_TM, D_MODEL), BF16), pltpu.VMEM((2, FF_TM, FF_TF), BF16)],
        compiler_params=pltpu.CompilerParams(
            dimension_semantics=("arbitrary", "arbitrary"), vmem_limit_bytes=58 * MIB),
        name=name,
    )(x1, n2, wg, wu, wd, fg)


def kernel(x_prompt, x_sample, cache_k, cache_v, norm1_g, w_in, gmlp_norm_g, w_s, b_s, sinks,
           w_pa, w_pb, w_o, norm2_g, w_ff_gate, w_ff_up, w_ff_down, final_g):
    xp = x_prompt.reshape(N_PROMPT, D_MODEL)
    xs = x_sample.reshape(N_SAMPLE, D_MODEL)

    z, kv = _inproj(xp, xs, norm1_g.reshape(1, D_MODEL), w_in.reshape(D_MODEL, IN_W))

    lng = gmlp_norm_g.reshape(1, GMLP_W)
    ws = w_s.reshape(GROUPS, CHUNK, CHUNK)
    bias = b_s.reshape(GROUPS, CHUNK).T
    (a_p,) = _gmlp(z, lng, ws, bias, sample=False)
    a_s, gvn_s = _gmlp(z, lng, ws, bias, sample=True)

    sinks1 = sinks.reshape(N_Q_HEADS)
    o_p, pk, pv, wpa, wpb, wo = _attn_prompt(
        sinks1, z, kv, (w_pa.reshape(GMLP_W, D_MODEL), w_pb.reshape(Q_W, D_MODEL),
                        w_o.reshape(D_MODEL, D_MODEL)))
    o_s, k_win, v_win = _attn_sample(
        sinks1, z, cache_k.reshape(DEC_BATCH, WINDOW, KV_W),
        cache_v.reshape(DEC_BATCH, WINDOW, KV_W), kv, ())

    x1, wg, wu, wd = _merge(
        xp, xs, a_p, a_s, o_p, o_s, z, wpa, wpb, wo,
        (w_ff_gate.reshape(D_MODEL, D_FF), w_ff_up.reshape(D_MODEL, D_FF), w_ff_down.reshape(D_FF, D_MODEL)))

    ffn_w = (norm2_g.reshape(1, D_MODEL), wg, wu, wd, final_g.reshape(1, D_MODEL))
    y_p = _ffn(x1, 0, N_PROMPT, *ffn_w, name="ffn_prompt")
    y_s = _ffn(x1, N_PROMPT, N_SAMPLE, *ffn_w, name="ffn_sample")

    return (y_p.reshape(BATCH, SEQ, D_MODEL),
            y_s.reshape(DEC_BATCH, DEC_SEQ, D_MODEL),
            pk.reshape(1, BATCH, WINDOW, N_KV_HEADS, HEAD_DIM),
            pv.reshape(1, BATCH, WINDOW, N_KV_HEADS, HEAD_DIM),
            k_win.reshape(1, DEC_BATCH, WINDOW, N_KV_HEADS, HEAD_DIM),
            v_win.reshape(1, DEC_BATCH, WINDOW, N_KV_HEADS, HEAD_DIM),
            gvn_s.reshape(1, DEC_BATCH, DEC_SEQ, GROUPS, GROUP_DIM))
```

```python
import functools

import jax
import jax.numpy as jnp
from jax import lax
from jax.experimental import pallas as pl
from jax.experimental.pallas import tpu as pltpu

F32 = jnp.float32
BF16 = jnp.bfloat16

D_MODEL = 2048
BATCH = 4
SEQ = 2048
DEC_BATCH = 128
DEC_SEQ = 8
HEAD_DIM = 64
N_Q_HEADS = 16
N_KV_HEADS = 4
Q_PER_KV = N_Q_HEADS // N_KV_HEADS
WINDOW = 128
CHUNK = 128
GROUPS = 8
GROUP_DIM = 128
Q_W = N_Q_HEADS * HEAD_DIM
KV_W = N_KV_HEADS * HEAD_DIM
GMLP_W = GROUPS * GROUP_DIM
D_FF = 5632
IN_W = Q_W + 2 * KV_W + 2 * GMLP_W + 2 * D_MODEL
EPS = 1e-6
NEG = -1e30

N_PROMPT = BATCH * SEQ
N_SAMPLE = DEC_BATCH * DEC_SEQ
N_TOK = N_PROMPT + N_SAMPLE

Z_W = 2 * D_MODEL + Q_W + 2 * GMLP_W
Z_GA, Z_GB, Z_Q, Z_U, Z_GV = 0, 2048, 4096, 5120, 6144

MIB = 1024 * 1024


def _rms(x, g):
    return x * lax.rsqrt(jnp.mean(x * x, axis=-1, keepdims=True) + EPS) * g


def _dot(a, b):
    return jnp.dot(a, b, preferred_element_type=F32)


def _dot_t(a, b):
    return lax.dot_general(a, b, (((1,), (1,)), ((), ())), preferred_element_type=F32)


IP_TM = 1024
IP_TN = 512
IP_CH = 256
IP_NPT = N_PROMPT // IP_TM
IP_NT = N_TOK // IP_TM
IP_NJ = 8


def _inproj_wa(j):
    return jnp.where(j == 0, 0, jnp.where(j == 1, 2, 2 * j - 1))


def _inproj_wb(j):
    return jnp.where(j == 0, 1, jnp.where(j == 1, 4, 2 * j))


def _inproj_zcol(j):
    return jnp.where(j < 2, Z_Q // (2 * IP_TN), jnp.where(j < 4, j + 3, j - 4))


def _inproj_kernel(xp_ref, xs_ref, g_ref, wa_ref, wb_ref, z_ref, kv_ref, h_ref):
    i = pl.program_id(0)
    j = pl.program_id(1)
    nch = IP_TM // IP_CH

    def first_step(x_ref):
        wa = wa_ref[...].astype(BF16)
        wb = wb_ref[...].astype(BF16)
        for c in range(nch):
            rows = slice(c * IP_CH, (c + 1) * IP_CH)
            h = _rms(x_ref[rows, :], g_ref[...]).astype(BF16)
            h_ref[rows, :] = h
            z_ref[rows, 0:IP_TN] = _dot(h, wa).astype(BF16)
            z_ref[rows, IP_TN:2 * IP_TN] = _dot(h, wb).astype(BF16)

    def pair_step(act):
        wa = wa_ref[...].astype(BF16)
        wb = wb_ref[...].astype(BF16)
        for c in range(nch):
            rows = slice(c * IP_CH, (c + 1) * IP_CH)
            h = h_ref[rows, :]
            z_ref[rows, 0:IP_TN] = act(_dot(h, wa)).astype(BF16)
            z_ref[rows, IP_TN:2 * IP_TN] = act(_dot(h, wb)).astype(BF16)

    @pl.when((j == 0) & (i < IP_NPT))
    def _():
        first_step(xp_ref)

    @pl.when((j == 0) & (i >= IP_NPT))
    def _():
        first_step(xs_ref)

    @pl.when(j == 1)
    def _():
        wa = wa_ref[...].astype(BF16)
        for c in range(nch):
            rows = slice(c * IP_CH, (c + 1) * IP_CH)
            kv_ref[rows, :] = _dot(h_ref[rows, :], wa)

    @pl.when((j >= 2) & (j < 4))
    def _():
        pair_step(jax.nn.gelu)

    @pl.when(j >= 4)
    def _():
        pair_step(jax.nn.sigmoid)


def _inproj(xp, xs, g, w):
    return pl.pallas_call(
        _inproj_kernel,
        grid=(IP_NT, IP_NJ),
        in_specs=[
            pl.BlockSpec((IP_TM, D_MODEL), lambda i, j: (jnp.minimum(i, IP_NPT - 1), 0)),
            pl.BlockSpec((IP_TM, D_MODEL), lambda i, j: (jnp.maximum(i - IP_NPT, 0), 0),
                         pipeline_mode=pl.Buffered(1)),
            pl.BlockSpec((1, D_MODEL), lambda i, j: (0, 0)),
            pl.BlockSpec((D_MODEL, IP_TN), lambda i, j: (0, _inproj_wa(j))),
            pl.BlockSpec((D_MODEL, IP_TN), lambda i, j: (0, _inproj_wb(j))),
        ],
        out_specs=[
            pl.BlockSpec((IP_TM, 2 * IP_TN), lambda i, j: (i, _inproj_zcol(j))),
            pl.BlockSpec((IP_TM, 2 * KV_W), lambda i, j: (i, 0)),
        ],
        out_shape=[
            jax.ShapeDtypeStruct((N_TOK, Z_W), BF16),
            jax.ShapeDtypeStruct((N_TOK, 2 * KV_W), F32),
        ],
        scratch_shapes=[pltpu.VMEM((IP_TM, D_MODEL), BF16)],
        compiler_params=pltpu.CompilerParams(
            dimension_semantics=("arbitrary", "arbitrary"), vmem_limit_bytes=60 * MIB),
        name="inproj",
    )(xp, xs, g, w, w)


NBLK = SEQ // WINDOW


PA_QB = 4
PA_ROWS = PA_QB * WINDOW
PA_STEPS_PER_SEQ = NBLK // PA_QB
GRP_W = Q_PER_KV * HEAD_DIM


def _lane_group_rotations(x):
    x1 = pltpu.roll(x, HEAD_DIM, 1)
    return [x.astype(BF16), x1.astype(BF16),
            pltpu.roll(x, 2 * HEAD_DIM, 1).astype(BF16), pltpu.roll(x1, 2 * HEAD_DIM, 1).astype(BF16)]


def _cast_blocks(src_refs, dst_refs):
    for src, dst in zip(src_refs, dst_refs, strict=True):
        dst[...] = src[...].astype(BF16)


def _cast_specs(weights, nblocks):
    specs = [pl.BlockSpec((w.shape[0] // nblocks, w.shape[1]), lambda t: (jnp.minimum(t, nblocks - 1), 0))
             for w in weights]
    shapes = [jax.ShapeDtypeStruct(w.shape, BF16) for w in weights]
    return specs, shapes


def _attn_prompt_kernel(sink_ref, q_ref, kc_ref, kp_ref, after_ref, *refs):
    ncast = (len(refs) - 3) // 2
    o_ref, pk_ref, pv_ref = refs[ncast:ncast + 3]
    _cast_blocks(refs[:ncast], refs[ncast + 3:])
    t = pl.program_id(0)
    kv = jnp.concatenate([kp_ref[...], kc_ref[...]], axis=0)
    k_rot = _lane_group_rotations(kv[:, :KV_W])
    v_rot = _lane_group_rotations(kv[:, KV_W:])
    lane_grp = lax.broadcasted_iota(jnp.int32, (WINDOW, GRP_W), 1) >> 6
    in_grp = [lane_grp == g for g in range(Q_PER_KV)]
    nrow = N_KV_HEADS * WINDOW
    key = lax.broadcasted_iota(jnp.int32, (2 * WINDOW, nrow), 0)
    qry = lax.broadcasted_iota(jnp.int32, (2 * WINDOW, nrow), 1) & (WINDOW - 1)
    band = (key >= qry) & (key <= qry + WINDOW)
    bias_inner = jnp.where(band, 0.0, NEG)
    bias_first = jnp.where(band & (key >= WINDOW), 0.0, NEG)

    for qb in range(PA_QB):
        is_first = ((t * PA_QB + qb) % NBLK) == 0
        bias = jnp.where(is_first, bias_first, bias_inner)
        q = q_ref[qb * WINDOW:(qb + 1) * WINDOW, :] * (HEAD_DIM ** -0.5)
        keys = slice(qb * WINDOW, (qb + 2) * WINDOW)
        outs = [[None] * Q_PER_KV for _ in range(N_KV_HEADS)]
        for rot in range(Q_PER_KV):
            lhs = jnp.concatenate(
                [jnp.where(in_grp[(h + rot) % Q_PER_KV], q[:, h * GRP_W:(h + 1) * GRP_W], 0)
                 for h in range(N_KV_HEADS)], axis=0)
            sink = jnp.concatenate(
                [jnp.full((1, WINDOW), sink_ref[h * Q_PER_KV + (h + rot) % Q_PER_KV], F32)
                 for h in range(N_KV_HEADS)], axis=1)
            s = _dot_t(k_rot[rot][keys], lhs) + bias
            m = jnp.maximum(jnp.max(s, axis=0, keepdims=True), sink)
            p = jnp.exp(s - m)
            denom = jnp.sum(p, axis=0, keepdims=True) + jnp.exp(sink - m)
            probs = (p / denom).astype(BF16)
            o = lax.dot_general(probs, v_rot[rot][keys], (((0,), (0,)), ((), ())),
                                preferred_element_type=F32)
            for h in range(N_KV_HEADS):
                outs[h][(h + rot) % Q_PER_KV] = o[h * WINDOW:(h + 1) * WINDOW]
        for h in range(N_KV_HEADS):
            oh = outs[h][Q_PER_KV - 1]
            for g in range(Q_PER_KV - 2, -1, -1):
                oh = jnp.where(in_grp[g], outs[h][g], oh)
            o_ref[qb * WINDOW:(qb + 1) * WINDOW, h * GRP_W:(h + 1) * GRP_W] = oh.astype(BF16)

    @pl.when(t % PA_STEPS_PER_SEQ == PA_STEPS_PER_SEQ - 1)
    def _():
        pk_ref[0] = kc_ref[PA_ROWS - WINDOW:PA_ROWS, :KV_W]
        pv_ref[0] = kc_ref[PA_ROWS - WINDOW:PA_ROWS, KV_W:]


def _attn_prompt(sinks, z, kv, after, cast_weights):
    nsteps = N_PROMPT // PA_ROWS
    cast_specs, cast_shapes = _cast_specs(cast_weights, nsteps)
    return pl.pallas_call(
        _attn_prompt_kernel,
        grid=(nsteps,),
        in_specs=[
            pl.BlockSpec(memory_space=pltpu.SMEM),
            pl.BlockSpec((PA_ROWS, Q_W), lambda t: (t, Z_Q // Q_W)),
            pl.BlockSpec((PA_ROWS, 2 * KV_W), lambda t: (t, 0)),
            pl.BlockSpec((WINDOW, 2 * KV_W), lambda t: (jnp.maximum(t * PA_QB - 1, 0), 0)),
            pl.BlockSpec(memory_space=pl.ANY),
        ] + cast_specs,
        out_specs=[
            pl.BlockSpec((PA_ROWS, Q_W), lambda t: (t, 0)),
            pl.BlockSpec((1, WINDOW, KV_W), lambda t: (t // PA_STEPS_PER_SEQ, 0, 0)),
            pl.BlockSpec((1, WINDOW, KV_W), lambda t: (t // PA_STEPS_PER_SEQ, 0, 0)),
        ] + cast_specs,
        out_shape=[
            jax.ShapeDtypeStruct((N_PROMPT, Q_W), BF16),
            jax.ShapeDtypeStruct((BATCH, WINDOW, KV_W), F32),
            jax.ShapeDtypeStruct((BATCH, WINDOW, KV_W), F32),
        ] + cast_shapes,
        compiler_params=pltpu.CompilerParams(
            dimension_semantics=("arbitrary",), vmem_limit_bytes=48 * MIB),
        name="attn_prompt",
    )(sinks, z, kv, kv, after, *cast_weights)


SA_BT = 16
SA_ROWS = N_Q_HEADS * DEC_SEQ
SA_STEPS = DEC_BATCH // SA_BT
SA_GROUP = 16


def _attn_sample_kernel(sink_ref, q_ref, ck_ref, cv_ref, kvn_ref, after_ref, *refs):
    ncast = (len(refs) - 3) // 2
    o_ref, kw_ref, vw_ref = refs[ncast:ncast + 3]
    _cast_blocks(refs[:ncast], refs[ncast + 3:])
    kvn = kvn_ref[...]
    k_new = kvn[:, :KV_W].astype(BF16)
    v_new = kvn[:, KV_W:].astype(BF16)
    head_of_lane = lax.broadcasted_iota(jnp.int32, (1, SA_ROWS), 1) >> 3
    sink_seq = jnp.zeros((1, SA_ROWS), F32)
    for hr in range(N_Q_HEADS):
        sink_seq = jnp.where(head_of_lane == hr, sink_ref[hr], sink_seq)

    def group(b0, nb):
        ntok = nb * DEC_SEQ
        nrow = nb * SA_ROWS
        tok = slice(b0 * DEC_SEQ, b0 * DEC_SEQ + ntok)
        lane_grp = lax.broadcasted_iota(jnp.int32, (ntok, GRP_W), 1) >> 6
        in_grp = [lane_grp == g for g in range(Q_PER_KV)]
        qf = q_ref[tok, :].astype(F32) * (HEAD_DIM ** -0.5)
        q_exp = []
        for h in range(N_KV_HEADS):
            q_h = qf[:, h * GRP_W:(h + 1) * GRP_W]
            for r in range(Q_PER_KV):
                shift = ((h - r) % Q_PER_KV) * HEAD_DIM
                moved = q_h if shift == 0 else pltpu.roll(q_h, shift, 1)
                q_exp.append(jnp.where(in_grp[h], moved, 0.0))
        lhs = jnp.concatenate([q_exp[hr][b * DEC_SEQ:(b + 1) * DEC_SEQ]
                               for b in range(nb) for hr in range(N_Q_HEADS)], axis=0).astype(BF16)
        seq_rows = lambda b: slice(b * SA_ROWS, (b + 1) * SA_ROWS)
        s_c = jnp.concatenate(
            [_dot_t(ck_ref[b0 + b].astype(BF16), lhs[seq_rows(b)]) for b in range(nb)], axis=1)
        s_n = _dot_t(k_new, lhs)

        key = lax.broadcasted_iota(jnp.int32, (WINDOW, nrow), 0)
        qry = lax.broadcasted_iota(jnp.int32, (WINDOW, nrow), 1)
        tq = qry & (DEC_SEQ - 1)
        s_c = jnp.where(key >= tq, s_c, NEG)
        s_n = jnp.where(((key >> 3) == (qry >> 7) + b0) & ((key & (DEC_SEQ - 1)) <= tq), s_n, NEG)
        sink = jnp.concatenate([sink_seq] * nb, axis=1)
        m = jnp.maximum(jnp.maximum(jnp.max(s_c, axis=0, keepdims=True),
                                    jnp.max(s_n, axis=0, keepdims=True)), sink)
        p_c = jnp.exp(s_c - m)
        p_n = jnp.exp(s_n - m)
        denom = (jnp.sum(p_c, axis=0, keepdims=True) + jnp.sum(p_n, axis=0, keepdims=True)
                 + jnp.exp(sink - m))
        p_c = (p_c / denom).astype(BF16)
        p_n = (p_n / denom).astype(BF16)
        contract0 = (((0,), (0,)), ((), ()))
        o = jnp.concatenate(
            [lax.dot_general(p_c[:, seq_rows(b)], cv_ref[b0 + b].astype(BF16), contract0,
                             preferred_element_type=F32) for b in range(nb)], axis=0)
        o = o + lax.dot_general(p_n, v_new, contract0, preferred_element_type=F32)

        for h in range(N_KV_HEADS):
            parts = []
            for r in range(Q_PER_KV):
                hr = h * Q_PER_KV + r
                o_hr = jnp.concatenate(
                    [o[b * SA_ROWS + hr * DEC_SEQ:b * SA_ROWS + (hr + 1) * DEC_SEQ] for b in range(nb)],
                    axis=0)
                shift = ((r - h) % Q_PER_KV) * HEAD_DIM
                parts.append(o_hr if shift == 0 else pltpu.roll(o_hr, shift, 1))
            o_h = parts[Q_PER_KV - 1]
            for g in range(Q_PER_KV - 2, -1, -1):
                o_h = jnp.where(in_grp[g], parts[g], o_h)
            o_ref[tok, h * GRP_W:(h + 1) * GRP_W] = o_h.astype(BF16)

    for g0 in range(0, SA_BT, SA_GROUP):
        group(g0, SA_GROUP)

    for b in range(SA_BT):
        kw_ref[b, 0:WINDOW - DEC_SEQ, :] = ck_ref[b, DEC_SEQ:WINDOW, :]
        kw_ref[b, WINDOW - DEC_SEQ:WINDOW, :] = kvn[b * DEC_SEQ:(b + 1) * DEC_SEQ, :KV_W]
        vw_ref[b, 0:WINDOW - DEC_SEQ, :] = cv_ref[b, DEC_SEQ:WINDOW, :]
        vw_ref[b, WINDOW - DEC_SEQ:WINDOW, :] = kvn[b * DEC_SEQ:(b + 1) * DEC_SEQ, KV_W:]


def _attn_sample(sinks, z, ck, cv, kv, after, cast_weights):
    step_tok = SA_BT * DEC_SEQ
    blk0 = N_PROMPT // step_tok
    cast_specs, cast_shapes = _cast_specs(cast_weights, SA_STEPS)
    return pl.pallas_call(
        _attn_sample_kernel,
        grid=(SA_STEPS,),
        in_specs=[
            pl.BlockSpec(memory_space=pltpu.SMEM),
            pl.BlockSpec((step_tok, Q_W), lambda s: (blk0 + s, Z_Q // Q_W)),
            pl.BlockSpec((SA_BT, WINDOW, KV_W), lambda s: (s, 0, 0)),
            pl.BlockSpec((SA_BT, WINDOW, KV_W), lambda s: (s, 0, 0)),
            pl.BlockSpec((step_tok, 2 * KV_W), lambda s: (blk0 + s, 0)),
            pl.BlockSpec(memory_space=pl.ANY),
        ] + cast_specs,
        out_specs=[
            pl.BlockSpec((step_tok, Q_W), lambda s: (s, 0)),
            pl.BlockSpec((SA_BT, WINDOW, KV_W), lambda s: (s, 0, 0)),
            pl.BlockSpec((SA_BT, WINDOW, KV_W), lambda s: (s, 0, 0)),
        ] + cast_specs,
        out_shape=[
            jax.ShapeDtypeStruct((N_SAMPLE, Q_W), BF16),
            jax.ShapeDtypeStruct((DEC_BATCH, WINDOW, KV_W), F32),
            jax.ShapeDtypeStruct((DEC_BATCH, WINDOW, KV_W), F32),
        ] + cast_shapes,
        compiler_params=pltpu.CompilerParams(
            dimension_semantics=("arbitrary",), vmem_limit_bytes=48 * MIB),
        name="attn_sample",
    )(sinks, z, ck, cv, kv, after, *cast_weights)


def _gmlp_kernel(u_ref, gv_ref, lng_ref, w_ref, bias_ref, after_ref, *refs, nchunk, sample, ncast):
    nout = 2 if sample else 1
    a_ref, *gvn_refs = refs[ncast:ncast + nout]
    _cast_blocks(refs[:ncast], refs[ncast + nout:])
    ri = lax.broadcasted_iota(jnp.int32, (CHUNK, CHUNK), 0)
    ci = lax.broadcasted_iota(jnp.int32, (CHUNK, CHUNK), 1)
    nseq = CHUNK // DEC_SEQ
    if sample:
        wmask = ((ri >> 3) == (ci >> 3)) & ((ri & (DEC_SEQ - 1)) >= (ci & (DEC_SEQ - 1)))
        corner_lanes = lax.broadcasted_iota(jnp.int32, (DEC_SEQ, CHUNK), 1) < DEC_SEQ

        def group_w(g):
            w8 = jnp.where(corner_lanes, w_ref[g, 0:DEC_SEQ, :], 0.0)
            shift = DEC_SEQ
            while shift < CHUNK:
                w8 = w8 + pltpu.roll(w8, shift, 1)
                shift *= 2
            return jnp.concatenate([w8] * nseq, axis=0)

        bias = jnp.concatenate([bias_ref[0:DEC_SEQ, :]] * nseq, axis=0)
    else:
        wmask = ri >= ci
        group_w = lambda g: w_ref[g]
        bias = bias_ref[...]
    w = [jnp.where(wmask, group_w(g), 0.0).astype(BF16) for g in range(GROUPS)]
    lng = lng_ref[...]
    for c in range(nchunk):
        rows = slice(c * CHUNK, (c + 1) * CHUNK)
        gv = gv_ref[rows, :].astype(F32)
        xc = gv - jnp.mean(gv, axis=-1, keepdims=True)
        y = xc * lax.rsqrt(jnp.mean(xc * xc, axis=-1, keepdims=True) + EPS) * lng
        if sample:
            gvn_refs[0][rows, :] = y
        yb = y.astype(BF16)
        for g in range(GROUPS):
            cols = slice(g * GROUP_DIM, (g + 1) * GROUP_DIM)
            mixed = _dot(w[g], yb[:, cols]) + bias[:, g:g + 1]
            a_ref[rows, cols] = (u_ref[rows, cols].astype(F32) * mixed).astype(BF16)


def _gmlp(z, lng, w, bias, after, *, sample, cast_weights=()):
    rows = CHUNK if sample else 8 * CHUNK
    nsteps = (N_SAMPLE if sample else N_PROMPT) // rows
    blk0 = (N_PROMPT // rows) if sample else 0
    n_out = N_SAMPLE if sample else N_PROMPT
    out_specs = [pl.BlockSpec((rows, GMLP_W), lambda s: (s, 0))]
    out_shape = [jax.ShapeDtypeStruct((n_out, GMLP_W), BF16)]
    if sample:
        out_specs.append(pl.BlockSpec((rows, GMLP_W), lambda s: (s, 0)))
        out_shape.append(jax.ShapeDtypeStruct((n_out, GMLP_W), F32))
    cast_specs, cast_shapes = _cast_specs(cast_weights, nsteps)
    return pl.pallas_call(
        functools.partial(_gmlp_kernel, nchunk=rows // CHUNK, sample=sample, ncast=len(cast_weights)),
        grid=(nsteps,),
        in_specs=[
            pl.BlockSpec((rows, GMLP_W), lambda s: (blk0 + s, Z_U // GMLP_W)),
            pl.BlockSpec((rows, GMLP_W), lambda s: (blk0 + s, Z_GV // GMLP_W)),
            pl.BlockSpec((1, GMLP_W), lambda s: (0, 0)),
            pl.BlockSpec((GROUPS, CHUNK, CHUNK), lambda s: (0, 0, 0)),
            pl.BlockSpec((CHUNK, GROUPS), lambda s: (0, 0)),
            pl.BlockSpec(memory_space=pl.ANY),
        ] + cast_specs,
        out_specs=out_specs + cast_specs,
        out_shape=out_shape + cast_shapes,
        compiler_params=pltpu.CompilerParams(
            dimension_semantics=("arbitrary",), vmem_limit_bytes=40 * MIB),
        name="gmlp_sample" if sample else "gmlp_prompt",
    )(z, z, lng, w, bias, after, *cast_weights)


MG_TM = 256
MG_NPT = N_PROMPT // MG_TM
MG_NT = N_TOK // MG_TM
MG_CAST_BLOCKS = 32


def _merge_kernel(xp_ref, xs_ref, ap_ref, as_ref, op_ref, os_ref, ga_ref, gb_ref,
                  wpa_ref, wpb_ref, wo_ref, *refs):
    ncast = (len(refs) - 1) // 2
    x1_ref = refs[ncast]
    i = pl.program_id(0)

    @pl.when(i < MG_CAST_BLOCKS)
    def _():
        _cast_blocks(refs[:ncast], refs[ncast + 1:])

    def body(x_ref, a_ref, o_ref):
        pa = _dot(a_ref[...], wpa_ref[...])
        pb = _dot(o_ref[...], wpb_ref[...])
        mix = ga_ref[...].astype(F32) * pa + gb_ref[...].astype(F32) * pb
        x1_ref[...] = x_ref[...] + _dot(mix.astype(BF16), wo_ref[...])

    @pl.when(i < MG_NPT)
    def _():
        body(xp_ref, ap_ref, op_ref)

    @pl.when(i >= MG_NPT)
    def _():
        body(xs_ref, as_ref, os_ref)


def _merge(xp, xs, a_p, a_s, o_p, o_s, z, wpa, wpb, wo, cast_weights):
    cast_specs, cast_shapes = _cast_specs(cast_weights, MG_CAST_BLOCKS)
    p_idx = lambda i: (jnp.minimum(i, MG_NPT - 1), 0)
    s_idx = lambda i: (jnp.maximum(i - MG_NPT, 0), 0)
    const = lambda i: (0, 0)
    resident = dict(pipeline_mode=pl.Buffered(1))
    return pl.pallas_call(
        _merge_kernel,
        grid=(MG_NT,),
        in_specs=[
            pl.BlockSpec((MG_TM, D_MODEL), p_idx),
            pl.BlockSpec((MG_TM, D_MODEL), s_idx),
            pl.BlockSpec((MG_TM, GMLP_W), p_idx),
            pl.BlockSpec((MG_TM, GMLP_W), s_idx),
            pl.BlockSpec((MG_TM, Q_W), p_idx),
            pl.BlockSpec((MG_TM, Q_W), s_idx),
            pl.BlockSpec((MG_TM, D_MODEL), lambda i: (i, Z_GA // D_MODEL)),
            pl.BlockSpec((MG_TM, D_MODEL), lambda i: (i, Z_GB // D_MODEL)),
            pl.BlockSpec((GMLP_W, D_MODEL), const, **resident),
            pl.BlockSpec((Q_W, D_MODEL), const, **resident),
            pl.BlockSpec((D_MODEL, D_MODEL), const, **resident),
        ] + cast_specs,
        out_specs=[pl.BlockSpec((MG_TM, D_MODEL), lambda i: (i, 0))] + cast_specs,
        out_shape=[jax.ShapeDtypeStruct((N_TOK, D_MODEL), F32)] + cast_shapes,
        compiler_params=pltpu.CompilerParams(
            dimension_semantics=("arbitrary",), vmem_limit_bytes=56 * MIB),
        name="merge",
    )(xp, xs, a_p, a_s, o_p, o_s, z, z, wpa, wpb, wo, *cast_weights)


FF_TM = 1024
FF_TF = 512
FF_NJ = D_FF // FF_TF


def _ffn_kernel(x1_ref, n2_ref, wg_ref, wu_ref, wd_ref, fg_ref, y_ref, hh_ref, act_ref):
    j = pl.program_id(1)

    def gate_up(slot):
        hh = hh_ref[...]
        act = jax.nn.silu(_dot(hh, wg_ref[...])) * _dot(hh, wu_ref[...])
        act_ref[slot] = act.astype(BF16)

    def down(slot, first, last):
        acc = (x1_ref if first else y_ref)[...] + _dot(act_ref[slot], wd_ref[...])
        y_ref[...] = _rms(acc, fg_ref[...]) if last else acc

    @pl.when(j == 0)
    def _():
        hh_ref[...] = _rms(x1_ref[...], n2_ref[...]).astype(BF16)
        gate_up(0)

    @pl.when(j == 1)
    def _():
        gate_up(1)
        down(0, True, False)

    @pl.when((j > 1) & (j < FF_NJ))
    def _():
        gate_up(j % 2)
        down(1 - j % 2, False, False)

    @pl.when(j == FF_NJ)
    def _():
        down((FF_NJ - 1) % 2, False, True)


def _ffn(x1, row0, nrows, n2, wg, wu, wd, fg, name):
    blk0 = row0 // FF_TM
    return pl.pallas_call(
        _ffn_kernel,
        grid=(nrows // FF_TM, FF_NJ + 1),
        in_specs=[
            pl.BlockSpec((FF_TM, D_MODEL), lambda i, j: (blk0 + i, 0)),
            pl.BlockSpec((1, D_MODEL), lambda i, j: (0, 0)),
            pl.BlockSpec((D_MODEL, FF_TF), lambda i, j: (0, jnp.minimum(j, FF_NJ - 1))),
            pl.BlockSpec((D_MODEL, FF_TF), lambda i, j: (0, jnp.minimum(j, FF_NJ - 1))),
            pl.BlockSpec((FF_TF, D_MODEL), lambda i, j: (jnp.maximum(j - 1, 0), 0)),
            pl.BlockSpec((1, D_MODEL), lambda i, j: (0, 0)),
        ],
        out_specs=pl.BlockSpec((FF_TM, D_MODEL), lambda i, j: (i, 0)),
        out_shape=jax.ShapeDtypeStruct((nrows, D_MODEL), F32),
        scratch_shapes=[pltpu.VMEM((FF_TM, D_MODEL), BF16), pltpu.VMEM((2, FF_TM, FF_TF), BF16)],
        compiler_params=pltpu.CompilerParams(
            dimension_semantics=("arbitrary", "arbitrary"), vmem_limit_bytes=58 * MIB),
        name=name,
    )(x1, n2, wg, wu, wd, fg)


def kernel(x_prompt, x_sample, cache_k, cache_v, norm1_g, w_in, gmlp_norm_g, w_s, b_s, sinks,
           w_pa, w_pb, w_o, norm2_g, w_ff_gate, w_ff_up, w_ff_down, final_g):
    xp = x_prompt.reshape(N_PROMPT, D_MODEL)
    xs = x_sample.reshape(N_SAMPLE, D_MODEL)

    z, kv = _inproj(xp, xs, norm1_g.reshape(1, D_MODEL), w_in.reshape(D_MODEL, IN_W))

    lng = gmlp_norm_g.reshape(1, GMLP_W)
    ws = w_s.reshape(GROUPS, CHUNK, CHUNK)
    bias = b_s.reshape(GROUPS, CHUNK).T
    (a_p,) = _gmlp(z, lng, ws, bias, kv, sample=False)
    a_s, gvn_s = _gmlp(z, lng, ws, bias, a_p, sample=True)

    sinks1 = sinks.reshape(N_Q_HEADS)
    o_p, pk, pv, wpa, wpb, wo = _attn_prompt(
        sinks1, z, kv, a_s, (w_pa.reshape(GMLP_W, D_MODEL), w_pb.reshape(Q_W, D_MODEL),
                        w_o.reshape(D_MODEL, D_MODEL)))
    o_s, k_win, v_win = _attn_sample(
        sinks1, z, cache_k.reshape(DEC_BATCH, WINDOW, KV_W),
        cache_v.reshape(DEC_BATCH, WINDOW, KV_W), kv, o_p, ())

    x1, wg, wu, wd = _merge(
        xp, xs, a_p, a_s, o_p, o_s, z, wpa, wpb, wo,
        (w_ff_gate.reshape(D_MODEL, D_FF), w_ff_up.reshape(D_MODEL, D_FF), w_ff_down.reshape(D_FF, D_MODEL)))

    ffn_w = (norm2_g.reshape(1, D_MODEL), wg, wu, wd, final_g.reshape(1, D_MODEL))
    y_p = _ffn(x1, 0, N_PROMPT, *ffn_w, name="ffn_prompt")
    y_s = _ffn(x1, N_PROMPT, N_SAMPLE, *ffn_w, name="ffn_sample")

    return (y_p.reshape(BATCH, SEQ, D_MODEL),
            y_s.reshape(DEC_BATCH, DEC_SEQ, D_MODEL),
            pk.reshape(1, BATCH, WINDOW, N_KV_HEADS, HEAD_DIM),
            pv.reshape(1, BATCH, WINDOW, N_KV_HEADS, HEAD_DIM),
            k_win.reshape(1, DEC_BATCH, WINDOW, N_KV_HEADS, HEAD_DIM),
            v_win.reshape(1, DEC_BATCH, WINDOW, N_KV_HEADS, HEAD_DIM),
            gvn_s.reshape(1, DEC_BATCH, DEC_SEQ, GROUPS, GROUP_DIM))
```

```python
import functools

import jax
import jax.numpy as jnp
from jax import lax
from jax.experimental import pallas as pl
from jax.experimental.pallas import tpu as pltpu

F32 = jnp.float32
BF16 = jnp.bfloat16

D_MODEL = 2048
BATCH = 4
SEQ = 2048
DEC_BATCH = 128
DEC_SEQ = 8
HEAD_DIM = 64
N_Q_HEADS = 16
N_KV_HEADS = 4
Q_PER_KV = N_Q_HEADS // N_KV_HEADS
WINDOW = 128
CHUNK = 128
GROUPS = 8
GROUP_DIM = 128
Q_W = N_Q_HEADS * HEAD_DIM
KV_W = N_KV_HEADS * HEAD_DIM
GMLP_W = GROUPS * GROUP_DIM
D_FF = 5632
IN_W = Q_W + 2 * KV_W + 2 * GMLP_W + 2 * D_MODEL
EPS = 1e-6
NEG = -1e30

N_PROMPT = BATCH * SEQ
N_SAMPLE = DEC_BATCH * DEC_SEQ
N_TOK = N_PROMPT + N_SAMPLE

Z_W = 2 * D_MODEL + Q_W + 2 * GMLP_W
Z_GA, Z_GB, Z_Q, Z_U, Z_GV = 0, 2048, 4096, 5120, 6144

MIB = 1024 * 1024


def _rms(x, g):
    return x * lax.rsqrt(jnp.mean(x * x, axis=-1, keepdims=True) + EPS) * g


def _dot(a, b):
    return jnp.dot(a, b, preferred_element_type=F32)


def _dot_t(a, b):
    return lax.dot_general(a, b, (((1,), (1,)), ((), ())), preferred_element_type=F32)


IP_TM = 1024
IP_TN = 512
IP_CH = 256
IP_NPT = N_PROMPT // IP_TM
IP_NT = N_TOK // IP_TM
IP_NJ = 8


def _inproj_wa(j):
    return jnp.where(j == 0, 0, jnp.where(j == 1, 2, 2 * j - 1))


def _inproj_wb(j):
    return jnp.where(j == 0, 1, jnp.where(j == 1, 4, 2 * j))


def _inproj_zcol(j):
    return jnp.where(j < 2, Z_Q // (2 * IP_TN), jnp.where(j < 4, j + 3, j - 4))


def _inproj_kernel(xp_ref, xs_ref, g_ref, wa_ref, wb_ref, z_ref, kv_ref, h_ref):
    i = pl.program_id(0)
    j = pl.program_id(1)
    nch = IP_TM // IP_CH

    def first_step(x_ref):
        wa = wa_ref[...].astype(BF16)
        wb = wb_ref[...].astype(BF16)
        for c in range(nch):
            rows = slice(c * IP_CH, (c + 1) * IP_CH)
            h = _rms(x_ref[rows, :], g_ref[...]).astype(BF16)
            h_ref[rows, :] = h
            z_ref[rows, 0:IP_TN] = _dot(h, wa).astype(BF16)
            z_ref[rows, IP_TN:2 * IP_TN] = _dot(h, wb).astype(BF16)

    def pair_step(act):
        wa = wa_ref[...].astype(BF16)
        wb = wb_ref[...].astype(BF16)
        for c in range(nch):
            rows = slice(c * IP_CH, (c + 1) * IP_CH)
            h = h_ref[rows, :]
            z_ref[rows, 0:IP_TN] = act(_dot(h, wa)).astype(BF16)
            z_ref[rows, IP_TN:2 * IP_TN] = act(_dot(h, wb)).astype(BF16)

    @pl.when((j == 0) & (i < IP_NPT))
    def _():
        first_step(xp_ref)

    @pl.when((j == 0) & (i >= IP_NPT))
    def _():
        first_step(xs_ref)

    @pl.when(j == 1)
    def _():
        wa = wa_ref[...].astype(BF16)
        for c in range(nch):
            rows = slice(c * IP_CH, (c + 1) * IP_CH)
            kv_ref[rows, :] = _dot(h_ref[rows, :], wa)

    @pl.when((j >= 2) & (j < 4))
    def _():
        pair_step(jax.nn.gelu)

    @pl.when(j >= 4)
    def _():
        pair_step(jax.nn.sigmoid)


def _inproj(xp, xs, g, w):
    return pl.pallas_call(
        _inproj_kernel,
        grid=(IP_NT, IP_NJ),
        in_specs=[
            pl.BlockSpec((IP_TM, D_MODEL),
                         lambda i, j: (jnp.minimum(i + jnp.minimum(j, 1), IP_NPT - 1), 0)),
            pl.BlockSpec((IP_TM, D_MODEL), lambda i, j: (jnp.maximum(i - IP_NPT, 0), 0),
                         pipeline_mode=pl.Buffered(1)),
            pl.BlockSpec((1, D_MODEL), lambda i, j: (0, 0)),
            pl.BlockSpec((D_MODEL, IP_TN), lambda i, j: (0, _inproj_wa(j))),
            pl.BlockSpec((D_MODEL, IP_TN), lambda i, j: (0, _inproj_wb(j))),
        ],
        out_specs=[
            pl.BlockSpec((IP_TM, 2 * IP_TN), lambda i, j: (i, _inproj_zcol(j))),
            pl.BlockSpec((IP_TM, 2 * KV_W), lambda i, j: (i, 0)),
        ],
        out_shape=[
            jax.ShapeDtypeStruct((N_TOK, Z_W), BF16),
            jax.ShapeDtypeStruct((N_TOK, 2 * KV_W), F32),
        ],
        scratch_shapes=[pltpu.VMEM((IP_TM, D_MODEL), BF16)],
        compiler_params=pltpu.CompilerParams(
            dimension_semantics=("arbitrary", "arbitrary"), vmem_limit_bytes=60 * MIB),
        name="inproj",
    )(xp, xs, g, w, w)


NBLK = SEQ // WINDOW


PA_QB = 4
PA_ROWS = PA_QB * WINDOW
PA_STEPS_PER_SEQ = NBLK // PA_QB
GRP_W = Q_PER_KV * HEAD_DIM


def _lane_group_rotations(x):
    x1 = pltpu.roll(x, HEAD_DIM, 1)
    return [x.astype(BF16), x1.astype(BF16),
            pltpu.roll(x, 2 * HEAD_DIM, 1).astype(BF16), pltpu.roll(x1, 2 * HEAD_DIM, 1).astype(BF16)]


def _cast_blocks(src_refs, dst_refs):
    for src, dst in zip(src_refs, dst_refs, strict=True):
        dst[...] = src[...].astype(BF16)


def _cast_specs(weights, nblocks):
    specs = [pl.BlockSpec((w.shape[0] // nblocks, w.shape[1]), lambda t: (jnp.minimum(t, nblocks - 1), 0))
             for w in weights]
    shapes = [jax.ShapeDtypeStruct(w.shape, BF16) for w in weights]
    return specs, shapes


def _attn_prompt_kernel(sink_ref, q_ref, kc_ref, kp_ref, after_ref, *refs):
    ncast = (len(refs) - 3) // 2
    o_ref, pk_ref, pv_ref = refs[ncast:ncast + 3]
    _cast_blocks(refs[:ncast], refs[ncast + 3:])
    t = pl.program_id(0)
    kv = jnp.concatenate([kp_ref[...], kc_ref[...]], axis=0)
    k_rot = _lane_group_rotations(kv[:, :KV_W])
    v_rot = _lane_group_rotations(kv[:, KV_W:])
    lane_grp = lax.broadcasted_iota(jnp.int32, (WINDOW, GRP_W), 1) >> 6
    in_grp = [lane_grp == g for g in range(Q_PER_KV)]
    nrow = N_KV_HEADS * WINDOW
    key = lax.broadcasted_iota(jnp.int32, (2 * WINDOW, nrow), 0)
    qry = lax.broadcasted_iota(jnp.int32, (2 * WINDOW, nrow), 1) & (WINDOW - 1)
    band = (key >= qry) & (key <= qry + WINDOW)
    bias_inner = jnp.where(band, 0.0, NEG)
    bias_first = jnp.where(band & (key >= WINDOW), 0.0, NEG)

    for qb in range(PA_QB):
        is_first = ((t * PA_QB + qb) % NBLK) == 0
        bias = jnp.where(is_first, bias_first, bias_inner)
        q = q_ref[qb * WINDOW:(qb + 1) * WINDOW, :] * (HEAD_DIM ** -0.5)
        keys = slice(qb * WINDOW, (qb + 2) * WINDOW)
        outs = [[None] * Q_PER_KV for _ in range(N_KV_HEADS)]
        for rot in range(Q_PER_KV):
            lhs = jnp.concatenate(
                [jnp.where(in_grp[(h + rot) % Q_PER_KV], q[:, h * GRP_W:(h + 1) * GRP_W], 0)
                 for h in range(N_KV_HEADS)], axis=0)
            sink = jnp.concatenate(
                [jnp.full((1, WINDOW), sink_ref[h * Q_PER_KV + (h + rot) % Q_PER_KV], F32)
                 for h in range(N_KV_HEADS)], axis=1)
            s = _dot_t(k_rot[rot][keys], lhs) + bias
            m = jnp.maximum(jnp.max(s, axis=0, keepdims=True), sink)
            p = jnp.exp(s - m)
            denom = jnp.sum(p, axis=0, keepdims=True) + jnp.exp(sink - m)
            probs = (p / denom).astype(BF16)
            o = lax.dot_general(probs, v_rot[rot][keys], (((0,), (0,)), ((), ())),
                                preferred_element_type=F32)
            for h in range(N_KV_HEADS):
                outs[h][(h + rot) % Q_PER_KV] = o[h * WINDOW:(h + 1) * WINDOW]
        for h in range(N_KV_HEADS):
            oh = outs[h][Q_PER_KV - 1]
            for g in range(Q_PER_KV - 2, -1, -1):
                oh = jnp.where(in_grp[g], outs[h][g], oh)
            o_ref[qb * WINDOW:(qb + 1) * WINDOW, h * GRP_W:(h + 1) * GRP_W] = oh.astype(BF16)

    @pl.when(t % PA_STEPS_PER_SEQ == PA_STEPS_PER_SEQ - 1)
    def _():
        pk_ref[0] = kc_ref[PA_ROWS - WINDOW:PA_ROWS, :KV_W]
        pv_ref[0] = kc_ref[PA_ROWS - WINDOW:PA_ROWS, KV_W:]


def _attn_prompt(sinks, z, kv, after, cast_weights):
    nsteps = N_PROMPT // PA_ROWS
    cast_specs, cast_shapes = _cast_specs(cast_weights, nsteps)
    return pl.pallas_call(
        _attn_prompt_kernel,
        grid=(nsteps,),
        in_specs=[
            pl.BlockSpec(memory_space=pltpu.SMEM),
            pl.BlockSpec((PA_ROWS, Q_W), lambda t: (t, Z_Q // Q_W)),
            pl.BlockSpec((PA_ROWS, 2 * KV_W), lambda t: (t, 0)),
            pl.BlockSpec((WINDOW, 2 * KV_W), lambda t: (jnp.maximum(t * PA_QB - 1, 0), 0)),
            pl.BlockSpec(memory_space=pl.ANY),
        ] + cast_specs,
        out_specs=[
            pl.BlockSpec((PA_ROWS, Q_W), lambda t: (t, 0)),
            pl.BlockSpec((1, WINDOW, KV_W), lambda t: (t // PA_STEPS_PER_SEQ, 0, 0)),
            pl.BlockSpec((1, WINDOW, KV_W), lambda t: (t // PA_STEPS_PER_SEQ, 0, 0)),
        ] + cast_specs,
        out_shape=[
            jax.ShapeDtypeStruct((N_PROMPT, Q_W), BF16),
            jax.ShapeDtypeStruct((BATCH, WINDOW, KV_W), F32),
            jax.ShapeDtypeStruct((BATCH, WINDOW, KV_W), F32),
        ] + cast_shapes,
        compiler_params=pltpu.CompilerParams(
            dimension_semantics=("arbitrary",), vmem_limit_bytes=48 * MIB),
        name="attn_prompt",
    )(sinks, z, kv, kv, after, *cast_weights)


SA_BT = 16
SA_ROWS = N_Q_HEADS * DEC_SEQ
SA_STEPS = DEC_BATCH // SA_BT
SA_GROUP = 16


def _attn_sample_kernel(sink_ref, q_ref, ck_ref, cv_ref, kvn_ref, after_ref, *refs):
    ncast = (len(refs) - 3) // 2
    o_ref, kw_ref, vw_ref = refs[ncast:ncast + 3]
    _cast_blocks(refs[:ncast], refs[ncast + 3:])
    kvn = kvn_ref[...]
    k_new = kvn[:, :KV_W].astype(BF16)
    v_new = kvn[:, KV_W:].astype(BF16)
    head_of_lane = lax.broadcasted_iota(jnp.int32, (1, SA_ROWS), 1) >> 3
    sink_seq = jnp.zeros((1, SA_ROWS), F32)
    for hr in range(N_Q_HEADS):
        sink_seq = jnp.where(head_of_lane == hr, sink_ref[hr], sink_seq)

    def group(b0, nb):
        ntok = nb * DEC_SEQ
        nrow = nb * SA_ROWS
        tok = slice(b0 * DEC_SEQ, b0 * DEC_SEQ + ntok)
        lane_grp = lax.broadcasted_iota(jnp.int32, (ntok, GRP_W), 1) >> 6
        in_grp = [lane_grp == g for g in range(Q_PER_KV)]
        qf = q_ref[tok, :].astype(F32) * (HEAD_DIM ** -0.5)
        q_exp = []
        for h in range(N_KV_HEADS):
            q_h = qf[:, h * GRP_W:(h + 1) * GRP_W]
            for r in range(Q_PER_KV):
                shift = ((h - r) % Q_PER_KV) * HEAD_DIM
                moved = q_h if shift == 0 else pltpu.roll(q_h, shift, 1)
                q_exp.append(jnp.where(in_grp[h], moved, 0.0))
        lhs = jnp.concatenate([q_exp[hr][b * DEC_SEQ:(b + 1) * DEC_SEQ]
                               for b in range(nb) for hr in range(N_Q_HEADS)], axis=0).astype(BF16)
        seq_rows = lambda b: slice(b * SA_ROWS, (b + 1) * SA_ROWS)
        s_c = jnp.concatenate(
            [_dot_t(ck_ref[b0 + b].astype(BF16), lhs[seq_rows(b)]) for b in range(nb)], axis=1)
        s_n = _dot_t(k_new, lhs)

        key = lax.broadcasted_iota(jnp.int32, (WINDOW, nrow), 0)
        qry = lax.broadcasted_iota(jnp.int32, (WINDOW, nrow), 1)
        tq = qry & (DEC_SEQ - 1)
        s_c = jnp.where(key >= tq, s_c, NEG)
        s_n = jnp.where(((key >> 3) == (qry >> 7) + b0) & ((key & (DEC_SEQ - 1)) <= tq), s_n, NEG)
        sink = jnp.concatenate([sink_seq] * nb, axis=1)
        m = jnp.maximum(jnp.maximum(jnp.max(s_c, axis=0, keepdims=True),
                                    jnp.max(s_n, axis=0, keepdims=True)), sink)
        p_c = jnp.exp(s_c - m)
        p_n = jnp.exp(s_n - m)
        denom = (jnp.sum(p_c, axis=0, keepdims=True) + jnp.sum(p_n, axis=0, keepdims=True)
                 + jnp.exp(sink - m))
        p_c = (p_c / denom).astype(BF16)
        p_n = (p_n / denom).astype(BF16)
        contract0 = (((0,), (0,)), ((), ()))
        o = jnp.concatenate(
            [lax.dot_general(p_c[:, seq_rows(b)], cv_ref[b0 + b].astype(BF16), contract0,
                             preferred_element_type=F32) for b in range(nb)], axis=0)
        o = o + lax.dot_general(p_n, v_new, contract0, preferred_element_type=F32)

        for h in range(N_KV_HEADS):
            parts = []
            for r in range(Q_PER_KV):
                hr = h * Q_PER_KV + r
                o_hr = jnp.concatenate(
                    [o[b * SA_ROWS + hr * DEC_SEQ:b * SA_ROWS + (hr + 1) * DEC_SEQ] for b in range(nb)],
                    axis=0)
                shift = ((r - h) % Q_PER_KV) * HEAD_DIM
                parts.append(o_hr if shift == 0 else pltpu.roll(o_hr, shift, 1))
            o_h = parts[Q_PER_KV - 1]
            for g in range(Q_PER_KV - 2, -1, -1):
                o_h = jnp.where(in_grp[g], parts[g], o_h)
            o_ref[tok, h * GRP_W:(h + 1) * GRP_W] = o_h.astype(BF16)

    for g0 in range(0, SA_BT, SA_GROUP):
        group(g0, SA_GROUP)

    for b in range(SA_BT):
        kw_ref[b, 0:WINDOW - DEC_SEQ, :] = ck_ref[b, DEC_SEQ:WINDOW, :]
        kw_ref[b, WINDOW - DEC_SEQ:WINDOW, :] = kvn[b * DEC_SEQ:(b + 1) * DEC_SEQ, :KV_W]
        vw_ref[b, 0:WINDOW - DEC_SEQ, :] = cv_ref[b, DEC_SEQ:WINDOW, :]
        vw_ref[b, WINDOW - DEC_SEQ:WINDOW, :] = kvn[b * DEC_SEQ:(b + 1) * DEC_SEQ, KV_W:]


def _attn_sample(sinks, z, ck, cv, kv, after, cast_weights):
    step_tok = SA_BT * DEC_SEQ
    blk0 = N_PROMPT // step_tok
    cast_specs, cast_shapes = _cast_specs(cast_weights, SA_STEPS)
    return pl.pallas_call(
        _attn_sample_kernel,
        grid=(SA_STEPS,),
        in_specs=[
            pl.BlockSpec(memory_space=pltpu.SMEM),
            pl.BlockSpec((step_tok, Q_W), lambda s: (blk0 + s, Z_Q // Q_W)),
            pl.BlockSpec((SA_BT, WINDOW, KV_W), lambda s: (s, 0, 0)),
            pl.BlockSpec((SA_BT, WINDOW, KV_W), lambda s: (s, 0, 0)),
            pl.BlockSpec((step_tok, 2 * KV_W), lambda s: (blk0 + s, 0)),
            pl.BlockSpec(memory_space=pl.ANY),
        ] + cast_specs,
        out_specs=[
            pl.BlockSpec((step_tok, Q_W), lambda s: (s, 0)),
            pl.BlockSpec((SA_BT, WINDOW, KV_W), lambda s: (s, 0, 0)),
            pl.BlockSpec((SA_BT, WINDOW, KV_W), lambda s: (s, 0, 0)),
        ] + cast_specs,
        out_shape=[
            jax.ShapeDtypeStruct((N_SAMPLE, Q_W), BF16),
            jax.ShapeDtypeStruct((DEC_BATCH, WINDOW, KV_W), F32),
            jax.ShapeDtypeStruct((DEC_BATCH, WINDOW, KV_W), F32),
        ] + cast_shapes,
        compiler_params=pltpu.CompilerParams(
            dimension_semantics=("arbitrary",), vmem_limit_bytes=48 * MIB),
        name="attn_sample",
    )(sinks, z, ck, cv, kv, after, *cast_weights)


def _gmlp_kernel(u_ref, gv_ref, lng_ref, w_ref, bias_ref, after_ref, *refs, nchunk, sample, ncast):
    nout = 2 if sample else 1
    a_ref, *gvn_refs = refs[ncast:ncast + nout]
    _cast_blocks(refs[:ncast], refs[ncast + nout:])
    ri = lax.broadcasted_iota(jnp.int32, (CHUNK, CHUNK), 0)
    ci = lax.broadcasted_iota(jnp.int32, (CHUNK, CHUNK), 1)
    nseq = CHUNK // DEC_SEQ
    if sample:
        wmask = ((ri >> 3) == (ci >> 3)) & ((ri & (DEC_SEQ - 1)) >= (ci & (DEC_SEQ - 1)))
        corner_lanes = lax.broadcasted_iota(jnp.int32, (DEC_SEQ, CHUNK), 1) < DEC_SEQ

        def group_w(g):
            w8 = jnp.where(corner_lanes, w_ref[g, 0:DEC_SEQ, :], 0.0)
            shift = DEC_SEQ
            while shift < CHUNK:
                w8 = w8 + pltpu.roll(w8, shift, 1)
                shift *= 2
            return jnp.concatenate([w8] * nseq, axis=0)

        bias = jnp.concatenate([bias_ref[0:DEC_SEQ, :]] * nseq, axis=0)
    else:
        wmask = ri >= ci
        group_w = lambda g: w_ref[g]
        bias = bias_ref[...]
    w = [jnp.where(wmask, group_w(g), 0.0).astype(BF16) for g in range(GROUPS)]
    lng = lng_ref[...]
    for c in range(nchunk):
        rows = slice(c * CHUNK, (c + 1) * CHUNK)
        gv = gv_ref[rows, :].astype(F32)
        xc = gv - jnp.mean(gv, axis=-1, keepdims=True)
        y = xc * lax.rsqrt(jnp.mean(xc * xc, axis=-1, keepdims=True) + EPS) * lng
        if sample:
            gvn_refs[0][rows, :] = y
        yb = y.astype(BF16)
        for g in range(GROUPS):
            cols = slice(g * GROUP_DIM, (g + 1) * GROUP_DIM)
            mixed = _dot(w[g], yb[:, cols]) + bias[:, g:g + 1]
            a_ref[rows, cols] = (u_ref[rows, cols].astype(F32) * mixed).astype(BF16)


def _gmlp(z, lng, w, bias, after, *, sample, cast_weights=()):
    rows = CHUNK if sample else 8 * CHUNK
    nsteps = (N_SAMPLE if sample else N_PROMPT) // rows
    blk0 = (N_PROMPT // rows) if sample else 0
    n_out = N_SAMPLE if sample else N_PROMPT
    out_specs = [pl.BlockSpec((rows, GMLP_W), lambda s: (s, 0))]
    out_shape = [jax.ShapeDtypeStruct((n_out, GMLP_W), BF16)]
    if sample:
        out_specs.append(pl.BlockSpec((rows, GMLP_W), lambda s: (s, 0)))
        out_shape.append(jax.ShapeDtypeStruct((n_out, GMLP_W), F32))
    cast_specs, cast_shapes = _cast_specs(cast_weights, nsteps)
    return pl.pallas_call(
        functools.partial(_gmlp_kernel, nchunk=rows // CHUNK, sample=sample, ncast=len(cast_weights)),
        grid=(nsteps,),
        in_specs=[
            pl.BlockSpec((rows, GMLP_W), lambda s: (blk0 + s, Z_U // GMLP_W)),
            pl.BlockSpec((rows, GMLP_W), lambda s: (blk0 + s, Z_GV // GMLP_W)),
            pl.BlockSpec((1, GMLP_W), lambda s: (0, 0)),
            pl.BlockSpec((GROUPS, CHUNK, CHUNK), lambda s: (0, 0, 0)),
            pl.BlockSpec((CHUNK, GROUPS), lambda s: (0, 0)),
            pl.BlockSpec(memory_space=pl.ANY),
        ] + cast_specs,
        out_specs=out_specs + cast_specs,
        out_shape=out_shape + cast_shapes,
        compiler_params=pltpu.CompilerParams(
            dimension_semantics=("arbitrary",), vmem_limit_bytes=40 * MIB),
        name="gmlp_sample" if sample else "gmlp_prompt",
    )(z, z, lng, w, bias, after, *cast_weights)


MG_TM = 256


def _merge_kernel(x_ref, a_ref, o_ref, ga_ref, gb_ref, wpa_ref, wpb_ref, wo_ref, *refs):
    ncast = (len(refs) - 1) // 2
    x1_ref = refs[ncast]
    _cast_blocks(refs[:ncast], refs[ncast + 1:])
    pa = _dot(a_ref[...], wpa_ref[...])
    pb = _dot(o_ref[...], wpb_ref[...])
    mix = ga_ref[...].astype(F32) * pa + gb_ref[...].astype(F32) * pb
    x1_ref[...] = x_ref[...] + _dot(mix.astype(BF16), wo_ref[...])


def _merge(x, a, o, z, zrow0, wpa, wpb, wo, cast_weights, name):
    nsteps = x.shape[0] // MG_TM
    zblk0 = zrow0 // MG_TM
    cast_specs, cast_shapes = _cast_specs(cast_weights, nsteps)
    row = lambda i: (i, 0)
    const = lambda i: (0, 0)
    resident = dict(pipeline_mode=pl.Buffered(1))
    return pl.pallas_call(
        _merge_kernel,
        grid=(nsteps,),
        in_specs=[
            pl.BlockSpec((MG_TM, D_MODEL), row),
            pl.BlockSpec((MG_TM, GMLP_W), row),
            pl.BlockSpec((MG_TM, Q_W), row),
            pl.BlockSpec((MG_TM, D_MODEL), lambda i: (zblk0 + i, Z_GA // D_MODEL)),
            pl.BlockSpec((MG_TM, D_MODEL), lambda i: (zblk0 + i, Z_GB // D_MODEL)),
            pl.BlockSpec((GMLP_W, D_MODEL), const, **resident),
            pl.BlockSpec((Q_W, D_MODEL), const, **resident),
            pl.BlockSpec((D_MODEL, D_MODEL), const, **resident),
        ] + cast_specs,
        out_specs=[pl.BlockSpec((MG_TM, D_MODEL), row)] + cast_specs,
        out_shape=[jax.ShapeDtypeStruct(x.shape, F32)] + cast_shapes,
        compiler_params=pltpu.CompilerParams(
            dimension_semantics=("arbitrary",), vmem_limit_bytes=56 * MIB),
        name=name,
    )(x, a, o, z, z, wpa, wpb, wo, *cast_weights)


FF_TM = 1024
FF_TF = 512
FF_NJ = D_FF // FF_TF


def _ffn_kernel(x1_ref, n2_ref, wg_ref, wu_ref, wd_ref, fg_ref, y_ref, hh_ref, act_ref):
    j = pl.program_id(1)

    def gate_up(slot):
        hh = hh_ref[...]
        act = jax.nn.silu(_dot(hh, wg_ref[...])) * _dot(hh, wu_ref[...])
        act_ref[slot] = act.astype(BF16)

    def down(slot, first, last):
        acc = (x1_ref if first else y_ref)[...] + _dot(act_ref[slot], wd_ref[...])
        y_ref[...] = _rms(acc, fg_ref[...]) if last else acc

    @pl.when(j == 0)
    def _():
        hh_ref[...] = _rms(x1_ref[...], n2_ref[...]).astype(BF16)
        gate_up(0)

    @pl.when(j == 1)
    def _():
        gate_up(1)
        down(0, True, False)

    @pl.when((j > 1) & (j < FF_NJ))
    def _():
        gate_up(j % 2)
        down(1 - j % 2, False, False)

    @pl.when(j == FF_NJ)
    def _():
        down((FF_NJ - 1) % 2, False, True)


def _ffn(x1, n2, wg, wu, wd, fg, name):
    nrows = x1.shape[0]
    return pl.pallas_call(
        _ffn_kernel,
        grid=(nrows // FF_TM, FF_NJ + 1),
        in_specs=[
            pl.BlockSpec((FF_TM, D_MODEL), lambda i, j: (i, 0)),
            pl.BlockSpec((1, D_MODEL), lambda i, j: (0, 0)),
            pl.BlockSpec((D_MODEL, FF_TF), lambda i, j: (0, jnp.minimum(j, FF_NJ - 1))),
            pl.BlockSpec((D_MODEL, FF_TF), lambda i, j: (0, jnp.minimum(j, FF_NJ - 1))),
            pl.BlockSpec((FF_TF, D_MODEL), lambda i, j: (jnp.maximum(j - 1, 0), 0)),
            pl.BlockSpec((1, D_MODEL), lambda i, j: (0, 0)),
        ],
        out_specs=pl.BlockSpec((FF_TM, D_MODEL), lambda i, j: (i, 0)),
        out_shape=jax.ShapeDtypeStruct((nrows, D_MODEL), F32),
        scratch_shapes=[pltpu.VMEM((FF_TM, D_MODEL), BF16), pltpu.VMEM((2, FF_TM, FF_TF), BF16)],
        compiler_params=pltpu.CompilerParams(
            dimension_semantics=("arbitrary", "arbitrary"), vmem_limit_bytes=58 * MIB),
        name=name,
    )(x1, n2, wg, wu, wd, fg)


def kernel(x_prompt, x_sample, cache_k, cache_v, norm1_g, w_in, gmlp_norm_g, w_s, b_s, sinks,
           w_pa, w_pb, w_o, norm2_g, w_ff_gate, w_ff_up, w_ff_down, final_g):
    xp = x_prompt.reshape(N_PROMPT, D_MODEL)
    xs = x_sample.reshape(N_SAMPLE, D_MODEL)

    z, kv = _inproj(xp, xs, norm1_g.reshape(1, D_MODEL), w_in.reshape(D_MODEL, IN_W))

    lng = gmlp_norm_g.reshape(1, GMLP_W)
    ws = w_s.reshape(GROUPS, CHUNK, CHUNK)
    bias = b_s.reshape(GROUPS, CHUNK).T
    (a_p,) = _gmlp(z, lng, ws, bias, kv, sample=False)
    a_s, gvn_s = _gmlp(z, lng, ws, bias, a_p, sample=True)

    sinks1 = sinks.reshape(N_Q_HEADS)
    o_p, pk, pv, wpa, wpb, wo = _attn_prompt(
        sinks1, z, kv, a_s, (w_pa.reshape(GMLP_W, D_MODEL), w_pb.reshape(Q_W, D_MODEL),
                             w_o.reshape(D_MODEL, D_MODEL)))

    x1_p, wg, wu, wd = _merge(
        xp, a_p, o_p, z, 0, wpa, wpb, wo,
        (w_ff_gate.reshape(D_MODEL, D_FF), w_ff_up.reshape(D_MODEL, D_FF), w_ff_down.reshape(D_FF, D_MODEL)),
        name="merge_prompt")

    o_s, k_win, v_win = _attn_sample(
        sinks1, z, cache_k.reshape(DEC_BATCH, WINDOW, KV_W),
        cache_v.reshape(DEC_BATCH, WINDOW, KV_W), kv, x1_p, ())
    (x1_s,) = _merge(xs, a_s, o_s, z, N_PROMPT, wpa, wpb, wo, (), name="merge_sample")

    ffn_w = (norm2_g.reshape(1, D_MODEL), wg, wu, wd, final_g.reshape(1, D_MODEL))
    y_p = _ffn(x1_p, *ffn_w, name="ffn_prompt")
    y_s = _ffn(x1_s, *ffn_w, name="ffn_sample")

    return (y_p.reshape(BATCH, SEQ, D_MODEL),
            y_s.reshape(DEC_BATCH, DEC_SEQ, D_MODEL),
            pk.reshape(1, BATCH, WINDOW, N_KV_HEADS, HEAD_DIM),
            pv.reshape(1, BATCH, WINDOW, N_KV_HEADS, HEAD_DIM),
            k_win.reshape(1, DEC_BATCH, WINDOW, N_KV_HEADS, HEAD_DIM),
            v_win.reshape(1, DEC_BATCH, WINDOW, N_KV_HEADS, HEAD_DIM),
            gvn_s.reshape(1, DEC_BATCH, DEC_SEQ, GROUPS, GROUP_DIM))
```

```python
import functools

import jax
import jax.numpy as jnp
from jax import lax
from jax.experimental import pallas as pl
from jax.experimental.pallas import tpu as pltpu

F32 = jnp.float32
BF16 = jnp.bfloat16

D_MODEL = 2048
BATCH = 4
SEQ = 2048
DEC_BATCH = 128
DEC_SEQ = 8
HEAD_DIM = 64
N_Q_HEADS = 16
N_KV_HEADS = 4
Q_PER_KV = N_Q_HEADS // N_KV_HEADS
WINDOW = 128
CHUNK = 128
GROUPS = 8
GROUP_DIM = 128
Q_W = N_Q_HEADS * HEAD_DIM
KV_W = N_KV_HEADS * HEAD_DIM
GMLP_W = GROUPS * GROUP_DIM
D_FF = 5632
IN_W = Q_W + 2 * KV_W + 2 * GMLP_W + 2 * D_MODEL
EPS = 1e-6
NEG = -1e30

N_PROMPT = BATCH * SEQ
N_SAMPLE = DEC_BATCH * DEC_SEQ
N_TOK = N_PROMPT + N_SAMPLE

Z_W = 2 * D_MODEL + Q_W + 2 * GMLP_W
Z_GA, Z_GB, Z_Q, Z_U, Z_GV = 0, 2048, 4096, 5120, 6144

MIB = 1024 * 1024


def _rms(x, g):
    return x * lax.rsqrt(jnp.mean(x * x, axis=-1, keepdims=True) + EPS) * g


def _dot(a, b):
    return jnp.dot(a, b, preferred_element_type=F32)


def _dot_t(a, b):
    return lax.dot_general(a, b, (((1,), (1,)), ((), ())), preferred_element_type=F32)


IP_TM = 1024
IP_TN = 512
IP_CH = 256
IP_NPT = N_PROMPT // IP_TM
IP_NT = N_TOK // IP_TM
IP_NJ = 8


def _inproj_wa(j):
    return jnp.where(j == 0, 0, jnp.where(j == 1, 2, 2 * j - 1))


def _inproj_wb(j):
    return jnp.where(j == 0, 1, jnp.where(j == 1, 4, 2 * j))


def _inproj_zcol(j):
    return jnp.where(j < 2, Z_Q // (2 * IP_TN), jnp.where(j < 4, j + 3, j - 4))


def _inproj_kernel(xp_ref, xs_ref, g_ref, wa_ref, wb_ref, z_ref, kv_ref, h_ref):
    i = pl.program_id(0)
    j = pl.program_id(1)
    nch = IP_TM // IP_CH

    def first_step(x_ref):
        wa = wa_ref[...].astype(BF16)
        wb = wb_ref[...].astype(BF16)
        for c in range(nch):
            rows = slice(c * IP_CH, (c + 1) * IP_CH)
            h = _rms(x_ref[rows, :], g_ref[...]).astype(BF16)
            h_ref[rows, :] = h
            z_ref[rows, 0:IP_TN] = _dot(h, wa).astype(BF16)
            z_ref[rows, IP_TN:2 * IP_TN] = _dot(h, wb).astype(BF16)

    def pair_step(act):
        wa = wa_ref[...].astype(BF16)
        wb = wb_ref[...].astype(BF16)
        for c in range(nch):
            rows = slice(c * IP_CH, (c + 1) * IP_CH)
            h = h_ref[rows, :]
            z_ref[rows, 0:IP_TN] = act(_dot(h, wa)).astype(BF16)
            z_ref[rows, IP_TN:2 * IP_TN] = act(_dot(h, wb)).astype(BF16)

    @pl.when((j == 0) & (i < IP_NPT))
    def _():
        first_step(xp_ref)

    @pl.when((j == 0) & (i >= IP_NPT))
    def _():
        first_step(xs_ref)

    @pl.when(j == 1)
    def _():
        wa = wa_ref[...].astype(BF16)
        for c in range(nch):
            rows = slice(c * IP_CH, (c + 1) * IP_CH)
            kv_ref[rows, :] = _dot(h_ref[rows, :], wa)

    @pl.when((j >= 2) & (j < 4))
    def _():
        pair_step(jax.nn.gelu)

    @pl.when(j >= 4)
    def _():
        pair_step(jax.nn.sigmoid)


def _inproj(xp, xs, g, w):
    return pl.pallas_call(
        _inproj_kernel,
        grid=(IP_NT, IP_NJ),
        in_specs=[
            pl.BlockSpec((IP_TM, D_MODEL),
                         lambda i, j: (jnp.minimum(i + jnp.minimum(j, 1), IP_NPT - 1), 0)),
            pl.BlockSpec((IP_TM, D_MODEL), lambda i, j: (jnp.maximum(i - IP_NPT, 0), 0),
                         pipeline_mode=pl.Buffered(1)),
            pl.BlockSpec((1, D_MODEL), lambda i, j: (0, 0)),
            pl.BlockSpec((D_MODEL, IP_TN), lambda i, j: (0, _inproj_wa(j))),
            pl.BlockSpec((D_MODEL, IP_TN), lambda i, j: (0, _inproj_wb(j))),
        ],
        out_specs=[
            pl.BlockSpec((IP_TM, 2 * IP_TN), lambda i, j: (i, _inproj_zcol(j))),
            pl.BlockSpec((IP_TM, 2 * KV_W), lambda i, j: (i, 0)),
        ],
        out_shape=[
            jax.ShapeDtypeStruct((N_TOK, Z_W), BF16),
            jax.ShapeDtypeStruct((N_TOK, 2 * KV_W), F32),
        ],
        scratch_shapes=[pltpu.VMEM((IP_TM, D_MODEL), BF16)],
        compiler_params=pltpu.CompilerParams(
            dimension_semantics=("arbitrary", "arbitrary"), vmem_limit_bytes=60 * MIB),
        name="inproj",
    )(xp, xs, g, w, w)


NBLK = SEQ // WINDOW


PA_QB = 4
PA_ROWS = PA_QB * WINDOW
PA_STEPS_PER_SEQ = NBLK // PA_QB
GRP_W = Q_PER_KV * HEAD_DIM


def _lane_group_rotations(x):
    x1 = pltpu.roll(x, HEAD_DIM, 1)
    return [x.astype(BF16), x1.astype(BF16),
            pltpu.roll(x, 2 * HEAD_DIM, 1).astype(BF16), pltpu.roll(x1, 2 * HEAD_DIM, 1).astype(BF16)]


def _cast_blocks(src_refs, dst_refs):
    for src, dst in zip(src_refs, dst_refs, strict=True):
        dst[...] = src[...].astype(BF16)


def _cast_specs(weights, nblocks):
    specs = [pl.BlockSpec((w.shape[0] // nblocks, w.shape[1]), lambda t: (jnp.minimum(t, nblocks - 1), 0))
             for w in weights]
    shapes = [jax.ShapeDtypeStruct(w.shape, BF16) for w in weights]
    return specs, shapes


def _attn_prompt_kernel(sink_ref, q_ref, kc_ref, kp_ref, after_ref, *refs):
    ncast = (len(refs) - 3) // 2
    o_ref, pk_ref, pv_ref = refs[ncast:ncast + 3]
    _cast_blocks(refs[:ncast], refs[ncast + 3:])
    t = pl.program_id(0)
    kv = jnp.concatenate([kp_ref[...], kc_ref[...]], axis=0)
    k_rot = _lane_group_rotations(kv[:, :KV_W])
    v_rot = _lane_group_rotations(kv[:, KV_W:])
    lane_grp = lax.broadcasted_iota(jnp.int32, (WINDOW, GRP_W), 1) >> 6
    in_grp = [lane_grp == g for g in range(Q_PER_KV)]
    nrow = N_KV_HEADS * WINDOW
    key = lax.broadcasted_iota(jnp.int32, (2 * WINDOW, nrow), 0)
    qry = lax.broadcasted_iota(jnp.int32, (2 * WINDOW, nrow), 1) & (WINDOW - 1)
    band = (key >= qry) & (key <= qry + WINDOW)
    bias_inner = jnp.where(band, 0.0, NEG)
    bias_first = jnp.where(band & (key >= WINDOW), 0.0, NEG)

    for qb in range(PA_QB):
        is_first = ((t * PA_QB + qb) % NBLK) == 0
        bias = jnp.where(is_first, bias_first, bias_inner)
        q = q_ref[qb * WINDOW:(qb + 1) * WINDOW, :] * (HEAD_DIM ** -0.5)
        keys = slice(qb * WINDOW, (qb + 2) * WINDOW)
        outs = [[None] * Q_PER_KV for _ in range(N_KV_HEADS)]
        for rot in range(Q_PER_KV):
            lhs = jnp.concatenate(
                [jnp.where(in_grp[(h + rot) % Q_PER_KV], q[:, h * GRP_W:(h + 1) * GRP_W], 0)
                 for h in range(N_KV_HEADS)], axis=0)
            sink = jnp.concatenate(
                [jnp.full((1, WINDOW), sink_ref[h * Q_PER_KV + (h + rot) % Q_PER_KV], F32)
                 for h in range(N_KV_HEADS)], axis=1)
            s = _dot_t(k_rot[rot][keys], lhs) + bias
            m = jnp.maximum(jnp.max(s, axis=0, keepdims=True), sink)
            p = jnp.exp(s - m)
            denom = jnp.sum(p, axis=0, keepdims=True) + jnp.exp(sink - m)
            probs = (p / denom).astype(BF16)
            o = lax.dot_general(probs, v_rot[rot][keys], (((0,), (0,)), ((), ())),
                                preferred_element_type=F32)
            for h in range(N_KV_HEADS):
                outs[h][(h + rot) % Q_PER_KV] = o[h * WINDOW:(h + 1) * WINDOW]
        for h in range(N_KV_HEADS):
            oh = outs[h][Q_PER_KV - 1]
            for g in range(Q_PER_KV - 2, -1, -1):
                oh = jnp.where(in_grp[g], outs[h][g], oh)
            o_ref[qb * WINDOW:(qb + 1) * WINDOW, h * GRP_W:(h + 1) * GRP_W] = oh.astype(BF16)

    @pl.when(t % PA_STEPS_PER_SEQ == PA_STEPS_PER_SEQ - 1)
    def _():
        pk_ref[0] = kc_ref[PA_ROWS - WINDOW:PA_ROWS, :KV_W]
        pv_ref[0] = kc_ref[PA_ROWS - WINDOW:PA_ROWS, KV_W:]


def _attn_prompt(sinks, z, kv, after, cast_weights):
    nsteps = N_PROMPT // PA_ROWS
    cast_specs, cast_shapes = _cast_specs(cast_weights, nsteps)
    return pl.pallas_call(
        _attn_prompt_kernel,
        grid=(nsteps,),
        in_specs=[
            pl.BlockSpec(memory_space=pltpu.SMEM),
            pl.BlockSpec((PA_ROWS, Q_W), lambda t: (t, Z_Q // Q_W)),
            pl.BlockSpec((PA_ROWS, 2 * KV_W), lambda t: (t, 0)),
            pl.BlockSpec((WINDOW, 2 * KV_W), lambda t: (jnp.maximum(t * PA_QB - 1, 0), 0)),
            pl.BlockSpec(memory_space=pl.ANY),
        ] + cast_specs,
        out_specs=[
            pl.BlockSpec((PA_ROWS, Q_W), lambda t: (t, 0)),
            pl.BlockSpec((1, WINDOW, KV_W), lambda t: (t // PA_STEPS_PER_SEQ, 0, 0)),
            pl.BlockSpec((1, WINDOW, KV_W), lambda t: (t // PA_STEPS_PER_SEQ, 0, 0)),
        ] + cast_specs,
        out_shape=[
            jax.ShapeDtypeStruct((N_PROMPT, Q_W), BF16),
            jax.ShapeDtypeStruct((BATCH, WINDOW, KV_W), F32),
            jax.ShapeDtypeStruct((BATCH, WINDOW, KV_W), F32),
        ] + cast_shapes,
        compiler_params=pltpu.CompilerParams(
            dimension_semantics=("arbitrary",), vmem_limit_bytes=48 * MIB),
        name="attn_prompt",
    )(sinks, z, kv, kv, after, *cast_weights)


SA_BT = 16
SA_ROWS = N_Q_HEADS * DEC_SEQ
SA_STEPS = DEC_BATCH // SA_BT
SA_GROUP = 16


def _attn_sample_kernel(sink_ref, q_ref, ckt_ref, cvt_ref, kvn_ref, after_ref, *refs):
    *refs, ck_ref, cv_ref = refs
    ncast = (len(refs) - 3) // 2
    o_ref, kwt_ref, vwt_ref = refs[ncast:ncast + 3]
    _cast_blocks(refs[:ncast], refs[ncast + 3:])
    for b in range(SA_BT):
        ck_ref[b] = ckt_ref[b].T
        cv_ref[b] = cvt_ref[b].T
    kvn = kvn_ref[...]
    k_new = kvn[:, :KV_W].astype(BF16)
    v_new = kvn[:, KV_W:].astype(BF16)
    head_of_lane = lax.broadcasted_iota(jnp.int32, (1, SA_ROWS), 1) >> 3
    sink_seq = jnp.zeros((1, SA_ROWS), F32)
    for hr in range(N_Q_HEADS):
        sink_seq = jnp.where(head_of_lane == hr, sink_ref[hr], sink_seq)

    def group(b0, nb):
        ntok = nb * DEC_SEQ
        nrow = nb * SA_ROWS
        tok = slice(b0 * DEC_SEQ, b0 * DEC_SEQ + ntok)
        lane_grp = lax.broadcasted_iota(jnp.int32, (ntok, GRP_W), 1) >> 6
        in_grp = [lane_grp == g for g in range(Q_PER_KV)]
        qf = q_ref[tok, :].astype(F32) * (HEAD_DIM ** -0.5)
        q_exp = []
        for h in range(N_KV_HEADS):
            q_h = qf[:, h * GRP_W:(h + 1) * GRP_W]
            for r in range(Q_PER_KV):
                shift = ((h - r) % Q_PER_KV) * HEAD_DIM
                moved = q_h if shift == 0 else pltpu.roll(q_h, shift, 1)
                q_exp.append(jnp.where(in_grp[h], moved, 0.0))
        lhs = jnp.concatenate([q_exp[hr][b * DEC_SEQ:(b + 1) * DEC_SEQ]
                               for b in range(nb) for hr in range(N_Q_HEADS)], axis=0).astype(BF16)
        seq_rows = lambda b: slice(b * SA_ROWS, (b + 1) * SA_ROWS)
        s_c = jnp.concatenate(
            [_dot_t(ck_ref[b0 + b].astype(BF16), lhs[seq_rows(b)]) for b in range(nb)], axis=1)
        s_n = _dot_t(k_new, lhs)

        key = lax.broadcasted_iota(jnp.int32, (WINDOW, nrow), 0)
        qry = lax.broadcasted_iota(jnp.int32, (WINDOW, nrow), 1)
        tq = qry & (DEC_SEQ - 1)
        s_c = jnp.where(key >= tq, s_c, NEG)
        s_n = jnp.where(((key >> 3) == (qry >> 7) + b0) & ((key & (DEC_SEQ - 1)) <= tq), s_n, NEG)
        sink = jnp.concatenate([sink_seq] * nb, axis=1)
        m = jnp.maximum(jnp.maximum(jnp.max(s_c, axis=0, keepdims=True),
                                    jnp.max(s_n, axis=0, keepdims=True)), sink)
        p_c = jnp.exp(s_c - m)
        p_n = jnp.exp(s_n - m)
        denom = (jnp.sum(p_c, axis=0, keepdims=True) + jnp.sum(p_n, axis=0, keepdims=True)
                 + jnp.exp(sink - m))
        p_c = (p_c / denom).astype(BF16)
        p_n = (p_n / denom).astype(BF16)
        contract0 = (((0,), (0,)), ((), ()))
        o = jnp.concatenate(
            [lax.dot_general(p_c[:, seq_rows(b)], cv_ref[b0 + b].astype(BF16), contract0,
                             preferred_element_type=F32) for b in range(nb)], axis=0)
        o = o + lax.dot_general(p_n, v_new, contract0, preferred_element_type=F32)

        for h in range(N_KV_HEADS):
            parts = []
            for r in range(Q_PER_KV):
                hr = h * Q_PER_KV + r
                o_hr = jnp.concatenate(
                    [o[b * SA_ROWS + hr * DEC_SEQ:b * SA_ROWS + (hr + 1) * DEC_SEQ] for b in range(nb)],
                    axis=0)
                shift = ((r - h) % Q_PER_KV) * HEAD_DIM
                parts.append(o_hr if shift == 0 else pltpu.roll(o_hr, shift, 1))
            o_h = parts[Q_PER_KV - 1]
            for g in range(Q_PER_KV - 2, -1, -1):
                o_h = jnp.where(in_grp[g], parts[g], o_h)
            o_ref[tok, h * GRP_W:(h + 1) * GRP_W] = o_h.astype(BF16)

    for g0 in range(0, SA_BT, SA_GROUP):
        group(g0, SA_GROUP)

    for b in range(SA_BT):
        new = kvn[b * DEC_SEQ:(b + 1) * DEC_SEQ, :]
        kwt_ref[b] = jnp.concatenate([ck_ref[b, DEC_SEQ:WINDOW, :], new[:, :KV_W]], axis=0).T
        vwt_ref[b] = jnp.concatenate([cv_ref[b, DEC_SEQ:WINDOW, :], new[:, KV_W:]], axis=0).T


def _attn_sample(sinks, z, ckt, cvt, kv, after, cast_weights):
    step_tok = SA_BT * DEC_SEQ
    blk0 = N_PROMPT // step_tok
    cast_specs, cast_shapes = _cast_specs(cast_weights, SA_STEPS)
    cache_spec = pl.BlockSpec((SA_BT, KV_W, WINDOW), lambda s: (s, 0, 0))
    return pl.pallas_call(
        _attn_sample_kernel,
        grid=(SA_STEPS,),
        in_specs=[
            pl.BlockSpec(memory_space=pltpu.SMEM),
            pl.BlockSpec((step_tok, Q_W), lambda s: (blk0 + s, Z_Q // Q_W)),
            cache_spec,
            cache_spec,
            pl.BlockSpec((step_tok, 2 * KV_W), lambda s: (blk0 + s, 0)),
            pl.BlockSpec(memory_space=pl.ANY),
        ] + cast_specs,
        out_specs=[pl.BlockSpec((step_tok, Q_W), lambda s: (s, 0)), cache_spec, cache_spec] + cast_specs,
        out_shape=[
            jax.ShapeDtypeStruct((N_SAMPLE, Q_W), BF16),
            jax.ShapeDtypeStruct((DEC_BATCH, KV_W, WINDOW), F32),
            jax.ShapeDtypeStruct((DEC_BATCH, KV_W, WINDOW), F32),
        ] + cast_shapes,
        scratch_shapes=[pltpu.VMEM((SA_BT, WINDOW, KV_W), F32), pltpu.VMEM((SA_BT, WINDOW, KV_W), F32)],
        compiler_params=pltpu.CompilerParams(
            dimension_semantics=("arbitrary",), vmem_limit_bytes=48 * MIB),
        name="attn_sample",
    )(sinks, z, ckt, cvt, kv, after, *cast_weights)


def _gmlp_kernel(u_ref, gv_ref, lng_ref, w_ref, bias_ref, after_ref, *refs, nchunk, sample, ncast):
    nout = 2 if sample else 1
    a_ref, *gvn_refs = refs[ncast:ncast + nout]
    _cast_blocks(refs[:ncast], refs[ncast + nout:])
    ri = lax.broadcasted_iota(jnp.int32, (CHUNK, CHUNK), 0)
    ci = lax.broadcasted_iota(jnp.int32, (CHUNK, CHUNK), 1)
    nseq = CHUNK // DEC_SEQ
    if sample:
        wmask = ((ri >> 3) == (ci >> 3)) & ((ri & (DEC_SEQ - 1)) >= (ci & (DEC_SEQ - 1)))
        corner_lanes = lax.broadcasted_iota(jnp.int32, (DEC_SEQ, CHUNK), 1) < DEC_SEQ

        def group_w(g):
            w8 = jnp.where(corner_lanes, w_ref[g, 0:DEC_SEQ, :], 0.0)
            shift = DEC_SEQ
            while shift < CHUNK:
                w8 = w8 + pltpu.roll(w8, shift, 1)
                shift *= 2
            return jnp.concatenate([w8] * nseq, axis=0)

        bias = jnp.concatenate([bias_ref[0:DEC_SEQ, :]] * nseq, axis=0)
    else:
        wmask = ri >= ci
        group_w = lambda g: w_ref[g]
        bias = bias_ref[...]
    w = [jnp.where(wmask, group_w(g), 0.0).astype(BF16) for g in range(GROUPS)]
    lng = lng_ref[...]
    for c in range(nchunk):
        rows = slice(c * CHUNK, (c + 1) * CHUNK)
        gv = gv_ref[rows, :].astype(F32)
        xc = gv - jnp.mean(gv, axis=-1, keepdims=True)
        y = xc * lax.rsqrt(jnp.mean(xc * xc, axis=-1, keepdims=True) + EPS) * lng
        if sample:
            gvn_refs[0][rows, :] = y
        yb = y.astype(BF16)
        for g in range(GROUPS):
            cols = slice(g * GROUP_DIM, (g + 1) * GROUP_DIM)
            mixed = _dot(w[g], yb[:, cols]) + bias[:, g:g + 1]
            a_ref[rows, cols] = (u_ref[rows, cols].astype(F32) * mixed).astype(BF16)


def _gmlp(z, lng, w, bias, after, *, sample, cast_weights=()):
    rows = CHUNK if sample else 8 * CHUNK
    nsteps = (N_SAMPLE if sample else N_PROMPT) // rows
    blk0 = (N_PROMPT // rows) if sample else 0
    n_out = N_SAMPLE if sample else N_PROMPT
    out_specs = [pl.BlockSpec((rows, GMLP_W), lambda s: (s, 0))]
    out_shape = [jax.ShapeDtypeStruct((n_out, GMLP_W), BF16)]
    if sample:
        out_specs.append(pl.BlockSpec((rows, GMLP_W), lambda s: (s, 0)))
        out_shape.append(jax.ShapeDtypeStruct((n_out, GMLP_W), F32))
    cast_specs, cast_shapes = _cast_specs(cast_weights, nsteps)
    return pl.pallas_call(
        functools.partial(_gmlp_kernel, nchunk=rows // CHUNK, sample=sample, ncast=len(cast_weights)),
        grid=(nsteps,),
        in_specs=[
            pl.BlockSpec((rows, GMLP_W), lambda s: (blk0 + s, Z_U // GMLP_W)),
            pl.BlockSpec((rows, GMLP_W), lambda s: (blk0 + s, Z_GV // GMLP_W)),
            pl.BlockSpec((1, GMLP_W), lambda s: (0, 0)),
            pl.BlockSpec((GROUPS, CHUNK, CHUNK), lambda s: (0, 0, 0)),
            pl.BlockSpec((CHUNK, GROUPS), lambda s: (0, 0)),
            pl.BlockSpec(memory_space=pl.ANY),
        ] + cast_specs,
        out_specs=out_specs + cast_specs,
        out_shape=out_shape + cast_shapes,
        compiler_params=pltpu.CompilerParams(
            dimension_semantics=("arbitrary",), vmem_limit_bytes=40 * MIB),
        name="gmlp_sample" if sample else "gmlp_prompt",
    )(z, z, lng, w, bias, after, *cast_weights)


MG_TM = 256


def _merge_kernel(x_ref, a_ref, o_ref, ga_ref, gb_ref, wpa_ref, wpb_ref, wo_ref, *refs):
    ncast = (len(refs) - 1) // 2
    x1_ref = refs[ncast]
    _cast_blocks(refs[:ncast], refs[ncast + 1:])
    pa = _dot(a_ref[...], wpa_ref[...])
    pb = _dot(o_ref[...], wpb_ref[...])
    mix = ga_ref[...].astype(F32) * pa + gb_ref[...].astype(F32) * pb
    x1_ref[...] = x_ref[...] + _dot(mix.astype(BF16), wo_ref[...])


def _merge(x, a, o, z, zrow0, wpa, wpb, wo, cast_weights, name):
    nsteps = x.shape[0] // MG_TM
    zblk0 = zrow0 // MG_TM
    cast_specs, cast_shapes = _cast_specs(cast_weights, nsteps)
    row = lambda i: (i, 0)
    const = lambda i: (0, 0)
    resident = dict(pipeline_mode=pl.Buffered(1))
    return pl.pallas_call(
        _merge_kernel,
        grid=(nsteps,),
        in_specs=[
            pl.BlockSpec((MG_TM, D_MODEL), row),
            pl.BlockSpec((MG_TM, GMLP_W), row),
            pl.BlockSpec((MG_TM, Q_W), row),
            pl.BlockSpec((MG_TM, D_MODEL), lambda i: (zblk0 + i, Z_GA // D_MODEL)),
            pl.BlockSpec((MG_TM, D_MODEL), lambda i: (zblk0 + i, Z_GB // D_MODEL)),
            pl.BlockSpec((GMLP_W, D_MODEL), const, **resident),
            pl.BlockSpec((Q_W, D_MODEL), const, **resident),
            pl.BlockSpec((D_MODEL, D_MODEL), const, **resident),
        ] + cast_specs,
        out_specs=[pl.BlockSpec((MG_TM, D_MODEL), row)] + cast_specs,
        out_shape=[jax.ShapeDtypeStruct(x.shape, F32)] + cast_shapes,
        compiler_params=pltpu.CompilerParams(
            dimension_semantics=("arbitrary",), vmem_limit_bytes=56 * MIB),
        name=name,
    )(x, a, o, z, z, wpa, wpb, wo, *cast_weights)


FF_TM = 1024
FF_TF = 512
FF_NJ = D_FF // FF_TF


def _ffn_kernel(x1_ref, n2_ref, wg_ref, wu_ref, wd_ref, fg_ref, y_ref, hh_ref, act_ref):
    j = pl.program_id(1)

    def gate_up(slot):
        hh = hh_ref[...]
        act = jax.nn.silu(_dot(hh, wg_ref[...])) * _dot(hh, wu_ref[...])
        act_ref[slot] = act.astype(BF16)

    def down(slot, first, last):
        acc = (x1_ref if first else y_ref)[...] + _dot(act_ref[slot], wd_ref[...])
        y_ref[...] = _rms(acc, fg_ref[...]) if last else acc

    @pl.when(j == 0)
    def _():
        hh_ref[...] = _rms(x1_ref[...], n2_ref[...]).astype(BF16)
        gate_up(0)

    @pl.when(j == 1)
    def _():
        gate_up(1)
        down(0, True, False)

    @pl.when((j > 1) & (j < FF_NJ))
    def _():
        gate_up(j % 2)
        down(1 - j % 2, False, False)

    @pl.when(j == FF_NJ)
    def _():
        down((FF_NJ - 1) % 2, False, True)


def _ffn(x1, n2, wg, wu, wd, fg, name):
    nrows = x1.shape[0]
    return pl.pallas_call(
        _ffn_kernel,
        grid=(nrows // FF_TM, FF_NJ + 1),
        in_specs=[
            pl.BlockSpec((FF_TM, D_MODEL), lambda i, j: (i, 0)),
            pl.BlockSpec((1, D_MODEL), lambda i, j: (0, 0)),
            pl.BlockSpec((D_MODEL, FF_TF), lambda i, j: (0, jnp.minimum(j, FF_NJ - 1))),
            pl.BlockSpec((D_MODEL, FF_TF), lambda i, j: (0, jnp.minimum(j, FF_NJ - 1))),
            pl.BlockSpec((FF_TF, D_MODEL), lambda i, j: (jnp.maximum(j - 1, 0), 0)),
            pl.BlockSpec((1, D_MODEL), lambda i, j: (0, 0)),
        ],
        out_specs=pl.BlockSpec((FF_TM, D_MODEL), lambda i, j: (i, 0)),
        out_shape=jax.ShapeDtypeStruct((nrows, D_MODEL), F32),
        scratch_shapes=[pltpu.VMEM((FF_TM, D_MODEL), BF16), pltpu.VMEM((2, FF_TM, FF_TF), BF16)],
        compiler_params=pltpu.CompilerParams(
            dimension_semantics=("arbitrary", "arbitrary"), vmem_limit_bytes=58 * MIB),
        name=name,
    )(x1, n2, wg, wu, wd, fg)


def kernel(x_prompt, x_sample, cache_k, cache_v, norm1_g, w_in, gmlp_norm_g, w_s, b_s, sinks,
           w_pa, w_pb, w_o, norm2_g, w_ff_gate, w_ff_up, w_ff_down, final_g):
    xp = x_prompt.reshape(N_PROMPT, D_MODEL)
    xs = x_sample.reshape(N_SAMPLE, D_MODEL)

    z, kv = _inproj(xp, xs, norm1_g.reshape(1, D_MODEL), w_in.reshape(D_MODEL, IN_W))

    lng = gmlp_norm_g.reshape(1, GMLP_W)
    ws = w_s.reshape(GROUPS, CHUNK, CHUNK)
    bias = b_s.reshape(GROUPS, CHUNK).T
    (a_p,) = _gmlp(z, lng, ws, bias, kv, sample=False)
    a_s, gvn_s = _gmlp(z, lng, ws, bias, a_p, sample=True)

    sinks1 = sinks.reshape(N_Q_HEADS)
    o_p, pk, pv, wpa, wpb, wo = _attn_prompt(
        sinks1, z, kv, a_s, (w_pa.reshape(GMLP_W, D_MODEL), w_pb.reshape(Q_W, D_MODEL),
                             w_o.reshape(D_MODEL, D_MODEL)))

    x1_p, wg, wu, wd = _merge(
        xp, a_p, o_p, z, 0, wpa, wpb, wo,
        (w_ff_gate.reshape(D_MODEL, D_FF), w_ff_up.reshape(D_MODEL, D_FF), w_ff_down.reshape(D_FF, D_MODEL)),
        name="merge_prompt")

    o_s, k_win, v_win = _attn_sample(
        sinks1, z, cache_k.reshape(DEC_BATCH, WINDOW, KV_W).transpose(0, 2, 1),
        cache_v.reshape(DEC_BATCH, WINDOW, KV_W).transpose(0, 2, 1), kv, x1_p, ())
    (x1_s,) = _merge(xs, a_s, o_s, z, N_PROMPT, wpa, wpb, wo, (), name="merge_sample")

    ffn_w = (norm2_g.reshape(1, D_MODEL), wg, wu, wd, final_g.reshape(1, D_MODEL))
    y_p = _ffn(x1_p, *ffn_w, name="ffn_prompt")
    y_s = _ffn(x1_s, *ffn_w, name="ffn_sample")

    return (y_p.reshape(BATCH, SEQ, D_MODEL),
            y_s.reshape(DEC_BATCH, DEC_SEQ, D_MODEL),
            pk.reshape(1, BATCH, WINDOW, N_KV_HEADS, HEAD_DIM),
            pv.reshape(1, BATCH, WINDOW, N_KV_HEADS, HEAD_DIM),
            k_win.transpose(0, 2, 1).reshape(1, DEC_BATCH, WINDOW, N_KV_HEADS, HEAD_DIM),
            v_win.transpose(0, 2, 1).reshape(1, DEC_BATCH, WINDOW, N_KV_HEADS, HEAD_DIM),
            gvn_s.reshape(1, DEC_BATCH, DEC_SEQ, GROUPS, GROUP_DIM))
```
